```python
import math
import jax, jax.numpy as jnp
from jax import lax
import numpy as np

D_MODEL = 2048
BATCH = 2
SEQ = 4096
DEPTH = 2
DEC_BATCH = 2
DEC_SEQ = 16384
PAST_LEN = 128

GRID_W = 64
HEAD_DIM = 128
GDN_HEADS = 8
ATT_HEADS = 8
ATT_KV_HEADS = 2
ATT_GROUP = ATT_HEADS // ATT_KV_HEADS
GDN_WIDTH = GDN_HEADS * HEAD_DIM
ATT_WIDTH = ATT_HEADS * HEAD_DIM
ATT_KV_WIDTH = ATT_KV_HEADS * HEAD_DIM
MIX_WIDTH = GDN_WIDTH + ATT_WIDTH
CONV_W = 5
CHUNK = 64
Q_BLOCK = 128
ROPE_THETA = 10000.0
POOL_WINDOWS = (2, 4, 8, 16)
N_POOL_GROUPS = 4
POOL_GROUP = D_MODEL // N_POOL_GROUPS
D_FF = 4 * D_MODEL
N_EVEN = (DEPTH + 1) // 2
N_ODD = DEPTH // 2
DEEPNORM_ALPHA = (2 * DEPTH) ** 0.25
DEEPNORM_BETA = (8 * DEPTH) ** -0.25
NORM_EPS = 1e-6
LN_EPS = 1e-5

GDN_QKV = 3 * GDN_WIDTH
OFF_Z = GDN_QKV
OFF_BETA = OFF_Z + GDN_WIDTH
OFF_A = OFF_BETA + 2 * GDN_HEADS
OFF_AQ = OFF_A + 2 * GDN_HEADS
OFF_AK = OFF_AQ + ATT_WIDTH
OFF_AV = OFF_AK + ATT_KV_WIDTH
D_IN = OFF_AV + ATT_KV_WIDTH

kernel_name = 'hybrid_gdn_axialgqa_pool_encoder'


def _layer_norm(x, g, b):
    xf = x.astype(jnp.float32)
    mu = jnp.mean(xf, -1, keepdims=True)
    xc = xf - mu
    var = jnp.mean(xc * xc, -1, keepdims=True)
    return (xc * lax.rsqrt(var + LN_EPS) * g + b).astype(x.dtype)


def _rms_norm(x, w):
    xf = x.astype(jnp.float32)
    return xf * lax.rsqrt(jnp.mean(xf * xf, -1, keepdims=True) + NORM_EPS) * w.astype(jnp.float32)


def _l2_normalize(x):
    return x * lax.rsqrt(jnp.sum(x * x, -1, keepdims=True) + NORM_EPS)


def _modulate(x, shift, scale):
    return x * (1 + scale[:, None, :]) + shift[:, None, :]


def _short_conv(x, w):
    c = x.shape[-1]
    return lax.conv_general_dilated(x, w[:, None, :], window_strides=(1,),
                                    padding=[(CONV_W // 2, CONV_W // 2)],
                                    dimension_numbers=('NWC', 'WIO', 'NWC'),
                                    feature_group_count=c)


def _gated_delta_rule(q, k, v, g, beta):
    bsz, t, h, dk = k.shape
    dv = v.shape[-1]
    n = t // CHUNK

    def to_chunks(a):
        return a.reshape(bsz, n, CHUNK, h, -1).transpose(0, 3, 1, 2, 4)

    q, k, v = to_chunks(q), to_chunks(k), to_chunks(v)
    beta = beta.reshape(bsz, n, CHUNK, h).transpose(0, 3, 1, 2)
    gc = jnp.cumsum(g.reshape(bsz, n, CHUNK, h).transpose(0, 3, 1, 2), axis=-1)
    idx = jnp.arange(CHUNK)
    incl = idx[:, None] >= idx[None, :]
    strict = idx[:, None] > idx[None, :]
    decay = jnp.exp(jnp.where(incl, gc[..., :, None] - gc[..., None, :], -jnp.inf))
    k_beta = k * beta[..., None]
    v_beta = v * beta[..., None]
    lower = jnp.where(strict, jnp.einsum('bhncd,bhnsd->bhncs', k_beta, k) * decay, 0.0)
    eye = jnp.eye(CHUNK, dtype=jnp.float32)
    a = eye + lower
    tinv = lax.linalg.triangular_solve(a, jnp.broadcast_to(eye, a.shape), left_side=True, lower=True)
    u = jnp.einsum('bhncs,bhnsd->bhncd', tinv, v_beta)
    w = jnp.einsum('bhncs,bhnsd->bhncd', tinv, k_beta * jnp.exp(gc)[..., None])
    qk = jnp.einsum('bhncd,bhnsd->bhncs', q, k) * decay

    def step(state, inp):
        q_i, k_i, u_i, w_i, qk_i, g_i = inp
        v_new = u_i - jnp.einsum('bhck,bhkv->bhcv', w_i, state)
        o_i = (jnp.einsum('bhck,bhkv->bhcv', q_i * jnp.exp(g_i)[..., None], state)
               + jnp.einsum('bhcs,bhsv->bhcv', qk_i, v_new))
        g_last = g_i[..., -1]
        state = (state * jnp.exp(g_last)[..., None, None]
                 + jnp.einsum('bhck,bhcv->bhkv', k_i * jnp.exp(g_last[..., None] - g_i)[..., None], v_new))
        return state, o_i

    xs = tuple(jnp.moveaxis(a_, 2, 0) for a_ in (q, k, u, w, qk, gc))
    state0 = jnp.zeros((bsz, h, dk, dv), jnp.float32)
    _, o = lax.scan(step, state0, xs)
    return o.transpose(1, 0, 3, 2, 4).reshape(bsz, t, h, dv)


def _axial_rope(n_tokens):
    rows = n_tokens // GRID_W
    row = jnp.broadcast_to(jnp.arange(rows)[:, None], (rows, GRID_W)).reshape(-1).astype(jnp.float32)
    col = jnp.broadcast_to(jnp.arange(GRID_W)[None, :], (rows, GRID_W)).reshape(-1).astype(jnp.float32)
    half = HEAD_DIM // 2
    inv_freq = ROPE_THETA ** (-jnp.arange(0, half, 2, dtype=jnp.float32) / half)
    ang_r = row[:, None] * inv_freq
    ang_c = col[:, None] * inv_freq
    ang = jnp.concatenate([ang_r, ang_r, ang_c, ang_c], -1)
    return jnp.cos(ang), jnp.sin(ang)


def _apply_rope(x, cos, sin):
    x1, x2, x3, x4 = jnp.split(x, 4, axis=-1)
    rot = jnp.concatenate([-x2, x1, -x4, x3], -1)
    return x * cos[:, None, :] + rot * sin[:, None, :]


def _block_attention(q, k, v):
    bsz, t = q.shape[0], q.shape[1]
    nb = t // Q_BLOCK
    qb = q.reshape(bsz, nb, Q_BLOCK, ATT_KV_HEADS, ATT_GROUP, HEAD_DIM).transpose(1, 0, 2, 3, 4, 5)
    scale = HEAD_DIM ** -0.5

    def one_block(q_blk):
        s = jnp.einsum('bqkgd,bskd->bkgqs', q_blk, k) * scale
        p = jax.nn.softmax(s, axis=-1)
        return jnp.einsum('bkgqs,bskd->bqkgd', p, v)

    o = lax.map(one_block, qb)
    return o.transpose(1, 0, 2, 3, 4, 5).reshape(bsz, t, ATT_WIDTH)


def _mix_delta_attn(h, w_in, conv_w, a_log, dt_bias, gdn_norm_w, q_norm_w, k_norm_w, w_out):
    bsz, t, _ = h.shape
    proj = h @ w_in
    qkv, z, b_raw, a_raw, aq, ak, av = jnp.split(proj, [OFF_Z, OFF_BETA, OFF_A, OFF_AQ, OFF_AK, OFF_AV], axis=-1)
    qkv = jax.nn.silu(_short_conv(qkv.astype(jnp.float32), conv_w.astype(jnp.float32)))
    gq, gk, gv = [a_.reshape(bsz, t, GDN_HEADS, HEAD_DIM) for a_ in jnp.split(qkv, 3, axis=-1)]
    gq = _l2_normalize(gq) * (HEAD_DIM ** -0.5)
    gk = _l2_normalize(gk)
    beta = jax.nn.sigmoid(b_raw.astype(jnp.float32)).reshape(bsz, t, 2, GDN_HEADS)
    logdec = -jnp.exp(a_log.astype(jnp.float32)) * jax.nn.softplus(
        a_raw.astype(jnp.float32).reshape(bsz, t, 2, GDN_HEADS) + dt_bias.astype(jnp.float32))
    o_fwd = _gated_delta_rule(gq, gk, gv, logdec[:, :, 0], beta[:, :, 0])
    flip = lambda a_: jnp.flip(a_, axis=1)
    o_bwd = flip(_gated_delta_rule(flip(gq), flip(gk), flip(gv), flip(logdec[:, :, 1]), flip(beta[:, :, 1])))
    o_gdn = _rms_norm(o_fwd + o_bwd, gdn_norm_w) * jax.nn.silu(
        z.astype(jnp.float32).reshape(bsz, t, GDN_HEADS, HEAD_DIM))
    o_gdn = o_gdn.reshape(bsz, t, GDN_WIDTH)
    aq = _rms_norm(aq.reshape(bsz, t, ATT_HEADS, HEAD_DIM), q_norm_w)
    ak = _rms_norm(ak.reshape(bsz, t, ATT_KV_HEADS, HEAD_DIM), k_norm_w)
    av = av.astype(jnp.float32).reshape(bsz, t, ATT_KV_HEADS, HEAD_DIM)
    cos, sin = _axial_rope(t)
    o_att = _block_attention(_apply_rope(aq, cos, sin), _apply_rope(ak, cos, sin), av)
    o = jnp.concatenate([o_gdn, o_att], axis=-1).astype(h.dtype)
    return o @ w_out


def _mix_pool(h, pool_w, pool_scale):
    bsz, t, d = h.shape
    hf = h.astype(jnp.float32)
    cs = jnp.concatenate([jnp.zeros((bsz, 1, d), jnp.float32), jnp.cumsum(hf, axis=1)], axis=1)
    pos = np.arange(t)
    outs = []
    for gi, win in enumerate(POOL_WINDOWS):
        lo = np.clip(pos - win // 2, 0, t - 1)
        hi = np.clip(pos + (win - 1 - win // 2), 0, t - 1)
        cnt = jnp.asarray(hi - lo + 1, dtype=jnp.float32)[None, :, None]
        sl = slice(gi * POOL_GROUP, (gi + 1) * POOL_GROUP)
        csg = cs[..., sl]
        mean = (csg[:, hi + 1] - csg[:, lo]) / cnt
        outs.append((mean - hf[..., sl]) @ pool_w[gi].astype(jnp.float32))
    return (jnp.concatenate(outs, axis=-1) * pool_scale.astype(jnp.float32)).astype(h.dtype)


def _mlp(h, w1, w2):
    u = jnp.maximum(h @ w1, 0)
    return (u * u) @ w2


def _trunk(x, c, w_in, conv_w, a_log, dt_bias, gdn_norm_w, q_norm_w, k_norm_w, w_out,
           pool_w, pool_scale, mlp_w1, mlp_w2, ada_w, ada_b, ln_g, ln_b):
    for l in range(DEPTH):
        cond = jax.nn.silu(c) @ ada_w[l] + ada_b[l]
        sh_m, sc_m, g_m, sh_f, sc_f, g_f = jnp.split(cond, 6, axis=-1)
        h = _modulate(x, sh_m, sc_m)
        if l % 2 == 0:
            j = l // 2
            m = _mix_delta_attn(h, w_in[j], conv_w[j], a_log[j], dt_bias[j], gdn_norm_w[j],
                                q_norm_w[j], k_norm_w[j], w_out[j])
        else:
            j = l // 2
            m = _mix_pool(h, pool_w[j], pool_scale[j])
        x = _layer_norm(DEEPNORM_ALPHA * x + g_m[:, None, :] * m, ln_g[l, 0], ln_b[l, 0])
        h = _modulate(x, sh_f, sc_f)
        x = _layer_norm(DEEPNORM_ALPHA * x + g_f[:, None, :] * _mlp(h, mlp_w1[l], mlp_w2[l]),
                        ln_g[l, 1], ln_b[l, 1])
    return x


def setup_inputs(seed: int = 0) -> dict:
    key = jax.random.key(seed)
    ks = jax.random.split(key, 20)
    f32 = jnp.float32
    nrm = lambda k_, shape, s: jax.random.normal(k_, shape, f32) * s
    return {
        'x_prompt': nrm(ks[0], (BATCH, SEQ, D_MODEL), 1.0),
        'x_sample': nrm(ks[1], (DEC_BATCH, DEC_SEQ, D_MODEL), 1.0),
        'c_prompt': nrm(ks[2], (BATCH, D_MODEL), 1.0),
        'c_sample': nrm(ks[3], (DEC_BATCH, D_MODEL), 1.0),
        'w_in': nrm(ks[4], (N_EVEN, D_MODEL, D_IN), D_MODEL ** -0.5),
        'conv_w': nrm(ks[5], (N_EVEN, CONV_W, GDN_QKV), CONV_W ** -0.5),
        'a_log': jnp.log(jax.random.uniform(ks[6], (N_EVEN, 2, GDN_HEADS), f32, 1.0, 16.0)),
        'dt_bias': nrm(ks[7], (N_EVEN, 2, GDN_HEADS), 0.1),
        'gdn_norm_w': 1.0 + nrm(ks[8], (N_EVEN, HEAD_DIM), 0.05),
        'q_norm_w': 1.0 + nrm(ks[9], (N_EVEN, HEAD_DIM), 0.05),
        'k_norm_w': 1.0 + nrm(ks[10], (N_EVEN, HEAD_DIM), 0.05),
        'w_out': nrm(ks[11], (N_EVEN, MIX_WIDTH, D_MODEL), MIX_WIDTH ** -0.5 * DEEPNORM_BETA),
        'pool_w': nrm(ks[12], (N_ODD, N_POOL_GROUPS, POOL_GROUP, POOL_GROUP), POOL_GROUP ** -0.5 * DEEPNORM_BETA),
        'pool_scale': 1.0 + nrm(ks[13], (N_ODD, D_MODEL), 0.05),
        'mlp_w1': nrm(ks[14], (DEPTH, D_MODEL, D_FF), D_MODEL ** -0.5),
        'mlp_w2': nrm(ks[15], (DEPTH, D_FF, D_MODEL), D_FF ** -0.5 * DEEPNORM_BETA),
        'ada_w': nrm(ks[16], (DEPTH, D_MODEL, 6 * D_MODEL), D_MODEL ** -0.5),
        'ada_b': nrm(ks[17], (DEPTH, 6 * D_MODEL), 0.02),
        'ln_g': 1.0 + nrm(ks[18], (DEPTH, 2, D_MODEL), 0.05),
        'ln_b': nrm(ks[19], (DEPTH, 2, D_MODEL), 0.02),
    }


def reference(x_prompt, x_sample, c_prompt, c_sample, w_in, conv_w, a_log, dt_bias, gdn_norm_w,
              q_norm_w, k_norm_w, w_out, pool_w, pool_scale, mlp_w1, mlp_w2, ada_w, ada_b, ln_g, ln_b):
    y_prompt = _trunk(x_prompt, c_prompt, w_in, conv_w, a_log, dt_bias, gdn_norm_w, q_norm_w, k_norm_w,
                      w_out, pool_w, pool_scale, mlp_w1, mlp_w2, ada_w, ada_b, ln_g, ln_b)
    y_sample = _trunk(x_sample, c_sample, w_in, conv_w, a_log, dt_bias, gdn_norm_w, q_norm_w, k_norm_w,
                      w_out, pool_w, pool_scale, mlp_w1, mlp_w2, ada_w, ada_b, ln_g, ln_b)
    return (y_prompt, y_sample)
```

```python
import functools

import jax
import jax.numpy as jnp
from jax import lax
from jax.experimental import pallas as pl
from jax.experimental.pallas import tpu as pltpu

F32 = jnp.float32
BF16 = jnp.bfloat16

D_MODEL = 2048
DEPTH = 2
GRID_W = 64
HEAD_DIM = 128
GDN_HEADS = 8
ATT_HEADS = 8
ATT_KV_HEADS = 2
ATT_GROUP = ATT_HEADS // ATT_KV_HEADS
GDN_WIDTH = GDN_HEADS * HEAD_DIM
ATT_WIDTH = ATT_HEADS * HEAD_DIM
ATT_KV_WIDTH = ATT_KV_HEADS * HEAD_DIM
CONV_W = 5
ROPE_THETA = 10000.0
POOL_WINDOWS = (2, 4, 8, 16)
N_POOL_GROUPS = 4
POOL_GROUP = D_MODEL // N_POOL_GROUPS
D_FF = 4 * D_MODEL
DEEPNORM_ALPHA = (2 * DEPTH) ** 0.25
NORM_EPS = 1e-6
LN_EPS = 1e-5

GDN_QKV = 3 * GDN_WIDTH
OFF_Z = GDN_QKV
OFF_BETA = OFF_Z + GDN_WIDTH
OFF_A = OFF_BETA + 2 * GDN_HEADS
OFF_AQ = OFF_A + 2 * GDN_HEADS
OFF_AK = OFF_AQ + ATT_WIDTH
OFF_AV = OFF_AK + ATT_KV_WIDTH
D_IN = OFF_AV + ATT_KV_WIDTH

PM_QKV = 0
PM_Z = GDN_QKV
PM_AQ = PM_Z + GDN_WIDTH
PM_AK = PM_AQ + ATT_WIDTH
PM_AV = PM_AK + ATT_KV_WIDTH
PM_WIDTH = PM_AV + ATT_KV_WIDTH

LANES = 128
SUBLANES = 8
BF16_ROWS = 16
V7X_VMEM_BYTES = 64 * 1024 * 1024

GDN_CHUNK = 128
N_GATE_COLS = 2 * GDN_HEADS

MOD_SH_M, MOD_SC_M, MOD_G_M, MOD_SH_F, MOD_SC_F, MOD_G_F = range(6)
MOD_ROWS = 8


def _vmem_limit(block_bytes, scratch_bytes=0, temp_bytes=0):
    need = 2 * block_bytes + scratch_bytes + temp_bytes
    return int(min(need + need // 4, V7X_VMEM_BYTES - 8 * 1024 * 1024))


def _nbytes(shape, dtype):
    n = 1
    for s in shape:
        n *= s
    return n * jnp.dtype(dtype).itemsize


def _sigmoid(x):
    return 1.0 / (1.0 + jnp.exp(-x))


def _silu(x):
    return x * _sigmoid(x)


def _layer_norm_rows(y, g, b):
    mu = jnp.mean(y, axis=-1, keepdims=True)
    yc = y - mu
    var = jnp.mean(yc * yc, axis=-1, keepdims=True)
    return yc * lax.rsqrt(var + LN_EPS) * g + b


def _dot(a, b):
    return jnp.dot(a, b, preferred_element_type=F32)


def _dot_nt(a, b):
    return lax.dot_general(a, b, (((1,), (1,)), ((), ())), preferred_element_type=F32)


def _ada_kernel(c_ref, w_ref, b_ref, o_ref):
    c = c_ref[...]
    s = _silu(c).astype(BF16)
    o_ref[...] = _dot(s, w_ref[...].astype(BF16)) + b_ref[...]


def _ada_call(c_all, ada_w, ada_b):
    rows = c_all.shape[0]
    n = ada_w.shape[-1]
    tn = 1024
    blocks = _nbytes((D_MODEL, tn), F32) + _nbytes((rows, D_MODEL), F32) + 2 * _nbytes((rows, tn), F32)
    return pl.pallas_call(
        _ada_kernel,
        grid=(DEPTH, n // tn),
        in_specs=[
            pl.BlockSpec((rows, D_MODEL), lambda l, j: (0, 0)),
            pl.BlockSpec((None, D_MODEL, tn), lambda l, j: (l, 0, j)),
            pl.BlockSpec((None, 1, tn), lambda l, j: (l, 0, j)),
        ],
        out_specs=pl.BlockSpec((None, rows, tn), lambda l, j: (l, 0, j)),
        out_shape=jax.ShapeDtypeStruct((DEPTH, rows, n), F32),
        compiler_params=pltpu.CompilerParams(
            dimension_semantics=("parallel", "parallel"),
            vmem_limit_bytes=_vmem_limit(blocks, temp_bytes=_nbytes((D_MODEL, tn), BF16))),
        name="ada",
    )(c_all, ada_w, ada_b.reshape(DEPTH, 1, n))


def _inproj_kernel(x_ref, mod_ref, w_ref, wg_ref, o_ref, g_ref, h_ref):
    @pl.when(pl.program_id(1) == 0)
    def _():
        m = mod_ref[...]
        h = x_ref[...] * (1.0 + m[MOD_SC_M:MOD_SC_M + 1, :]) + m[MOD_SH_M:MOD_SH_M + 1, :]
        hb = h.astype(BF16)
        h_ref[...] = hb
        g_ref[...] = _dot(hb, wg_ref[...])

    o_ref[...] = _dot(h_ref[...], w_ref[...]).astype(o_ref.dtype)


def _inproj_call(x2, mod, w_main, w_gate, seq):
    m = x2.shape[0]
    tm, tn = 512, 512
    blocks = (_nbytes((tm, D_MODEL), F32) + _nbytes((MOD_ROWS, D_MODEL), F32) + _nbytes((D_MODEL, tn), BF16)
              + _nbytes((D_MODEL, LANES), BF16) + _nbytes((tm, tn), BF16) + _nbytes((tm, LANES), F32))
    scratch = _nbytes((tm, D_MODEL), BF16)
    return pl.pallas_call(
        _inproj_kernel,
        grid=(m // tm, PM_WIDTH // tn),
        in_specs=[
            pl.BlockSpec((tm, D_MODEL), lambda i, j: (i, 0)),
            pl.BlockSpec((None, MOD_ROWS, D_MODEL), lambda i, j: ((i * tm) // seq, 0, 0)),
            pl.BlockSpec((D_MODEL, tn), lambda i, j: (0, j)),
            pl.BlockSpec((D_MODEL, LANES), lambda i, j: (0, 0)),
        ],
        out_specs=[
            pl.BlockSpec((tm, tn), lambda i, j: (i, j)),
            pl.BlockSpec((tm, LANES), lambda i, j: (i, 0)),
        ],
        out_shape=[
            jax.ShapeDtypeStruct((m, PM_WIDTH), BF16),
            jax.ShapeDtypeStruct((m, LANES), F32),
        ],
        scratch_shapes=[pltpu.VMEM((tm, D_MODEL), BF16)],
        compiler_params=pltpu.CompilerParams(
            dimension_semantics=("parallel", "arbitrary"),
            vmem_limit_bytes=_vmem_limit(blocks, scratch, _nbytes((tm, D_MODEL), F32))),
        name="inproj",
    )(x2, mod, w_main, w_gate)


def _shifted_rows(x, halo8, shift, n_rows):
    k = abs(shift)
    rolled = pltpu.roll(x, (-shift) % n_rows, 0)
    r8 = lax.broadcasted_iota(jnp.int32, (SUBLANES, x.shape[1]), 0)
    if shift < 0:
        fix = pltpu.roll(halo8, k, 0)
        head = jnp.where(r8 < k, fix, rolled[:SUBLANES])
        return jnp.concatenate([head, rolled[SUBLANES:]], axis=0)
    fix = pltpu.roll(halo8, SUBLANES - k, 0)
    tail = jnp.where(r8 >= SUBLANES - k, fix, rolled[n_rows - SUBLANES:])
    return jnp.concatenate([rolled[:n_rows - SUBLANES], tail], axis=0)


def _gdnprep_kernel(x_ref, xp_ref, xn_ref, gate_ref, cw_ref, gp_ref,
                    q_ref, k_ref, v_ref, gcol_ref, grow_ref, *, tt):
    i = pl.program_id(1)
    first = i == 0
    last = i == pl.num_programs(1) - 1
    cw = cw_ref[...]
    for part, out_ref in enumerate((q_ref, k_ref, v_ref)):
        cols = slice(part * GDN_WIDTH, (part + 1) * GDN_WIDTH)
        x = x_ref[:, cols].astype(F32)
        prev8 = xp_ref[:, cols].astype(F32)[BF16_ROWS - SUBLANES:]
        next8 = xn_ref[:, cols].astype(F32)[:SUBLANES]
        prev8 = jnp.where(first, 0.0, prev8)
        next8 = jnp.where(last, 0.0, next8)
        w = cw[:, cols]
        acc = x * w[CONV_W // 2:CONV_W // 2 + 1, :]
        for tap in range(CONV_W):
            shift = tap - CONV_W // 2
            if shift == 0:
                continue
            halo = prev8 if shift < 0 else next8
            acc = acc + _shifted_rows(x, halo, shift, tt) * w[tap:tap + 1, :]
        y = _silu(acc)
        if part < 2:
            heads = []
            for h in range(GDN_HEADS):
                yh = y[:, h * HEAD_DIM:(h + 1) * HEAD_DIM]
                inv = lax.rsqrt(jnp.sum(yh * yh, axis=-1, keepdims=True) + NORM_EPS)
                if part == 0:
                    inv = inv * (HEAD_DIM ** -0.5)
                heads.append(yh * inv)
            y = jnp.concatenate(heads, axis=1)
        out_ref[...] = y.astype(out_ref.dtype)

    raw = gate_ref[...]
    gp = gp_ref[...]
    col = lax.broadcasted_iota(jnp.int32, raw.shape, 1)
    beta = _sigmoid(raw)
    z = raw + gp[0:1, :]
    softplus = jnp.maximum(z, 0.0) + jnp.log(1.0 + jnp.exp(-jnp.abs(z)))
    logdec = -jnp.exp(gp[1:2, :]) * softplus
    is_dec = (col >= N_GATE_COLS) & (col < 2 * N_GATE_COLS)
    gsrc = jnp.where(is_dec, logdec, 0.0)
    r = lax.broadcasted_iota(jnp.int32, (tt, tt), 0)
    c = lax.broadcasted_iota(jnp.int32, (tt, tt), 1)
    same = (r // GDN_CHUNK) == (c // GDN_CHUNK)
    p_lo = jnp.where(same & (c <= r), 1.0, 0.0).astype(F32)
    p_up = jnp.where(same & (c >= r), 1.0, 0.0).astype(F32)
    cum_lo = jnp.dot(p_lo, gsrc, precision=lax.Precision.HIGHEST, preferred_element_type=F32)
    cum_up = jnp.dot(p_up, gsrc, precision=lax.Precision.HIGHEST, preferred_element_type=F32)
    total = cum_lo + cum_up - gsrc
    fwd_col = col < N_GATE_COLS + GDN_HEADS
    gc = jnp.where(fwd_col, cum_lo, cum_up)
    tot_shift = pltpu.roll(total, N_GATE_COLS, 1)
    gcol = jnp.where(col < N_GATE_COLS, beta,
                     jnp.where(col < 2 * N_GATE_COLS, gc,
                               jnp.where(col < 3 * N_GATE_COLS, tot_shift, 0.0)))
    gcol_ref[...] = gcol
    grow_ref[...] = gcol.T


def _gdnprep_call(pm, gates, conv_w8, gate_params):
    bsz, seq, _ = pm.shape
    tt = 256
    hb = tt // BF16_ROWS
    n_halo = seq // BF16_ROWS
    blocks = (_nbytes((tt, GDN_QKV), BF16) + 2 * _nbytes((BF16_ROWS, GDN_QKV), BF16) + _nbytes((tt, LANES), F32)
              + _nbytes((SUBLANES, GDN_QKV), F32) + 3 * _nbytes((tt, GDN_WIDTH), BF16) + 2 * _nbytes((tt, LANES), F32))
    temps = 8 * _nbytes((tt, GDN_WIDTH), F32) + 4 * _nbytes((tt, tt), F32)
    return pl.pallas_call(
        functools.partial(_gdnprep_kernel, tt=tt),
        grid=(bsz, seq // tt),
        in_specs=[
            pl.BlockSpec((None, tt, GDN_QKV), lambda b, i: (b, i, 0)),
            pl.BlockSpec((None, BF16_ROWS, GDN_QKV), lambda b, i: (b, jnp.maximum(i * hb - 1, 0), 0)),
            pl.BlockSpec((None, BF16_ROWS, GDN_QKV), lambda b, i: (b, jnp.minimum((i + 1) * hb, n_halo - 1), 0)),
            pl.BlockSpec((None, tt, LANES), lambda b, i: (b, i, 0)),
            pl.BlockSpec((SUBLANES, GDN_QKV), lambda b, i: (0, 0)),
            pl.BlockSpec((SUBLANES, LANES), lambda b, i: (0, 0)),
        ],
        out_specs=[
            pl.BlockSpec((None, tt, GDN_WIDTH), lambda b, i: (b, i, 0)),
            pl.BlockSpec((None, tt, GDN_WIDTH), lambda b, i: (b, i, 0)),
            pl.BlockSpec((None, tt, GDN_WIDTH), lambda b, i: (b, i, 0)),
            pl.BlockSpec((None, tt, LANES), lambda b, i: (b, i, 0)),
            pl.BlockSpec((None, LANES, tt), lambda b, i: (b, 0, i)),
        ],
        out_shape=[
            jax.ShapeDtypeStruct((bsz, seq, GDN_WIDTH), BF16),
            jax.ShapeDtypeStruct((bsz, seq, GDN_WIDTH), BF16),
            jax.ShapeDtypeStruct((bsz, seq, GDN_WIDTH), BF16),
            jax.ShapeDtypeStruct((bsz, seq, LANES), F32),
            jax.ShapeDtypeStruct((bsz, LANES, seq), F32),
        ],
        compiler_params=pltpu.CompilerParams(
            dimension_semantics=("parallel", "parallel"),
            vmem_limit_bytes=_vmem_limit(blocks, temp_bytes=temps)),
        name="gdnprep",
    )(pm, pm, pm, gates, conv_w8, gate_params)


def _gdn_unit(q, k, v, beta, gcc, gtc, gcr, gtr, state, reverse):
    n = GDN_CHUNK
    r = lax.broadcasted_iota(jnp.int32, (n, n), 0)
    c = lax.broadcasted_iota(jnp.int32, (n, n), 1)
    incl = (r <= c) if reverse else (r >= c)
    strict = (r < c) if reverse else (r > c)
    decay = jnp.where(incl, jnp.exp(jnp.where(incl, gcc - gcr, 0.0)), 0.0)

    kf = k.astype(F32)
    kb = kf * beta
    vb = (v.astype(F32) * beta).astype(BF16)
    aq = _dot_nt(jnp.concatenate([kb.astype(BF16), q], axis=0), k)
    low = jnp.where(strict, aq[:n] * decay, 0.0)
    qk = aq[n:] * decay

    eye = jnp.where(r == c, 1.0, 0.0).astype(F32)
    tinv = eye - low
    lb = low.astype(BF16)
    power = _dot(lb, lb)
    n_factors = (n - 1).bit_length() - 1
    for it in range(n_factors):
        pb = power.astype(BF16)
        if it + 1 < n_factors:
            both = _dot(jnp.concatenate([tinv.astype(BF16), pb], axis=0), pb)
            tinv = tinv + both[:n]
            power = both[n:]
        else:
            tinv = tinv + _dot(tinv.astype(BF16), pb)

    egc = jnp.exp(gcc)
    kbg = (kb * egc).astype(BF16)
    uw = _dot(tinv.astype(BF16), jnp.concatenate([vb, kbg], axis=1))
    u = uw[:, :HEAD_DIM]
    w = uw[:, HEAD_DIM:]
    qg = (q.astype(F32) * egc).astype(BF16)
    ws = _dot(jnp.concatenate([w.astype(BF16), qg], axis=0), state.astype(BF16))
    v_new = (u - ws[:n]).astype(BF16)
    kdec_t = (kf * jnp.exp(gtc - gcc)).T.astype(BF16)
    os_ = _dot(jnp.concatenate([qk.astype(BF16), kdec_t], axis=0), v_new)
    out = ws[n:] + os_[:n]
    new_state = state * jnp.exp(gtr) + os_[n:]
    return out, new_state


def _gdn_kernel(qf_ref, kf_ref, vf_ref, gcf_ref, grf_ref,
                qb_ref, kb_ref, vb_ref, gcb_ref, grb_ref,
                of_ref, ob_ref, state_ref):
    @pl.when(pl.program_id(1) == 0)
    def _():
        state_ref[...] = jnp.zeros_like(state_ref)

    for d, (q_ref, k_ref, v_ref, gc_ref, gr_ref, o_ref) in enumerate((
            (qf_ref, kf_ref, vf_ref, gcf_ref, grf_ref, of_ref),
            (qb_ref, kb_ref, vb_ref, gcb_ref, grb_ref, ob_ref))):
        gcol = gc_ref[...]
        grow = gr_ref[...]
        for h in range(GDN_HEADS):
            lanes = slice(h * HEAD_DIM, (h + 1) * HEAD_DIM)
            j = d * GDN_HEADS + h
            out, new_state = _gdn_unit(
                q_ref[:, lanes], k_ref[:, lanes], v_ref[:, lanes],
                gcol[:, j:j + 1],
                gcol[:, N_GATE_COLS + j:N_GATE_COLS + j + 1],
                gcol[:, 2 * N_GATE_COLS + j:2 * N_GATE_COLS + j + 1],
                grow[N_GATE_COLS + j:N_GATE_COLS + j + 1, :],
                grow[2 * N_GATE_COLS + j:2 * N_GATE_COLS + j + 1, :],
                state_ref[d, h], reverse=(d == 1))
            o_ref[:, lanes] = out
            state_ref[d, h] = new_state


def _gdn_call(q, k, v, gcol, grow):
    bsz, seq, _ = q.shape
    tb = GDN_CHUNK
    nb = seq // tb
    qkv_spec_f = pl.BlockSpec((None, tb, GDN_WIDTH), lambda b, i: (b, i, 0))
    qkv_spec_b = pl.BlockSpec((None, tb, GDN_WIDTH), lambda b, i: (b, nb - 1 - i, 0))
    gc_spec_f = pl.BlockSpec((None, tb, LANES), lambda b, i: (b, i, 0))
    gc_spec_b = pl.BlockSpec((None, tb, LANES), lambda b, i: (b, nb - 1 - i, 0))
    gr_spec_f = pl.BlockSpec((None, LANES, tb), lambda b, i: (b, 0, i))
    gr_spec_b = pl.BlockSpec((None, LANES, tb), lambda b, i: (b, 0, nb - 1 - i))
    blocks = 2 * (3 * _nbytes((tb, GDN_WIDTH), BF16) + 2 * _nbytes((tb, LANES), F32) + _nbytes((tb, GDN_WIDTH), F32))
    scratch = _nbytes((2, GDN_HEADS, HEAD_DIM, HEAD_DIM), F32)
    temps = 2 * GDN_HEADS * 24 * _nbytes((GDN_CHUNK, LANES), F32)
    return pl.pallas_call(
        _gdn_kernel,
        grid=(bsz, nb),
        in_specs=[qkv_spec_f, qkv_spec_f, qkv_spec_f, gc_spec_f, gr_spec_f,
                  qkv_spec_b, qkv_spec_b, qkv_spec_b, gc_spec_b, gr_spec_b],
        out_specs=[qkv_spec_f, qkv_spec_b],
        out_shape=[jax.ShapeDtypeStruct((bsz, seq, GDN_WIDTH), F32),
                   jax.ShapeDtypeStruct((bsz, seq, GDN_WIDTH), F32)],
        scratch_shapes=[pltpu.VMEM((2, GDN_HEADS, HEAD_DIM, HEAD_DIM), F32)],
        compiler_params=pltpu.CompilerParams(
            dimension_semantics=("parallel", "arbitrary"),
            vmem_limit_bytes=_vmem_limit(blocks, scratch, temps)),
        name="gdn",
    )(q, k, v, gcol, grow, q, k, v, gcol, grow)


def _rope_tables(seq):
    t = jnp.arange(seq)
    row = (t // GRID_W).astype(F32)
    col = (t % GRID_W).astype(F32)
    half = HEAD_DIM // 2
    inv_freq = ROPE_THETA ** (-jnp.arange(0, half, 2, dtype=F32) / half)
    ang_r = row[:, None] * inv_freq
    ang_c = col[:, None] * inv_freq
    ang = jnp.concatenate([ang_r, ang_r, ang_c, ang_c], -1)
    cos, sin = jnp.cos(ang), jnp.sin(ang)
    lane = jnp.arange(HEAD_DIM)
    first = (lane // (HEAD_DIM // 4)) % 2 == 0
    sin_up = jnp.where(first, -sin, 0.0)
    sin_dn = jnp.where(first, 0.0, sin)
    return cos, sin_up, sin_dn


def _norm_rope(x, w, cos, sin_up, sin_dn, scale):
    xn = x * lax.rsqrt(jnp.mean(x * x, axis=-1, keepdims=True) + NORM_EPS) * w
    quarter = HEAD_DIM // 4
    up = pltpu.roll(xn, HEAD_DIM - quarter, 1)
    dn = pltpu.roll(xn, quarter, 1)
    y = xn * cos + up * sin_up + dn * sin_dn
    return y * scale if scale != 1.0 else y


def _attprep_kernel(aq_ref, ak_ref, cos_ref, su_ref, sd_ref, qw_ref, kw_ref, q_ref, k_ref):
    cos, su, sd = cos_ref[...], su_ref[...], sd_ref[...]
    qw, kw = qw_ref[...], kw_ref[...]
    for h in range(ATT_HEADS):
        lanes = slice(h * HEAD_DIM, (h + 1) * HEAD_DIM)
        q_ref[:, lanes] = _norm_rope(aq_ref[:, lanes].astype(F32), qw, cos, su, sd,
                                     HEAD_DIM ** -0.5).astype(q_ref.dtype)
    for h in range(ATT_KV_HEADS):
        lanes = slice(h * HEAD_DIM, (h + 1) * HEAD_DIM)
        k_ref[:, lanes] = _norm_rope(ak_ref[:, lanes].astype(F32), kw, cos, su, sd, 1.0).astype(k_ref.dtype)


def _attprep_call(pm, cos, sin_up, sin_dn, q_norm_w, k_norm_w):
    bsz, seq, _ = pm.shape
    tt = 512
    tab = pl.BlockSpec((tt, HEAD_DIM), lambda b, i: (i, 0))
    vec = pl.BlockSpec((1, HEAD_DIM), lambda b, i: (0, 0))
    blocks = (2 * _nbytes((tt, ATT_WIDTH), BF16) + 2 * _nbytes((tt, ATT_KV_WIDTH), BF16)
              + 3 * _nbytes((tt, HEAD_DIM), F32))
    return pl.pallas_call(
        _attprep_kernel,
        grid=(bsz, seq // tt),
        in_specs=[
            pl.BlockSpec((None, tt, ATT_WIDTH), lambda b, i: (b, i, PM_AQ // ATT_WIDTH)),
            pl.BlockSpec((None, tt, ATT_KV_WIDTH), lambda b, i: (b, i, PM_AK // ATT_KV_WIDTH)),
            tab, tab, tab, vec, vec,
        ],
        out_specs=[
            pl.BlockSpec((None, tt, ATT_WIDTH), lambda b, i: (b, i, 0)),
            pl.BlockSpec((None, tt, ATT_KV_WIDTH), lambda b, i: (b, i, 0)),
        ],
        out_shape=[jax.ShapeDtypeStruct((bsz, seq, ATT_WIDTH), BF16),
                   jax.ShapeDtypeStruct((bsz, seq, ATT_KV_WIDTH), BF16)],
        compiler_params=pltpu.CompilerParams(
            dimension_semantics=("parallel", "parallel"),
            vmem_limit_bytes=_vmem_limit(blocks, temp_bytes=8 * _nbytes((tt, HEAD_DIM), F32))),
        name="attprep",
    )(pm, pm, cos, sin_up, sin_dn, q_norm_w.reshape(1, HEAD_DIM), k_norm_w.reshape(1, HEAD_DIM))


def _flash_kernel(q_ref, k_ref, v_ref, o_ref, m_ref, l_ref, acc_ref, *, tq, tk, n_k):
    q = jnp.concatenate([q_ref[:, g * HEAD_DIM:(g + 1) * HEAD_DIM] for g in range(ATT_GROUP)], axis=0)
    m_ref[...] = jnp.full_like(m_ref, -jnp.inf)
    l_ref[...] = jnp.zeros_like(l_ref)
    acc_ref[...] = jnp.zeros_like(acc_ref)

    def body(j, carry):
        start = pl.multiple_of(j * tk, tk)
        kj = k_ref[pl.ds(start, tk), :]
        vj = v_ref[pl.ds(start, tk), :]
        s = _dot_nt(q, kj)
        m_prev = m_ref[...]
        m_new = jnp.maximum(m_prev, jnp.max(s, axis=-1, keepdims=True))
        p = jnp.exp(s - m_new)
        alpha = jnp.exp(m_prev - m_new)
        l_ref[...] = alpha * l_ref[...] + jnp.sum(p, axis=-1, keepdims=True)
        acc_ref[...] = alpha * acc_ref[...] + _dot(p.astype(BF16), vj)
        m_ref[...] = m_new
        return carry

    lax.fori_loop(0, n_k, body, 0)
    out = acc_ref[...] / l_ref[...]
    for g in range(ATT_GROUP):
        o_ref[:, g * HEAD_DIM:(g + 1) * HEAD_DIM] = out[g * tq:(g + 1) * tq].astype(o_ref.dtype)


def _flash_call(qr, kr, pm):
    bsz, seq, _ = qr.shape
    tq, tk = 256, 512
    gw = ATT_GROUP * HEAD_DIM
    rows = ATT_GROUP * tq
    blocks = 2 * _nbytes((tq, gw), BF16) + 2 * _nbytes((seq, HEAD_DIM), BF16)
    scratch = 2 * _nbytes((rows, LANES), F32) + _nbytes((rows, HEAD_DIM), F32)
    temps = 3 * _nbytes((rows, tk), F32)
    return pl.pallas_call(
        functools.partial(_flash_kernel, tq=tq, tk=tk, n_k=seq // tk),
        grid=(bsz, ATT_KV_HEADS, seq // tq),
        in_specs=[
            pl.BlockSpec((None, tq, gw), lambda b, h, i: (b, i, h)),
            pl.BlockSpec((None, seq, HEAD_DIM), lambda b, h, i: (b, 0, h)),
            pl.BlockSpec((None, seq, HEAD_DIM), lambda b, h, i: (b, 0, PM_AV // HEAD_DIM + h)),
        ],
        out_specs=pl.BlockSpec((None, tq, gw), lambda b, h, i: (b, i, h)),
        out_shape=jax.ShapeDtypeStruct((bsz, seq, ATT_WIDTH), BF16),
        scratch_shapes=[pltpu.VMEM((rows, 1), F32), pltpu.VMEM((rows, 1), F32),
                        pltpu.VMEM((rows, HEAD_DIM), F32)],
        compiler_params=pltpu.CompilerParams(
            dimension_semantics=("parallel", "parallel", "arbitrary"),
            vmem_limit_bytes=_vmem_limit(blocks, scratch, temps)),
        name="flash",
    )(qr, kr, pm)


def _outproj_kernel(of_ref, ob_ref, z_ref, oa_ref, x_ref, mod_ref, w_ref, nw_ref, lg_ref, lb_ref, o_ref):
    nw = nw_ref[...]
    heads = []
    for h in range(GDN_HEADS):
        lanes = slice(h * HEAD_DIM, (h + 1) * HEAD_DIM)
        o = of_ref[:, lanes] + ob_ref[:, lanes]
        on = o * lax.rsqrt(jnp.mean(o * o, axis=-1, keepdims=True) + NORM_EPS) * nw
        heads.append((on * _silu(z_ref[:, lanes].astype(F32))).astype(BF16))
    og = jnp.concatenate(heads, axis=1)
    mix = _dot(og, w_ref[:GDN_WIDTH, :]) + _dot(oa_ref[...], w_ref[GDN_WIDTH:, :])
    m = mod_ref[...]
    y = DEEPNORM_ALPHA * x_ref[...] + m[MOD_G_M:MOD_G_M + 1, :] * mix
    o_ref[...] = _layer_norm_rows(y, lg_ref[...], lb_ref[...])


def _outproj_call(o_f, o_b, pm2, o_att, x2, mod, w_out, gdn_norm_w, ln_g, ln_b, seq):
    m = x2.shape[0]
    tm = 256
    blocks = (2 * _nbytes((tm, GDN_WIDTH), F32) + 2 * _nbytes((tm, GDN_WIDTH), BF16)
              + 2 * _nbytes((tm, D_MODEL), F32) + _nbytes((MOD_ROWS, D_MODEL), F32)
              + _nbytes((D_MODEL, D_MODEL), BF16))
    vec = pl.BlockSpec((1, D_MODEL), lambda i: (0, 0))
    return pl.pallas_call(
        _outproj_kernel,
        grid=(m // tm,),
        in_specs=[
            pl.BlockSpec((tm, GDN_WIDTH), lambda i: (i, 0)),
            pl.BlockSpec((tm, GDN_WIDTH), lambda i: (i, 0)),
            pl.BlockSpec((tm, GDN_WIDTH), lambda i: (i, PM_Z // GDN_WIDTH)),
            pl.BlockSpec((tm, ATT_WIDTH), lambda i: (i, 0)),
            pl.BlockSpec((tm, D_MODEL), lambda i: (i, 0)),
            pl.BlockSpec((None, MOD_ROWS, D_MODEL), lambda i: ((i * tm) // seq, 0, 0)),
            pl.BlockSpec((D_MODEL, D_MODEL), lambda i: (0, 0)),
            pl.BlockSpec((1, HEAD_DIM), lambda i: (0, 0)),
            vec, vec,
        ],
        out_specs=pl.BlockSpec((tm, D_MODEL), lambda i: (i, 0)),
        out_shape=jax.ShapeDtypeStruct((m, D_MODEL), F32),
        compiler_params=pltpu.CompilerParams(
            dimension_semantics=("parallel",),
            vmem_limit_bytes=_vmem_limit(blocks, temp_bytes=4 * _nbytes((tm, D_MODEL), F32))),
        name="outproj",
    )(o_f, o_b, pm2, o_att, x2, mod, w_out, gdn_norm_w.reshape(1, HEAD_DIM),
      ln_g.reshape(1, D_MODEL), ln_b.reshape(1, D_MODEL))


POOL_HALO = 128


def _pool_kernel(x_ref, xp_ref, xn_ref, mod_ref, pw_ref, ps_ref, lg_ref, lb_ref, o_ref, *, tt, seq):
    t0 = pl.program_id(1) * tt
    m = mod_ref[...]
    sc = 1.0 + m[MOD_SC_M:MOD_SC_M + 1, :]
    sh = m[MOD_SH_M:MOD_SH_M + 1, :]
    x = x_ref[...]
    h_main = x * sc + sh
    h_ext = jnp.concatenate([xp_ref[...] * sc + sh, h_main, xn_ref[...] * sc + sh], axis=0)
    ke = tt + 2 * POOL_HALO
    pr = t0 + lax.broadcasted_iota(jnp.int32, (tt, ke), 0)
    pc = t0 - POOL_HALO + lax.broadcasted_iota(jnp.int32, (tt, ke), 1)
    pos = t0 + lax.broadcasted_iota(jnp.int32, (tt, 1), 0)
    valid = (pc >= 0) & (pc < seq)
    outs = []
    for gi, win in enumerate(POOL_WINDOWS):
        back, fwd = win // 2, win - 1 - win // 2
        band = jnp.where(valid & (pc >= pr - back) & (pc <= pr + fwd), 1.0, 0.0).astype(BF16)
        cnt = (jnp.minimum(pos + fwd, seq - 1) - jnp.maximum(pos - back, 0) + 1).astype(F32)
        lanes = slice(gi * POOL_GROUP, (gi + 1) * POOL_GROUP)
        he = h_ext[:, lanes]
        hi = he.astype(BF16)
        lo = (he - hi.astype(F32)).astype(BF16)
        wsum = _dot(band, hi) + _dot(band, lo)
        diff = wsum / cnt - h_main[:, lanes]
        outs.append(_dot(diff.astype(BF16), pw_ref[gi]))
    mix = jnp.concatenate(outs, axis=1) * ps_ref[...]
    y = DEEPNORM_ALPHA * x + m[MOD_G_M:MOD_G_M + 1, :] * mix
    o_ref[...] = _layer_norm_rows(y, lg_ref[...], lb_ref[...])


def _pool_call(x, mod, pool_w, pool_scale, ln_g, ln_b):
    bsz, seq, _ = x.shape
    tt = 256
    hb = tt // POOL_HALO
    n_halo = seq // POOL_HALO
    vec = pl.BlockSpec((1, D_MODEL), lambda b, i: (0, 0))
    blocks = (2 * _nbytes((tt, D_MODEL), F32) + 2 * _nbytes((POOL_HALO, D_MODEL), F32)
              + _nbytes((MOD_ROWS, D_MODEL), F32) + _nbytes(pool_w.shape, BF16))
    temps = 4 * _nbytes((tt + 2 * POOL_HALO, D_MODEL), F32)
    return pl.pallas_call(
        functools.partial(_pool_kernel, tt=tt, seq=seq),
        grid=(bsz, seq // tt),
        in_specs=[
            pl.BlockSpec((None, tt, D_MODEL), lambda b, i: (b, i, 0)),
            pl.BlockSpec((None, POOL_HALO, D_MODEL), lambda b, i: (b, jnp.maximum(i * hb - 1, 0), 0)),
            pl.BlockSpec((None, POOL_HALO, D_MODEL), lambda b, i: (b, jnp.minimum((i + 1) * hb, n_halo - 1), 0)),
            pl.BlockSpec((None, MOD_ROWS, D_MODEL), lambda b, i: (b, 0, 0)),
            pl.BlockSpec(pool_w.shape, lambda b, i: (0, 0, 0)),
            vec, vec, vec,
        ],
        out_specs=pl.BlockSpec((None, tt, D_MODEL), lambda b, i: (b, i, 0)),
        out_shape=jax.ShapeDtypeStruct((bsz, seq, D_MODEL), F32),
        compiler_params=pltpu.CompilerParams(
            dimension_semantics=("parallel", "parallel"),
            vmem_limit_bytes=_vmem_limit(blocks, temp_bytes=temps)),
        name="pool",
    )(x, x, x, mod, pool_w, pool_scale.reshape(1, D_MODEL), ln_g.reshape(1, D_MODEL), ln_b.reshape(1, D_MODEL))


def _mlp_kernel(x_ref, mod_ref, w1_ref, w2_ref, lg_ref, lb_ref, o_ref, h_ref, acc_ref):
    j = pl.program_id(1)

    @pl.when(j == 0)
    def _():
        m = mod_ref[...]
        h = x_ref[...] * (1.0 + m[MOD_SC_F:MOD_SC_F + 1, :]) + m[MOD_SH_F:MOD_SH_F + 1, :]
        h_ref[...] = h.astype(BF16)
        acc_ref[...] = jnp.zeros_like(acc_ref)

    u = jnp.maximum(_dot(h_ref[...], w1_ref[...]), 0.0)
    acc_ref[...] += _dot((u * u).astype(BF16), w2_ref[...])

    @pl.when(j == pl.num_programs(1) - 1)
    def _():
        m = mod_ref[...]
        y = DEEPNORM_ALPHA * x_ref[...] + m[MOD_G_F:MOD_G_F + 1, :] * acc_ref[...]
        o_ref[...] = _layer_norm_rows(y, lg_ref[...], lb_ref[...])


def _mlp_call(x2, mod, w1, w2, ln_g, ln_b, seq):
    m = x2.shape[0]
    tm, tf = 512, 1024
    vec = pl.BlockSpec((1, D_MODEL), lambda i, j: (0, 0))
    blocks = (2 * _nbytes((tm, D_MODEL), F32) + _nbytes((MOD_ROWS, D_MODEL), F32)
              + 2 * _nbytes((D_MODEL, tf), BF16))
    scratch = _nbytes((tm, D_MODEL), BF16) + _nbytes((tm, D_MODEL), F32)
    temps = 2 * _nbytes((tm, tf), F32) + _nbytes((tm, D_MODEL), F32)
    return pl.pallas_call(
        _mlp_kernel,
        grid=(m // tm, D_FF // tf),
        in_specs=[
            pl.BlockSpec((tm, D_MODEL), lambda i, j: (i, 0)),
            pl.BlockSpec((None, MOD_ROWS, D_MODEL), lambda i, j: ((i * tm) // seq, 0, 0)),
            pl.BlockSpec((D_MODEL, tf), lambda i, j: (0, j)),
            pl.BlockSpec((tf, D_MODEL), lambda i, j: (j, 0)),
            vec, vec,
        ],
        out_specs=pl.BlockSpec((tm, D_MODEL), lambda i, j: (i, 0)),
        out_shape=jax.ShapeDtypeStruct((m, D_MODEL), F32),
        scratch_shapes=[pltpu.VMEM((tm, D_MODEL), BF16), pltpu.VMEM((tm, D_MODEL), F32)],
        compiler_params=pltpu.CompilerParams(
            dimension_semantics=("parallel", "arbitrary"),
            vmem_limit_bytes=_vmem_limit(blocks, scratch, temps)),
        name="mlp",
    )(x2, mod, w1, w2, ln_g.reshape(1, D_MODEL), ln_b.reshape(1, D_MODEL))


def _mod_table(cond_rows):
    bsz = cond_rows.shape[0]
    t = cond_rows.reshape(bsz, 6, D_MODEL)
    return jnp.pad(t, ((0, 0), (0, MOD_ROWS - 6), (0, 0)))


def _pack_weights(w_in, conv_w, a_log, dt_bias, w_out, pool_w, mlp_w1, mlp_w2):
    wi = w_in[0]
    w_main = jnp.concatenate(
        [wi[:, :OFF_BETA], wi[:, OFF_AQ:]], axis=1).astype(BF16)
    w_gate = jnp.pad(wi[:, OFF_BETA:OFF_AQ], ((0, 0), (0, LANES - 2 * N_GATE_COLS))).astype(BF16)
    conv_w8 = jnp.pad(conv_w[0], ((0, SUBLANES - CONV_W), (0, 0)))
    gate_params = jnp.zeros((SUBLANES, LANES), F32)
    gate_params = gate_params.at[0, N_GATE_COLS:2 * N_GATE_COLS].set(dt_bias[0].reshape(-1))
    gate_params = gate_params.at[1, N_GATE_COLS:2 * N_GATE_COLS].set(a_log[0].reshape(-1))
    return dict(w_main=w_main, w_gate=w_gate, conv_w8=conv_w8, gate_params=gate_params,
                w_out=w_out[0].astype(BF16), pool_w=pool_w[0].astype(BF16),
                w1=mlp_w1.astype(BF16), w2=mlp_w2.astype(BF16))


def _trunk(x, cond, pk, gdn_norm_w, q_norm_w, k_norm_w, pool_scale, ln_g, ln_b):
    bsz, seq, _ = x.shape
    m = bsz * seq
    mod0 = _mod_table(cond[0])
    mod1 = _mod_table(cond[1])
    x2 = x.reshape(m, D_MODEL)

    pm2, gates2 = _inproj_call(x2, mod0, pk["w_main"], pk["w_gate"], seq)
    pm = pm2.reshape(bsz, seq, PM_WIDTH)
    gq, gk, gv, gcol, grow = _gdnprep_call(pm, gates2.reshape(bsz, seq, LANES), pk["conv_w8"], pk["gate_params"])
    o_f, o_b = _gdn_call(gq, gk, gv, gcol, grow)
    cos, sin_up, sin_dn = _rope_tables(seq)
    qr, kr = _attprep_call(pm, cos, sin_up, sin_dn, q_norm_w[0], k_norm_w[0])
    o_att = _flash_call(qr, kr, pm)
    x2 = _outproj_call(o_f.reshape(m, GDN_WIDTH), o_b.reshape(m, GDN_WIDTH), pm2, o_att.reshape(m, ATT_WIDTH),
                       x2, mod0, pk["w_out"], gdn_norm_w[0], ln_g[0, 0], ln_b[0, 0], seq)
    x2 = _mlp_call(x2, mod0, pk["w1"][0], pk["w2"][0], ln_g[0, 1], ln_b[0, 1], seq)

    x3 = _pool_call(x2.reshape(bsz, seq, D_MODEL), mod1, pk["pool_w"], pool_scale[0], ln_g[1, 0], ln_b[1, 0])
    x2 = _mlp_call(x3.reshape(m, D_MODEL), mod1, pk["w1"][1], pk["w2"][1], ln_g[1, 1], ln_b[1, 1], seq)
    return x2.reshape(bsz, seq, D_MODEL)


def kernel(x_prompt, x_sample, c_prompt, c_sample, w_in, conv_w, a_log, dt_bias, gdn_norm_w, q_norm_w,
           k_norm_w, w_out, pool_w, pool_scale, mlp_w1, mlp_w2, ada_w, ada_b, ln_g, ln_b):
    bp, bs = c_prompt.shape[0], c_sample.shape[0]
    c_all = jnp.concatenate([c_prompt, c_sample], axis=0)
    c_all = jnp.pad(c_all, ((0, (-c_all.shape[0]) % SUBLANES), (0, 0)))
    cond = _ada_call(c_all, ada_w, ada_b)
    pk = _pack_weights(w_in, conv_w, a_log, dt_bias, w_out, pool_w, mlp_w1, mlp_w2)
    y_prompt = _trunk(x_prompt, cond[:, :bp], pk, gdn_norm_w, q_norm_w, k_norm_w, pool_scale, ln_g, ln_b)
    y_sample = _trunk(x_sample, cond[:, bp:bp + bs], pk, gdn_norm_w, q_norm_w, k_norm_w, pool_scale, ln_g, ln_b)
    return (y_prompt, y_sample)
```

```python
import functools

import jax
import jax.numpy as jnp
from jax import lax
from jax.experimental import pallas as pl
from jax.experimental.pallas import tpu as pltpu

F32 = jnp.float32
BF16 = jnp.bfloat16

D_MODEL = 2048
DEPTH = 2
GRID_W = 64
HEAD_DIM = 128
GDN_HEADS = 8
ATT_HEADS = 8
ATT_KV_HEADS = 2
ATT_GROUP = ATT_HEADS // ATT_KV_HEADS
GDN_WIDTH = GDN_HEADS * HEAD_DIM
ATT_WIDTH = ATT_HEADS * HEAD_DIM
ATT_KV_WIDTH = ATT_KV_HEADS * HEAD_DIM
CONV_W = 5
ROPE_THETA = 10000.0
POOL_WINDOWS = (2, 4, 8, 16)
N_POOL_GROUPS = 4
POOL_GROUP = D_MODEL // N_POOL_GROUPS
D_FF = 4 * D_MODEL
DEEPNORM_ALPHA = (2 * DEPTH) ** 0.25
NORM_EPS = 1e-6
LN_EPS = 1e-5

GDN_QKV = 3 * GDN_WIDTH
OFF_Z = GDN_QKV
OFF_BETA = OFF_Z + GDN_WIDTH
OFF_A = OFF_BETA + 2 * GDN_HEADS
OFF_AQ = OFF_A + 2 * GDN_HEADS
OFF_AK = OFF_AQ + ATT_WIDTH
OFF_AV = OFF_AK + ATT_KV_WIDTH
D_IN = OFF_AV + ATT_KV_WIDTH

PM_QKV = 0
PM_Z = GDN_QKV
PM_AQ = PM_Z + GDN_WIDTH
PM_AK = PM_AQ + ATT_WIDTH
PM_AV = PM_AK + ATT_KV_WIDTH
PM_WIDTH = PM_AV + ATT_KV_WIDTH

LANES = 128
SUBLANES = 8
BF16_ROWS = 16
V7X_VMEM_BYTES = 64 * 1024 * 1024

GDN_CHUNK = 128
N_GATE_COLS = 2 * GDN_HEADS

MOD_SH_M, MOD_SC_M, MOD_G_M, MOD_SH_F, MOD_SC_F, MOD_G_F = range(6)
MOD_ROWS = 8


def _vmem_limit(block_bytes, scratch_bytes=0, temp_bytes=0):
    need = 2 * block_bytes + scratch_bytes + temp_bytes
    return int(min(need + need // 4, V7X_VMEM_BYTES - 8 * 1024 * 1024))


def _nbytes(shape, dtype):
    n = 1
    for s in shape:
        n *= s
    return n * jnp.dtype(dtype).itemsize


def _sigmoid(x):
    return 1.0 / (1.0 + jnp.exp(-x))


def _silu(x):
    return x * _sigmoid(x)


def _layer_norm_rows(y, g, b):
    mu = jnp.mean(y, axis=-1, keepdims=True)
    yc = y - mu
    var = jnp.mean(yc * yc, axis=-1, keepdims=True)
    return yc * lax.rsqrt(var + LN_EPS) * g + b


def _dot(a, b):
    return jnp.dot(a, b, preferred_element_type=F32)


def _dot_nt(a, b):
    return lax.dot_general(a, b, (((1,), (1,)), ((), ())), preferred_element_type=F32)


def _ada_kernel(c_ref, w_ref, b_ref, o_ref):
    c = c_ref[...]
    s = _silu(c).astype(BF16)
    o_ref[...] = _dot(s, w_ref[...].astype(BF16)) + b_ref[...]


def _ada_call(c_all, ada_w, ada_b):
    rows = c_all.shape[0]
    n = ada_w.shape[-1]
    tn = 1024
    blocks = _nbytes((D_MODEL, tn), F32) + _nbytes((rows, D_MODEL), F32) + 2 * _nbytes((rows, tn), F32)
    return pl.pallas_call(
        _ada_kernel,
        grid=(DEPTH, n // tn),
        in_specs=[
            pl.BlockSpec((rows, D_MODEL), lambda l, j: (0, 0)),
            pl.BlockSpec((None, D_MODEL, tn), lambda l, j: (l, 0, j)),
            pl.BlockSpec((None, 1, tn), lambda l, j: (l, 0, j)),
        ],
        out_specs=pl.BlockSpec((None, rows, tn), lambda l, j: (l, 0, j)),
        out_shape=jax.ShapeDtypeStruct((DEPTH, rows, n), F32),
        compiler_params=pltpu.CompilerParams(
            dimension_semantics=("parallel", "parallel"),
            vmem_limit_bytes=_vmem_limit(blocks, temp_bytes=_nbytes((D_MODEL, tn), BF16))),
        name="ada",
    )(c_all, ada_w, ada_b.reshape(DEPTH, 1, n))


def _inproj_kernel(x_ref, mod_ref, w_ref, wg_ref, o_ref, g_ref, h_ref):
    @pl.when(pl.program_id(1) == 0)
    def _():
        m = mod_ref[...]
        h = x_ref[...] * (1.0 + m[MOD_SC_M:MOD_SC_M + 1, :]) + m[MOD_SH_M:MOD_SH_M + 1, :]
        hb = h.astype(BF16)
        h_ref[...] = hb
        g_ref[...] = _dot(hb, wg_ref[...])

    o_ref[...] = _dot(h_ref[...], w_ref[...]).astype(o_ref.dtype)


def _inproj_call(x2, mod, w_main, w_gate, seq):
    m = x2.shape[0]
    tm, tn = 512, 512
    blocks = (_nbytes((tm, D_MODEL), F32) + _nbytes((MOD_ROWS, D_MODEL), F32) + _nbytes((D_MODEL, tn), BF16)
              + _nbytes((D_MODEL, LANES), BF16) + _nbytes((tm, tn), BF16) + _nbytes((tm, LANES), F32))
    scratch = _nbytes((tm, D_MODEL), BF16)
    return pl.pallas_call(
        _inproj_kernel,
        grid=(m // tm, PM_WIDTH // tn),
        in_specs=[
            pl.BlockSpec((tm, D_MODEL), lambda i, j: (i, 0)),
            pl.BlockSpec((None, MOD_ROWS, D_MODEL), lambda i, j: ((i * tm) // seq, 0, 0)),
            pl.BlockSpec((D_MODEL, tn), lambda i, j: (0, j)),
            pl.BlockSpec((D_MODEL, LANES), lambda i, j: (0, 0)),
        ],
        out_specs=[
            pl.BlockSpec((tm, tn), lambda i, j: (i, j)),
            pl.BlockSpec((tm, LANES), lambda i, j: (i, 0)),
        ],
        out_shape=[
            jax.ShapeDtypeStruct((m, PM_WIDTH), BF16),
            jax.ShapeDtypeStruct((m, LANES), F32),
        ],
        scratch_shapes=[pltpu.VMEM((tm, D_MODEL), BF16)],
        compiler_params=pltpu.CompilerParams(
            dimension_semantics=("parallel", "arbitrary"),
            vmem_limit_bytes=_vmem_limit(blocks, scratch, _nbytes((tm, D_MODEL), F32))),
        name="inproj",
    )(x2, mod, w_main, w_gate)


def _shifted_rows(x, halo8, shift, n_rows):
    k = abs(shift)
    rolled = pltpu.roll(x, (-shift) % n_rows, 0)
    r8 = lax.broadcasted_iota(jnp.int32, (SUBLANES, x.shape[1]), 0)
    if shift < 0:
        fix = pltpu.roll(halo8, k, 0)
        head = jnp.where(r8 < k, fix, rolled[:SUBLANES])
        return jnp.concatenate([head, rolled[SUBLANES:]], axis=0)
    fix = pltpu.roll(halo8, SUBLANES - k, 0)
    tail = jnp.where(r8 >= SUBLANES - k, fix, rolled[n_rows - SUBLANES:])
    return jnp.concatenate([rolled[:n_rows - SUBLANES], tail], axis=0)


def _gdnprep_kernel(x_ref, xp_ref, xn_ref, gate_ref, cw_ref, gp_ref,
                    q_ref, k_ref, v_ref, gcol_ref, grow_ref, *, tt):
    i = pl.program_id(1)
    first = i == 0
    last = i == pl.num_programs(1) - 1
    cw = cw_ref[...]
    for part, out_ref in enumerate((q_ref, k_ref, v_ref)):
        cols = slice(part * GDN_WIDTH, (part + 1) * GDN_WIDTH)
        x = x_ref[:, cols].astype(F32)
        prev8 = xp_ref[:, cols].astype(F32)[BF16_ROWS - SUBLANES:]
        next8 = xn_ref[:, cols].astype(F32)[:SUBLANES]
        prev8 = jnp.where(first, 0.0, prev8)
        next8 = jnp.where(last, 0.0, next8)
        w = cw[:, cols]
        acc = x * w[CONV_W // 2:CONV_W // 2 + 1, :]
        for tap in range(CONV_W):
            shift = tap - CONV_W // 2
            if shift == 0:
                continue
            halo = prev8 if shift < 0 else next8
            acc = acc + _shifted_rows(x, halo, shift, tt) * w[tap:tap + 1, :]
        y = _silu(acc)
        if part < 2:
            heads = []
            for h in range(GDN_HEADS):
                yh = y[:, h * HEAD_DIM:(h + 1) * HEAD_DIM]
                inv = lax.rsqrt(jnp.sum(yh * yh, axis=-1, keepdims=True) + NORM_EPS)
                if part == 0:
                    inv = inv * (HEAD_DIM ** -0.5)
                heads.append(yh * inv)
            y = jnp.concatenate(heads, axis=1)
        out_ref[...] = y.astype(out_ref.dtype)

    raw = gate_ref[...]
    gp = gp_ref[...]
    col = lax.broadcasted_iota(jnp.int32, raw.shape, 1)
    beta = _sigmoid(raw)
    z = raw + gp[0:1, :]
    softplus = jnp.maximum(z, 0.0) + jnp.log(1.0 + jnp.exp(-jnp.abs(z)))
    logdec = -jnp.exp(gp[1:2, :]) * softplus
    is_dec = (col >= N_GATE_COLS) & (col < 2 * N_GATE_COLS)
    gsrc = jnp.where(is_dec, logdec, 0.0)
    r = lax.broadcasted_iota(jnp.int32, (tt, tt), 0)
    c = lax.broadcasted_iota(jnp.int32, (tt, tt), 1)
    same = (r // GDN_CHUNK) == (c // GDN_CHUNK)
    p_lo = jnp.where(same & (c <= r), 1.0, 0.0).astype(F32)
    p_up = jnp.where(same & (c >= r), 1.0, 0.0).astype(F32)
    cum_lo = jnp.dot(p_lo, gsrc, precision=lax.Precision.HIGHEST, preferred_element_type=F32)
    cum_up = jnp.dot(p_up, gsrc, precision=lax.Precision.HIGHEST, preferred_element_type=F32)
    total = cum_lo + cum_up - gsrc
    fwd_col = col < N_GATE_COLS + GDN_HEADS
    gc = jnp.where(fwd_col, cum_lo, cum_up)
    tot_shift = pltpu.roll(total, N_GATE_COLS, 1)
    gcol = jnp.where(col < N_GATE_COLS, beta,
                     jnp.where(col < 2 * N_GATE_COLS, gc,
                               jnp.where(col < 3 * N_GATE_COLS, tot_shift, 0.0)))
    gcol_ref[...] = gcol
    grow_ref[...] = gcol.T


def _gdnprep_call(pm, gates, conv_w8, gate_params):
    bsz, seq, _ = pm.shape
    tt = 256
    hb = tt // BF16_ROWS
    n_halo = seq // BF16_ROWS
    blocks = (_nbytes((tt, GDN_QKV), BF16) + 2 * _nbytes((BF16_ROWS, GDN_QKV), BF16) + _nbytes((tt, LANES), F32)
              + _nbytes((SUBLANES, GDN_QKV), F32) + 3 * _nbytes((tt, GDN_WIDTH), BF16) + 2 * _nbytes((tt, LANES), F32))
    temps = 8 * _nbytes((tt, GDN_WIDTH), F32) + 4 * _nbytes((tt, tt), F32)
    return pl.pallas_call(
        functools.partial(_gdnprep_kernel, tt=tt),
        grid=(bsz, seq // tt),
        in_specs=[
            pl.BlockSpec((None, tt, GDN_QKV), lambda b, i: (b, i, 0)),
            pl.BlockSpec((None, BF16_ROWS, GDN_QKV), lambda b, i: (b, jnp.maximum(i * hb - 1, 0), 0)),
            pl.BlockSpec((None, BF16_ROWS, GDN_QKV), lambda b, i: (b, jnp.minimum((i + 1) * hb, n_halo - 1), 0)),
            pl.BlockSpec((None, tt, LANES), lambda b, i: (b, i, 0)),
            pl.BlockSpec((SUBLANES, GDN_QKV), lambda b, i: (0, 0)),
            pl.BlockSpec((SUBLANES, LANES), lambda b, i: (0, 0)),
        ],
        out_specs=[
            pl.BlockSpec((None, tt, GDN_WIDTH), lambda b, i: (b, i, 0)),
            pl.BlockSpec((None, tt, GDN_WIDTH), lambda b, i: (b, i, 0)),
            pl.BlockSpec((None, tt, GDN_WIDTH), lambda b, i: (b, i, 0)),
            pl.BlockSpec((None, tt, LANES), lambda b, i: (b, i, 0)),
            pl.BlockSpec((None, LANES, tt), lambda b, i: (b, 0, i)),
        ],
        out_shape=[
            jax.ShapeDtypeStruct((bsz, seq, GDN_WIDTH), BF16),
            jax.ShapeDtypeStruct((bsz, seq, GDN_WIDTH), BF16),
            jax.ShapeDtypeStruct((bsz, seq, GDN_WIDTH), BF16),
            jax.ShapeDtypeStruct((bsz, seq, LANES), F32),
            jax.ShapeDtypeStruct((bsz, LANES, seq), F32),
        ],
        compiler_params=pltpu.CompilerParams(
            dimension_semantics=("parallel", "parallel"),
            vmem_limit_bytes=_vmem_limit(blocks, temp_bytes=temps)),
        name="gdnprep",
    )(pm, pm, pm, gates, conv_w8, gate_params)


def _gdn_unit(q, k, v, beta, gcc, gtc, gcr, gtr, state, reverse):
    n = GDN_CHUNK
    r = lax.broadcasted_iota(jnp.int32, (n, n), 0)
    c = lax.broadcasted_iota(jnp.int32, (n, n), 1)
    incl = (r <= c) if reverse else (r >= c)
    strict = (r < c) if reverse else (r > c)
    decay = jnp.where(incl, jnp.exp(jnp.where(incl, gcc - gcr, 0.0)), 0.0)

    kf = k.astype(F32)
    kb = kf * beta
    vb = (v.astype(F32) * beta).astype(BF16)
    aq = _dot_nt(jnp.concatenate([kb.astype(BF16), q], axis=0), k)
    low = jnp.where(strict, aq[:n] * decay, 0.0)
    qk = aq[n:] * decay

    eye = jnp.where(r == c, 1.0, 0.0).astype(F32)
    tinv = eye - low
    lb = low.astype(BF16)
    power = _dot(lb, lb)
    n_factors = (n - 1).bit_length() - 1
    for it in range(n_factors):
        pb = power.astype(BF16)
        if it + 1 < n_factors:
            both = _dot(jnp.concatenate([tinv.astype(BF16), pb], axis=0), pb)
            tinv = tinv + both[:n]
            power = both[n:]
        else:
            tinv = tinv + _dot(tinv.astype(BF16), pb)

    egc = jnp.exp(gcc)
    kbg = (kb * egc).astype(BF16)
    uw = _dot(tinv.astype(BF16), jnp.concatenate([vb, kbg], axis=1))
    u = uw[:, :HEAD_DIM]
    w = uw[:, HEAD_DIM:]
    qg = (q.astype(F32) * egc).astype(BF16)
    ws = _dot(jnp.concatenate([w.astype(BF16), qg], axis=0), state.astype(BF16))
    v_new = (u - ws[:n]).astype(BF16)
    kdec_t = (kf * jnp.exp(gtc - gcc)).T.astype(BF16)
    os_ = _dot(jnp.concatenate([qk.astype(BF16), kdec_t], axis=0), v_new)
    out = ws[n:] + os_[:n]
    new_state = state * jnp.exp(gtr) + os_[n:]
    return out, new_state


def _gdn_kernel(qf_ref, kf_ref, vf_ref, gcf_ref, grf_ref,
                qb_ref, kb_ref, vb_ref, gcb_ref, grb_ref,
                of_ref, ob_ref, state_ref):
    @pl.when(pl.program_id(1) == 0)
    def _():
        state_ref[...] = jnp.zeros_like(state_ref)

    for d, (q_ref, k_ref, v_ref, gc_ref, gr_ref, o_ref) in enumerate((
            (qf_ref, kf_ref, vf_ref, gcf_ref, grf_ref, of_ref),
            (qb_ref, kb_ref, vb_ref, gcb_ref, grb_ref, ob_ref))):
        gcol = gc_ref[...]
        grow = gr_ref[...]
        for h in range(GDN_HEADS):
            lanes = slice(h * HEAD_DIM, (h + 1) * HEAD_DIM)
            j = d * GDN_HEADS + h
            out, new_state = _gdn_unit(
                q_ref[:, lanes], k_ref[:, lanes], v_ref[:, lanes],
                gcol[:, j:j + 1],
                gcol[:, N_GATE_COLS + j:N_GATE_COLS + j + 1],
                gcol[:, 2 * N_GATE_COLS + j:2 * N_GATE_COLS + j + 1],
                grow[N_GATE_COLS + j:N_GATE_COLS + j + 1, :],
                grow[2 * N_GATE_COLS + j:2 * N_GATE_COLS + j + 1, :],
                state_ref[d, h], reverse=(d == 1))
            o_ref[:, lanes] = out
            state_ref[d, h] = new_state


def _gdn_call(q, k, v, gcol, grow):
    bsz, seq, _ = q.shape
    tb = GDN_CHUNK
    nb = seq // tb
    qkv_spec_f = pl.BlockSpec((None, tb, GDN_WIDTH), lambda b, i: (b, i, 0))
    qkv_spec_b = pl.BlockSpec((None, tb, GDN_WIDTH), lambda b, i: (b, nb - 1 - i, 0))
    gc_spec_f = pl.BlockSpec((None, tb, LANES), lambda b, i: (b, i, 0))
    gc_spec_b = pl.BlockSpec((None, tb, LANES), lambda b, i: (b, nb - 1 - i, 0))
    gr_spec_f = pl.BlockSpec((None, LANES, tb), lambda b, i: (b, 0, i))
    gr_spec_b = pl.BlockSpec((None, LANES, tb), lambda b, i: (b, 0, nb - 1 - i))
    blocks = 2 * (3 * _nbytes((tb, GDN_WIDTH), BF16) + 2 * _nbytes((tb, LANES), F32) + _nbytes((tb, GDN_WIDTH), F32))
    scratch = _nbytes((2, GDN_HEADS, HEAD_DIM, HEAD_DIM), F32)
    temps = 2 * GDN_HEADS * 24 * _nbytes((GDN_CHUNK, LANES), F32)
    return pl.pallas_call(
        _gdn_kernel,
        grid=(bsz, nb),
        in_specs=[qkv_spec_f, qkv_spec_f, qkv_spec_f, gc_spec_f, gr_spec_f,
                  qkv_spec_b, qkv_spec_b, qkv_spec_b, gc_spec_b, gr_spec_b],
        out_specs=[qkv_spec_f, qkv_spec_b],
        out_shape=[jax.ShapeDtypeStruct((bsz, seq, GDN_WIDTH), F32),
                   jax.ShapeDtypeStruct((bsz, seq, GDN_WIDTH), F32)],
        scratch_shapes=[pltpu.VMEM((2, GDN_HEADS, HEAD_DIM, HEAD_DIM), F32)],
        compiler_params=pltpu.CompilerParams(
            dimension_semantics=("parallel", "arbitrary"),
            vmem_limit_bytes=_vmem_limit(blocks, scratch, temps)),
        name="gdn",
    )(q, k, v, gcol, grow, q, k, v, gcol, grow)


def _rope_tables(seq):
    t = jnp.arange(seq)
    row = (t // GRID_W).astype(F32)
    col = (t % GRID_W).astype(F32)
    half = HEAD_DIM // 2
    inv_freq = ROPE_THETA ** (-jnp.arange(0, half, 2, dtype=F32) / half)
    ang_r = row[:, None] * inv_freq
    ang_c = col[:, None] * inv_freq
    ang = jnp.concatenate([ang_r, ang_r, ang_c, ang_c], -1)
    cos, sin = jnp.cos(ang), jnp.sin(ang)
    lane = jnp.arange(HEAD_DIM)
    first = (lane // (HEAD_DIM // 4)) % 2 == 0
    sin_up = jnp.where(first, -sin, 0.0)
    sin_dn = jnp.where(first, 0.0, sin)
    return cos, sin_up, sin_dn


def _norm_rope(x, w, cos, sin_up, sin_dn, scale):
    xn = x * lax.rsqrt(jnp.mean(x * x, axis=-1, keepdims=True) + NORM_EPS) * w
    quarter = HEAD_DIM // 4
    up = pltpu.roll(xn, HEAD_DIM - quarter, 1)
    dn = pltpu.roll(xn, quarter, 1)
    y = xn * cos + up * sin_up + dn * sin_dn
    return y * scale if scale != 1.0 else y


ATT_TK = 512
LOG2_E = 1.4426950408889634


def _attprep_kernel(aq_ref, ak_ref, av_ref, cos_ref, su_ref, sd_ref, qw_ref, kw_ref, qt_ref, k_ref, vt_ref):
    cos, su, sd = cos_ref[...], su_ref[...], sd_ref[...]
    qw, kw = qw_ref[...], kw_ref[...]
    for h in range(ATT_HEADS):
        lanes = slice(h * HEAD_DIM, (h + 1) * HEAD_DIM)
        q = _norm_rope(aq_ref[:, lanes].astype(F32), qw, cos, su, sd, HEAD_DIM ** -0.5 * LOG2_E)
        qt_ref[lanes, :] = q.T.astype(qt_ref.dtype)
    for h in range(ATT_KV_HEADS):
        lanes = slice(h * HEAD_DIM, (h + 1) * HEAD_DIM)
        k_ref[:, lanes] = _norm_rope(ak_ref[:, lanes].astype(F32), kw, cos, su, sd, 1.0).astype(k_ref.dtype)
        vt_ref[h] = av_ref[:, lanes].astype(F32).T.astype(vt_ref.dtype)


def _attprep_call(pm, cos, sin_up, sin_dn, q_norm_w, k_norm_w):
    bsz, seq, _ = pm.shape
    tt = ATT_TK
    tab = pl.BlockSpec((tt, HEAD_DIM), lambda b, i: (i, 0))
    vec = pl.BlockSpec((1, HEAD_DIM), lambda b, i: (0, 0))
    blocks = (2 * _nbytes((tt, ATT_WIDTH), BF16) + 4 * _nbytes((tt, ATT_KV_WIDTH), BF16)
              + 3 * _nbytes((tt, HEAD_DIM), F32))
    return pl.pallas_call(
        _attprep_kernel,
        grid=(bsz, seq // tt),
        in_specs=[
            pl.BlockSpec((None, tt, ATT_WIDTH), lambda b, i: (b, i, PM_AQ // ATT_WIDTH)),
            pl.BlockSpec((None, tt, ATT_KV_WIDTH), lambda b, i: (b, i, PM_AK // ATT_KV_WIDTH)),
            pl.BlockSpec((None, tt, ATT_KV_WIDTH), lambda b, i: (b, i, PM_AV // ATT_KV_WIDTH)),
            tab, tab, tab, vec, vec,
        ],
        out_specs=[
            pl.BlockSpec((None, ATT_WIDTH, tt), lambda b, i: (b, 0, i)),
            pl.BlockSpec((None, tt, ATT_KV_WIDTH), lambda b, i: (b, i, 0)),
            pl.BlockSpec((None, ATT_KV_HEADS, None, HEAD_DIM, tt), lambda b, i: (b, 0, i, 0, 0)),
        ],
        out_shape=[jax.ShapeDtypeStruct((bsz, ATT_WIDTH, seq), BF16),
                   jax.ShapeDtypeStruct((bsz, seq, ATT_KV_WIDTH), BF16),
                   jax.ShapeDtypeStruct((bsz, ATT_KV_HEADS, seq // tt, HEAD_DIM, tt), BF16)],
        compiler_params=pltpu.CompilerParams(
            dimension_semantics=("parallel", "parallel"),
            vmem_limit_bytes=_vmem_limit(blocks, temp_bytes=8 * _nbytes((tt, HEAD_DIM), F32))),
        name="attprep",
    )(pm, pm, pm, cos, sin_up, sin_dn, q_norm_w.reshape(1, HEAD_DIM), k_norm_w.reshape(1, HEAD_DIM))


def _flash_kernel(qt_ref, k_ref, vt_ref, o_ref, m_ref, l_ref, acc_ref, qc_ref, s_ref, mc_ref, *, tq, tk, n_k):
    m_ref[...] = jnp.full_like(m_ref, -jnp.inf)
    l_ref[...] = jnp.zeros_like(l_ref)
    acc_ref[...] = jnp.zeros_like(acc_ref)
    for g in range(ATT_GROUP):
        qc_ref[:, g * tq:(g + 1) * tq] = qt_ref[g * HEAD_DIM:(g + 1) * HEAD_DIM, :]

    def scores(j, slot):
        start = pl.multiple_of(j * tk, tk)
        st = _dot(k_ref[pl.ds(start, tk), :], qc_ref[...])
        s_ref[slot] = st
        mc_ref[slot] = jnp.max(st, axis=0, keepdims=True)

    def accumulate(j, slot):
        m_prev = m_ref[...]
        m_new = jnp.maximum(m_prev, mc_ref[slot])
        alpha = jnp.exp2(m_prev - m_new)
        p = jnp.exp2(s_ref[slot] - m_new)
        l_ref[...] = alpha * l_ref[...] + jnp.sum(p, axis=0, keepdims=True)
        acc_ref[...] = alpha * acc_ref[...] + _dot(vt_ref[j], p.astype(BF16))
        m_ref[...] = m_new

    scores(0, 0)

    def body(j, carry):
        slot = j % 2
        scores(j + 1, 1 - slot)
        accumulate(j, slot)
        return carry

    lax.fori_loop(0, n_k - 1, body, 0)
    accumulate(n_k - 1, (n_k - 1) % 2)
    for g in range(ATT_GROUP):
        lanes = slice(g * tq, (g + 1) * tq)
        out_t = acc_ref[:, lanes] / l_ref[:, lanes]
        o_ref[:, g * HEAD_DIM:(g + 1) * HEAD_DIM] = out_t.T.astype(o_ref.dtype)


def _flash_call(qt, kr, vt):
    bsz, _, seq = qt.shape
    tq, tk = 256, ATT_TK
    gw = ATT_GROUP * HEAD_DIM
    nq = ATT_GROUP * tq
    blocks = 2 * _nbytes((gw, tq), BF16) + 2 * _nbytes((seq, HEAD_DIM), BF16)
    scratch = (4 * _nbytes((SUBLANES, nq), F32) + _nbytes((HEAD_DIM, nq), F32) + _nbytes((HEAD_DIM, nq), BF16)
               + 2 * _nbytes((tk, nq), F32))
    temps = 2 * _nbytes((tk, nq), F32)
    return pl.pallas_call(
        functools.partial(_flash_kernel, tq=tq, tk=tk, n_k=seq // tk),
        grid=(bsz, ATT_KV_HEADS, seq // tq),
        in_specs=[
            pl.BlockSpec((None, gw, tq), lambda b, h, i: (b, h, i)),
            pl.BlockSpec((None, seq, HEAD_DIM), lambda b, h, i: (b, 0, h)),
            pl.BlockSpec((None, None, seq // tk, HEAD_DIM, tk), lambda b, h, i: (b, h, 0, 0, 0)),
        ],
        out_specs=pl.BlockSpec((None, tq, gw), lambda b, h, i: (b, i, h)),
        out_shape=jax.ShapeDtypeStruct((bsz, seq, ATT_WIDTH), BF16),
        scratch_shapes=[pltpu.VMEM((1, nq), F32), pltpu.VMEM((1, nq), F32),
                        pltpu.VMEM((HEAD_DIM, nq), F32), pltpu.VMEM((HEAD_DIM, nq), BF16),
                        pltpu.VMEM((2, tk, nq), F32), pltpu.VMEM((2, 1, nq), F32)],
        compiler_params=pltpu.CompilerParams(
            dimension_semantics=("parallel", "parallel", "arbitrary"),
            vmem_limit_bytes=_vmem_limit(blocks, scratch, temps)),
        name="flash",
    )(qt, kr, vt)


def _outproj_kernel(of_ref, ob_ref, z_ref, oa_ref, x_ref, mod_ref, w_ref, nw_ref, lg_ref, lb_ref, o_ref):
    nw = nw_ref[...]
    heads = []
    for h in range(GDN_HEADS):
        lanes = slice(h * HEAD_DIM, (h + 1) * HEAD_DIM)
        o = of_ref[:, lanes] + ob_ref[:, lanes]
        on = o * lax.rsqrt(jnp.mean(o * o, axis=-1, keepdims=True) + NORM_EPS) * nw
        heads.append((on * _silu(z_ref[:, lanes].astype(F32))).astype(BF16))
    og = jnp.concatenate(heads, axis=1)
    mix = _dot(og, w_ref[:GDN_WIDTH, :]) + _dot(oa_ref[...], w_ref[GDN_WIDTH:, :])
    m = mod_ref[...]
    y = DEEPNORM_ALPHA * x_ref[...] + m[MOD_G_M:MOD_G_M + 1, :] * mix
    o_ref[...] = _layer_norm_rows(y, lg_ref[...], lb_ref[...])


def _outproj_call(o_f, o_b, pm2, o_att, x2, mod, w_out, gdn_norm_w, ln_g, ln_b, seq):
    m = x2.shape[0]
    tm = 256
    blocks = (2 * _nbytes((tm, GDN_WIDTH), F32) + 2 * _nbytes((tm, GDN_WIDTH), BF16)
              + 2 * _nbytes((tm, D_MODEL), F32) + _nbytes((MOD_ROWS, D_MODEL), F32)
              + _nbytes((D_MODEL, D_MODEL), BF16))
    vec = pl.BlockSpec((1, D_MODEL), lambda i: (0, 0))
    return pl.pallas_call(
        _outproj_kernel,
        grid=(m // tm,),
        in_specs=[
            pl.BlockSpec((tm, GDN_WIDTH), lambda i: (i, 0)),
            pl.BlockSpec((tm, GDN_WIDTH), lambda i: (i, 0)),
            pl.BlockSpec((tm, GDN_WIDTH), lambda i: (i, PM_Z // GDN_WIDTH)),
            pl.BlockSpec((tm, ATT_WIDTH), lambda i: (i, 0)),
            pl.BlockSpec((tm, D_MODEL), lambda i: (i, 0)),
            pl.BlockSpec((None, MOD_ROWS, D_MODEL), lambda i: ((i * tm) // seq, 0, 0)),
            pl.BlockSpec((D_MODEL, D_MODEL), lambda i: (0, 0)),
            pl.BlockSpec((1, HEAD_DIM), lambda i: (0, 0)),
            vec, vec,
        ],
        out_specs=pl.BlockSpec((tm, D_MODEL), lambda i: (i, 0)),
        out_shape=jax.ShapeDtypeStruct((m, D_MODEL), F32),
        compiler_params=pltpu.CompilerParams(
            dimension_semantics=("parallel",),
            vmem_limit_bytes=_vmem_limit(blocks, temp_bytes=4 * _nbytes((tm, D_MODEL), F32))),
        name="outproj",
    )(o_f, o_b, pm2, o_att, x2, mod, w_out, gdn_norm_w.reshape(1, HEAD_DIM),
      ln_g.reshape(1, D_MODEL), ln_b.reshape(1, D_MODEL))


POOL_HALO = 128


def _pool_kernel(x_ref, xp_ref, xn_ref, mod_ref, pw_ref, ps_ref, lg_ref, lb_ref, o_ref, *, tt, seq):
    t0 = pl.program_id(1) * tt
    m = mod_ref[...]
    sc = 1.0 + m[MOD_SC_M:MOD_SC_M + 1, :]
    sh = m[MOD_SH_M:MOD_SH_M + 1, :]
    x = x_ref[...]
    h_main = x * sc + sh
    h_ext = jnp.concatenate([xp_ref[...] * sc + sh, h_main, xn_ref[...] * sc + sh], axis=0)
    ke = tt + 2 * POOL_HALO
    pr = t0 + lax.broadcasted_iota(jnp.int32, (tt, ke), 0)
    pc = t0 - POOL_HALO + lax.broadcasted_iota(jnp.int32, (tt, ke), 1)
    pos = t0 + lax.broadcasted_iota(jnp.int32, (tt, 1), 0)
    valid = (pc >= 0) & (pc < seq)
    outs = []
    for gi, win in enumerate(POOL_WINDOWS):
        back, fwd = win // 2, win - 1 - win // 2
        band = jnp.where(valid & (pc >= pr - back) & (pc <= pr + fwd), 1.0, 0.0).astype(BF16)
        cnt = (jnp.minimum(pos + fwd, seq - 1) - jnp.maximum(pos - back, 0) + 1).astype(F32)
        lanes = slice(gi * POOL_GROUP, (gi + 1) * POOL_GROUP)
        he = h_ext[:, lanes]
        hi = he.astype(BF16)
        lo = (he - hi.astype(F32)).astype(BF16)
        wsum = _dot(band, hi) + _dot(band, lo)
        diff = wsum / cnt - h_main[:, lanes]
        outs.append(_dot(diff.astype(BF16), pw_ref[gi]))
    mix = jnp.concatenate(outs, axis=1) * ps_ref[...]
    y = DEEPNORM_ALPHA * x + m[MOD_G_M:MOD_G_M + 1, :] * mix
    o_ref[...] = _layer_norm_rows(y, lg_ref[...], lb_ref[...])


def _pool_call(x, mod, pool_w, pool_scale, ln_g, ln_b):
    bsz, seq, _ = x.shape
    tt = 256
    hb = tt // POOL_HALO
    n_halo = seq // POOL_HALO
    vec = pl.BlockSpec((1, D_MODEL), lambda b, i: (0, 0))
    blocks = (2 * _nbytes((tt, D_MODEL), F32) + 2 * _nbytes((POOL_HALO, D_MODEL), F32)
              + _nbytes((MOD_ROWS, D_MODEL), F32) + _nbytes(pool_w.shape, BF16))
    temps = 4 * _nbytes((tt + 2 * POOL_HALO, D_MODEL), F32)
    return pl.pallas_call(
        functools.partial(_pool_kernel, tt=tt, seq=seq),
        grid=(bsz, seq // tt),
        in_specs=[
            pl.BlockSpec((None, tt, D_MODEL), lambda b, i: (b, i, 0)),
            pl.BlockSpec((None, POOL_HALO, D_MODEL), lambda b, i: (b, jnp.maximum(i * hb - 1, 0), 0)),
            pl.BlockSpec((None, POOL_HALO, D_MODEL), lambda b, i: (b, jnp.minimum((i + 1) * hb, n_halo - 1), 0)),
            pl.BlockSpec((None, MOD_ROWS, D_MODEL), lambda b, i: (b, 0, 0)),
            pl.BlockSpec(pool_w.shape, lambda b, i: (0, 0, 0)),
            vec, vec, vec,
        ],
        out_specs=pl.BlockSpec((None, tt, D_MODEL), lambda b, i: (b, i, 0)),
        out_shape=jax.ShapeDtypeStruct((bsz, seq, D_MODEL), F32),
        compiler_params=pltpu.CompilerParams(
            dimension_semantics=("parallel", "parallel"),
            vmem_limit_bytes=_vmem_limit(blocks, temp_bytes=temps)),
        name="pool",
    )(x, x, x, mod, pool_w, pool_scale.reshape(1, D_MODEL), ln_g.reshape(1, D_MODEL), ln_b.reshape(1, D_MODEL))


def _mlp_kernel(x_ref, mod_ref, w1_ref, w2_ref, lg_ref, lb_ref, o_ref, h_ref, acc_ref):
    j = pl.program_id(1)

    @pl.when(j == 0)
    def _():
        m = mod_ref[...]
        h = x_ref[...] * (1.0 + m[MOD_SC_F:MOD_SC_F + 1, :]) + m[MOD_SH_F:MOD_SH_F + 1, :]
        h_ref[...] = h.astype(BF16)
        acc_ref[...] = jnp.zeros_like(acc_ref)

    u = jnp.maximum(_dot(h_ref[...], w1_ref[...]), 0.0)
    acc_ref[...] += _dot((u * u).astype(BF16), w2_ref[...])

    @pl.when(j == pl.num_programs(1) - 1)
    def _():
        m = mod_ref[...]
        y = DEEPNORM_ALPHA * x_ref[...] + m[MOD_G_F:MOD_G_F + 1, :] * acc_ref[...]
        o_ref[...] = _layer_norm_rows(y, lg_ref[...], lb_ref[...])


def _mlp_call(x2, mod, w1, w2, ln_g, ln_b, seq):
    m = x2.shape[0]
    tm, tf = 512, 1024
    vec = pl.BlockSpec((1, D_MODEL), lambda i, j: (0, 0))
    blocks = (2 * _nbytes((tm, D_MODEL), F32) + _nbytes((MOD_ROWS, D_MODEL), F32)
              + 2 * _nbytes((D_MODEL, tf), BF16))
    scratch = _nbytes((tm, D_MODEL), BF16) + _nbytes((tm, D_MODEL), F32)
    temps = 2 * _nbytes((tm, tf), F32) + _nbytes((tm, D_MODEL), F32)
    return pl.pallas_call(
        _mlp_kernel,
        grid=(m // tm, D_FF // tf),
        in_specs=[
            pl.BlockSpec((tm, D_MODEL), lambda i, j: (i, 0)),
            pl.BlockSpec((None, MOD_ROWS, D_MODEL), lambda i, j: ((i * tm) // seq, 0, 0)),
            pl.BlockSpec((D_MODEL, tf), lambda i, j: (0, j)),
            pl.BlockSpec((tf, D_MODEL), lambda i, j: (j, 0)),
            vec, vec,
        ],
        out_specs=pl.BlockSpec((tm, D_MODEL), lambda i, j: (i, 0)),
        out_shape=jax.ShapeDtypeStruct((m, D_MODEL), F32),
        scratch_shapes=[pltpu.VMEM((tm, D_MODEL), BF16), pltpu.VMEM((tm, D_MODEL), F32)],
        compiler_params=pltpu.CompilerParams(
            dimension_semantics=("parallel", "arbitrary"),
            vmem_limit_bytes=_vmem_limit(blocks, scratch, temps)),
        name="mlp",
    )(x2, mod, w1, w2, ln_g.reshape(1, D_MODEL), ln_b.reshape(1, D_MODEL))


def _mod_table(cond_rows):
    bsz = cond_rows.shape[0]
    t = cond_rows.reshape(bsz, 6, D_MODEL)
    return jnp.pad(t, ((0, 0), (0, MOD_ROWS - 6), (0, 0)))


def _pack_weights(w_in, conv_w, a_log, dt_bias, w_out, pool_w, mlp_w1, mlp_w2):
    wi = w_in[0]
    w_main = jnp.concatenate(
        [wi[:, :OFF_BETA], wi[:, OFF_AQ:]], axis=1).astype(BF16)
    w_gate = jnp.pad(wi[:, OFF_BETA:OFF_AQ], ((0, 0), (0, LANES - 2 * N_GATE_COLS))).astype(BF16)
    conv_w8 = jnp.pad(conv_w[0], ((0, SUBLANES - CONV_W), (0, 0)))
    gate_params = jnp.zeros((SUBLANES, LANES), F32)
    gate_params = gate_params.at[0, N_GATE_COLS:2 * N_GATE_COLS].set(dt_bias[0].reshape(-1))
    gate_params = gate_params.at[1, N_GATE_COLS:2 * N_GATE_COLS].set(a_log[0].reshape(-1))
    return dict(w_main=w_main, w_gate=w_gate, conv_w8=conv_w8, gate_params=gate_params,
                w_out=w_out[0].astype(BF16), pool_w=pool_w[0].astype(BF16),
                w1=mlp_w1.astype(BF16), w2=mlp_w2.astype(BF16))


def _trunk(x, cond, pk, gdn_norm_w, q_norm_w, k_norm_w, pool_scale, ln_g, ln_b):
    bsz, seq, _ = x.shape
    m = bsz * seq
    mod0 = _mod_table(cond[0])
    mod1 = _mod_table(cond[1])
    x2 = x.reshape(m, D_MODEL)

    pm2, gates2 = _inproj_call(x2, mod0, pk["w_main"], pk["w_gate"], seq)
    pm = pm2.reshape(bsz, seq, PM_WIDTH)
    gq, gk, gv, gcol, grow = _gdnprep_call(pm, gates2.reshape(bsz, seq, LANES), pk["conv_w8"], pk["gate_params"])
    o_f, o_b = _gdn_call(gq, gk, gv, gcol, grow)
    cos, sin_up, sin_dn = _rope_tables(seq)
    qt, kr, vt = _attprep_call(pm, cos, sin_up, sin_dn, q_norm_w[0], k_norm_w[0])
    o_att = _flash_call(qt, kr, vt)
    x2 = _outproj_call(o_f.reshape(m, GDN_WIDTH), o_b.reshape(m, GDN_WIDTH), pm2, o_att.reshape(m, ATT_WIDTH),
                       x2, mod0, pk["w_out"], gdn_norm_w[0], ln_g[0, 0], ln_b[0, 0], seq)
    x2 = _mlp_call(x2, mod0, pk["w1"][0], pk["w2"][0], ln_g[0, 1], ln_b[0, 1], seq)

    x3 = _pool_call(x2.reshape(bsz, seq, D_MODEL), mod1, pk["pool_w"], pool_scale[0], ln_g[1, 0], ln_b[1, 0])
    x2 = _mlp_call(x3.reshape(m, D_MODEL), mod1, pk["w1"][1], pk["w2"][1], ln_g[1, 1], ln_b[1, 1], seq)
    return x2.reshape(bsz, seq, D_MODEL)


def kernel(x_prompt, x_sample, c_prompt, c_sample, w_in, conv_w, a_log, dt_bias, gdn_norm_w, q_norm_w,
           k_norm_w, w_out, pool_w, pool_scale, mlp_w1, mlp_w2, ada_w, ada_b, ln_g, ln_b):
    bp, bs = c_prompt.shape[0], c_sample.shape[0]
    c_all = jnp.concatenate([c_prompt, c_sample], axis=0)
    c_all = jnp.pad(c_all, ((0, (-c_all.shape[0]) % SUBLANES), (0, 0)))
    cond = _ada_call(c_all, ada_w, ada_b)
    pk = _pack_weights(w_in, conv_w, a_log, dt_bias, w_out, pool_w, mlp_w1, mlp_w2)
    y_prompt = _trunk(x_prompt, cond[:, :bp], pk, gdn_norm_w, q_norm_w, k_norm_w, pool_scale, ln_g, ln_b)
    y_sample = _trunk(x_sample, cond[:, bp:bp + bs], pk, gdn_norm_w, q_norm_w, k_norm_w, pool_scale, ln_g, ln_b)
    return (y_prompt, y_sample)
```

```python
import functools

import jax
import jax.numpy as jnp
from jax import lax
from jax.experimental import pallas as pl
from jax.experimental.pallas import tpu as pltpu

F32 = jnp.float32
BF16 = jnp.bfloat16

D_MODEL = 2048
DEPTH = 2
GRID_W = 64
HEAD_DIM = 128
GDN_HEADS = 8
ATT_HEADS = 8
ATT_KV_HEADS = 2
ATT_GROUP = ATT_HEADS // ATT_KV_HEADS
GDN_WIDTH = GDN_HEADS * HEAD_DIM
ATT_WIDTH = ATT_HEADS * HEAD_DIM
ATT_KV_WIDTH = ATT_KV_HEADS * HEAD_DIM
CONV_W = 5
ROPE_THETA = 10000.0
POOL_WINDOWS = (2, 4, 8, 16)
N_POOL_GROUPS = 4
POOL_GROUP = D_MODEL // N_POOL_GROUPS
D_FF = 4 * D_MODEL
DEEPNORM_ALPHA = (2 * DEPTH) ** 0.25
NORM_EPS = 1e-6
LN_EPS = 1e-5

GDN_QKV = 3 * GDN_WIDTH
OFF_Z = GDN_QKV
OFF_BETA = OFF_Z + GDN_WIDTH
OFF_A = OFF_BETA + 2 * GDN_HEADS
OFF_AQ = OFF_A + 2 * GDN_HEADS
OFF_AK = OFF_AQ + ATT_WIDTH
OFF_AV = OFF_AK + ATT_KV_WIDTH
D_IN = OFF_AV + ATT_KV_WIDTH

PM_QKV = 0
PM_Z = GDN_QKV
PM_AQ = PM_Z + GDN_WIDTH
PM_AK = PM_AQ + ATT_WIDTH
PM_AV = PM_AK + ATT_KV_WIDTH
PM_WIDTH = PM_AV + ATT_KV_WIDTH

LANES = 128
SUBLANES = 8
BF16_ROWS = 16
V7X_VMEM_BYTES = 64 * 1024 * 1024

GDN_CHUNK = 128
N_GATE_COLS = 2 * GDN_HEADS

MOD_SH_M, MOD_SC_M, MOD_G_M, MOD_SH_F, MOD_SC_F, MOD_G_F = range(6)
MOD_ROWS = 8


def _vmem_limit(block_bytes, scratch_bytes=0, temp_bytes=0):
    need = 2 * block_bytes + scratch_bytes + temp_bytes
    return int(min(need + need // 4, V7X_VMEM_BYTES - 8 * 1024 * 1024))


def _nbytes(shape, dtype):
    n = 1
    for s in shape:
        n *= s
    return n * jnp.dtype(dtype).itemsize


def _sigmoid(x):
    return 1.0 / (1.0 + jnp.exp(-x))


def _silu(x):
    return x * _sigmoid(x)


def _layer_norm_rows(y, g, b):
    mu = jnp.mean(y, axis=-1, keepdims=True)
    yc = y - mu
    var = jnp.mean(yc * yc, axis=-1, keepdims=True)
    return yc * lax.rsqrt(var + LN_EPS) * g + b


def _dot(a, b):
    return jnp.dot(a, b, preferred_element_type=F32)


def _dot_nt(a, b):
    return lax.dot_general(a, b, (((1,), (1,)), ((), ())), preferred_element_type=F32)


def _ada_kernel(c_ref, w_ref, b_ref, o_ref):
    c = c_ref[...]
    s = _silu(c).astype(BF16)
    o_ref[...] = _dot(s, w_ref[...].astype(BF16)) + b_ref[...]


def _ada_call(c_all, ada_w, ada_b):
    rows = c_all.shape[0]
    n = ada_w.shape[-1]
    tn = 1024
    blocks = _nbytes((D_MODEL, tn), F32) + _nbytes((rows, D_MODEL), F32) + 2 * _nbytes((rows, tn), F32)
    return pl.pallas_call(
        _ada_kernel,
        grid=(DEPTH, n // tn),
        in_specs=[
            pl.BlockSpec((rows, D_MODEL), lambda l, j: (0, 0)),
            pl.BlockSpec((None, D_MODEL, tn), lambda l, j: (l, 0, j)),
            pl.BlockSpec((None, 1, tn), lambda l, j: (l, 0, j)),
        ],
        out_specs=pl.BlockSpec((None, rows, tn), lambda l, j: (l, 0, j)),
        out_shape=jax.ShapeDtypeStruct((DEPTH, rows, n), F32),
        compiler_params=pltpu.CompilerParams(
            dimension_semantics=("parallel", "parallel"),
            vmem_limit_bytes=_vmem_limit(blocks, temp_bytes=_nbytes((D_MODEL, tn), BF16))),
        name="ada",
    )(c_all, ada_w, ada_b.reshape(DEPTH, 1, n))


def _inproj_kernel(x_ref, mod_ref, w_ref, wg_ref, o_ref, g_ref, h_ref):
    @pl.when(pl.program_id(1) == 0)
    def _():
        m = mod_ref[...]
        h = x_ref[...] * (1.0 + m[MOD_SC_M:MOD_SC_M + 1, :]) + m[MOD_SH_M:MOD_SH_M + 1, :]
        hb = h.astype(BF16)
        h_ref[...] = hb
        g_ref[...] = _dot(hb, wg_ref[...])

    o_ref[...] = _dot(h_ref[...], w_ref[...]).astype(o_ref.dtype)


def _inproj_call(x2, mod, w_main, w_gate, seq):
    m = x2.shape[0]
    tm, tn = 512, PM_WIDTH // 2
    blocks = (_nbytes((tm, D_MODEL), F32) + _nbytes((MOD_ROWS, D_MODEL), F32) + _nbytes((D_MODEL, tn), BF16)
              + _nbytes((D_MODEL, LANES), BF16) + _nbytes((tm, tn), BF16) + _nbytes((tm, LANES), F32))
    scratch = _nbytes((tm, D_MODEL), BF16)
    return pl.pallas_call(
        _inproj_kernel,
        grid=(m // tm, PM_WIDTH // tn),
        in_specs=[
            pl.BlockSpec((tm, D_MODEL), lambda i, j: (i, 0)),
            pl.BlockSpec((None, MOD_ROWS, D_MODEL), lambda i, j: ((i * tm) // seq, 0, 0)),
            pl.BlockSpec((D_MODEL, tn), lambda i, j: (0, j)),
            pl.BlockSpec((D_MODEL, LANES), lambda i, j: (0, 0)),
        ],
        out_specs=[
            pl.BlockSpec((tm, tn), lambda i, j: (i, j)),
            pl.BlockSpec((tm, LANES), lambda i, j: (i, 0)),
        ],
        out_shape=[
            jax.ShapeDtypeStruct((m, PM_WIDTH), BF16),
            jax.ShapeDtypeStruct((m, LANES), F32),
        ],
        scratch_shapes=[pltpu.VMEM((tm, D_MODEL), BF16)],
        compiler_params=pltpu.CompilerParams(
            dimension_semantics=("parallel", "arbitrary"),
            vmem_limit_bytes=_vmem_limit(blocks, scratch, _nbytes((tm, D_MODEL), F32))),
        name="inproj",
    )(x2, mod, w_main, w_gate)


def _shifted_rows(x, halo8, shift, n_rows):
    k = abs(shift)
    rolled = pltpu.roll(x, (-shift) % n_rows, 0)
    r8 = lax.broadcasted_iota(jnp.int32, (SUBLANES, x.shape[1]), 0)
    if shift < 0:
        fix = pltpu.roll(halo8, k, 0)
        head = jnp.where(r8 < k, fix, rolled[:SUBLANES])
        return jnp.concatenate([head, rolled[SUBLANES:]], axis=0)
    fix = pltpu.roll(halo8, SUBLANES - k, 0)
    tail = jnp.where(r8 >= SUBLANES - k, fix, rolled[n_rows - SUBLANES:])
    return jnp.concatenate([rolled[:n_rows - SUBLANES], tail], axis=0)


def _gdnprep_kernel(x_ref, xp_ref, xn_ref, gate_ref, cw_ref, gp_ref,
                    q_ref, k_ref, v_ref, gcol_ref, grow_ref, *, tt):
    i = pl.program_id(1)
    first = i == 0
    last = i == pl.num_programs(1) - 1
    cw = cw_ref[...]
    for part, out_ref in enumerate((q_ref, k_ref, v_ref)):
        cols = slice(part * GDN_WIDTH, (part + 1) * GDN_WIDTH)
        x = x_ref[:, cols].astype(F32)
        prev8 = xp_ref[:, cols].astype(F32)[BF16_ROWS - SUBLANES:]
        next8 = xn_ref[:, cols].astype(F32)[:SUBLANES]
        prev8 = jnp.where(first, 0.0, prev8)
        next8 = jnp.where(last, 0.0, next8)
        w = cw[:, cols]
        acc = x * w[CONV_W // 2:CONV_W // 2 + 1, :]
        for tap in range(CONV_W):
            shift = tap - CONV_W // 2
            if shift == 0:
                continue
            halo = prev8 if shift < 0 else next8
            acc = acc + _shifted_rows(x, halo, shift, tt) * w[tap:tap + 1, :]
        y = _silu(acc)
        if part < 2:
            heads = []
            for h in range(GDN_HEADS):
                yh = y[:, h * HEAD_DIM:(h + 1) * HEAD_DIM]
                inv = lax.rsqrt(jnp.sum(yh * yh, axis=-1, keepdims=True) + NORM_EPS)
                if part == 0:
                    inv = inv * (HEAD_DIM ** -0.5)
                heads.append(yh * inv)
            y = jnp.concatenate(heads, axis=1)
        out_ref[...] = y.astype(out_ref.dtype)

    raw = gate_ref[...]
    gp = gp_ref[...]
    col = lax.broadcasted_iota(jnp.int32, raw.shape, 1)
    beta = _sigmoid(raw)
    z = raw + gp[0:1, :]
    softplus = jnp.maximum(z, 0.0) + jnp.log(1.0 + jnp.exp(-jnp.abs(z)))
    logdec = -jnp.exp(gp[1:2, :]) * softplus
    is_dec = (col >= N_GATE_COLS) & (col < 2 * N_GATE_COLS)
    gsrc = jnp.where(is_dec, logdec, 0.0)
    r = lax.broadcasted_iota(jnp.int32, (tt, tt), 0)
    c = lax.broadcasted_iota(jnp.int32, (tt, tt), 1)
    same = (r // GDN_CHUNK) == (c // GDN_CHUNK)
    p_lo = jnp.where(same & (c <= r), 1.0, 0.0).astype(F32)
    p_up = jnp.where(same & (c >= r), 1.0, 0.0).astype(F32)
    cum_lo = jnp.dot(p_lo, gsrc, precision=lax.Precision.HIGHEST, preferred_element_type=F32)
    cum_up = jnp.dot(p_up, gsrc, precision=lax.Precision.HIGHEST, preferred_element_type=F32)
    total = cum_lo + cum_up - gsrc
    fwd_col = col < N_GATE_COLS + GDN_HEADS
    gc = jnp.where(fwd_col, cum_lo, cum_up)
    tot_shift = pltpu.roll(total, N_GATE_COLS, 1)
    gcol = jnp.where(col < N_GATE_COLS, beta,
                     jnp.where(col < 2 * N_GATE_COLS, gc,
                               jnp.where(col < 3 * N_GATE_COLS, tot_shift, 0.0)))
    gcol_ref[...] = gcol
    grow_ref[...] = gcol.T


def _gdnprep_call(pm, gates, conv_w8, gate_params):
    bsz, seq, _ = pm.shape
    tt = 256
    hb = tt // BF16_ROWS
    n_halo = seq // BF16_ROWS
    blocks = (_nbytes((tt, GDN_QKV), BF16) + 2 * _nbytes((BF16_ROWS, GDN_QKV), BF16) + _nbytes((tt, LANES), F32)
              + _nbytes((SUBLANES, GDN_QKV), F32) + 3 * _nbytes((tt, GDN_WIDTH), BF16) + 2 * _nbytes((tt, LANES), F32))
    temps = 8 * _nbytes((tt, GDN_WIDTH), F32) + 4 * _nbytes((tt, tt), F32)
    return pl.pallas_call(
        functools.partial(_gdnprep_kernel, tt=tt),
        grid=(bsz, seq // tt),
        in_specs=[
            pl.BlockSpec((None, tt, GDN_QKV), lambda b, i: (b, i, 0)),
            pl.BlockSpec((None, BF16_ROWS, GDN_QKV), lambda b, i: (b, jnp.maximum(i * hb - 1, 0), 0)),
            pl.BlockSpec((None, BF16_ROWS, GDN_QKV), lambda b, i: (b, jnp.minimum((i + 1) * hb, n_halo - 1), 0)),
            pl.BlockSpec((None, tt, LANES), lambda b, i: (b, i, 0)),
            pl.BlockSpec((SUBLANES, GDN_QKV), lambda b, i: (0, 0)),
            pl.BlockSpec((SUBLANES, LANES), lambda b, i: (0, 0)),
        ],
        out_specs=[
            pl.BlockSpec((None, tt, GDN_WIDTH), lambda b, i: (b, i, 0)),
            pl.BlockSpec((None, tt, GDN_WIDTH), lambda b, i: (b, i, 0)),
            pl.BlockSpec((None, tt, GDN_WIDTH), lambda b, i: (b, i, 0)),
            pl.BlockSpec((None, tt, LANES), lambda b, i: (b, i, 0)),
            pl.BlockSpec((None, LANES, tt), lambda b, i: (b, 0, i)),
        ],
        out_shape=[
            jax.ShapeDtypeStruct((bsz, seq, GDN_WIDTH), BF16),
            jax.ShapeDtypeStruct((bsz, seq, GDN_WIDTH), BF16),
            jax.ShapeDtypeStruct((bsz, seq, GDN_WIDTH), BF16),
            jax.ShapeDtypeStruct((bsz, seq, LANES), F32),
            jax.ShapeDtypeStruct((bsz, LANES, seq), F32),
        ],
        compiler_params=pltpu.CompilerParams(
            dimension_semantics=("parallel", "parallel"),
            vmem_limit_bytes=_vmem_limit(blocks, temp_bytes=temps)),
        name="gdnprep",
    )(pm, pm, pm, gates, conv_w8, gate_params)


def _gdn_kernel(qf_ref, kf_ref, vf_ref, gcf_ref, grf_ref,
                qb_ref, kb_ref, vb_ref, gcb_ref, grb_ref,
                of_ref, ob_ref, state_ref):
    @pl.when(pl.program_id(1) == 0)
    def _():
        state_ref[...] = jnp.zeros_like(state_ref)

    n = GDN_CHUNK
    r = lax.broadcasted_iota(jnp.int32, (n, n), 0)
    c = lax.broadcasted_iota(jnp.int32, (n, n), 1)
    eye = jnp.where(r == c, 1.0, 0.0).astype(F32)

    units = []
    for d, (q_ref, k_ref, v_ref, gc_ref, gr_ref, o_ref) in enumerate((
            (qf_ref, kf_ref, vf_ref, gcf_ref, grf_ref, of_ref),
            (qb_ref, kb_ref, vb_ref, gcb_ref, grb_ref, ob_ref))):
        gcol = gc_ref[...]
        grow = gr_ref[...]
        reverse = d == 1
        incl = (r <= c) if reverse else (r >= c)
        strict = (r < c) if reverse else (r > c)
        for h in range(GDN_HEADS):
            lanes = slice(h * HEAD_DIM, (h + 1) * HEAD_DIM)
            j = d * GDN_HEADS + h
            beta = gcol[:, j:j + 1]
            gcc = gcol[:, N_GATE_COLS + j:N_GATE_COLS + j + 1]
            gtc = gcol[:, 2 * N_GATE_COLS + j:2 * N_GATE_COLS + j + 1]
            gcr = grow[N_GATE_COLS + j:N_GATE_COLS + j + 1, :]
            gtr = grow[2 * N_GATE_COLS + j:2 * N_GATE_COLS + j + 1, :]
            q = q_ref[:, lanes]
            k = k_ref[:, lanes]
            kf = k.astype(F32)
            kb = kf * beta
            egc = jnp.exp(gcc)
            units.append(dict(
                d=d, h=h, lanes=lanes, o_ref=o_ref, strict=strict, q=q, k=k,
                decay=jnp.where(incl, jnp.exp(jnp.where(incl, gcc - gcr, 0.0)), 0.0),
                kbb=kb.astype(BF16),
                vb=(v_ref[:, lanes].astype(F32) * beta).astype(BF16),
                kbg=(kb * egc).astype(BF16),
                qg=(q.astype(F32) * egc).astype(BF16),
                kdec_t=(kf * jnp.exp(gtc - gcc)).T.astype(BF16),
                sdec=jnp.exp(gtr)))

    for u in units:
        aq = _dot_nt(jnp.concatenate([u["kbb"], u["q"]], axis=0), u["k"])
        low = jnp.where(u["strict"], aq[:n] * u["decay"], 0.0)
        u["qk"] = (aq[n:] * u["decay"]).astype(BF16)
        u["tinv"] = eye - low
        u["lb"] = low.astype(BF16)

    for u in units:
        u["power"] = _dot(u["lb"], u["lb"])
    n_factors = (n - 1).bit_length() - 1
    for it in range(n_factors):
        for u in units:
            pb = u["power"].astype(BF16)
            if it + 1 < n_factors:
                both = _dot(jnp.concatenate([u["tinv"].astype(BF16), pb], axis=0), pb)
                u["tinv"] = u["tinv"] + both[:n]
                u["power"] = both[n:]
            else:
                u["tinv"] = u["tinv"] + _dot(u["tinv"].astype(BF16), pb)

    for u in units:
        u["uw"] = _dot(u["tinv"].astype(BF16), jnp.concatenate([u["vb"], u["kbg"]], axis=1))
    for u in units:
        u["state"] = state_ref[u["d"], u["h"]]
        w = u["uw"][:, HEAD_DIM:].astype(BF16)
        u["ws"] = _dot(jnp.concatenate([w, u["qg"]], axis=0), u["state"].astype(BF16))
    for u in units:
        v_new = (u["uw"][:, :HEAD_DIM] - u["ws"][:n]).astype(BF16)
        u["os"] = _dot(jnp.concatenate([u["qk"], u["kdec_t"]], axis=0), v_new)
    for u in units:
        u["o_ref"][:, u["lanes"]] = u["ws"][n:] + u["os"][:n]
        state_ref[u["d"], u["h"]] = u["state"] * u["sdec"] + u["os"][n:]


def _gdn_call(q, k, v, gcol, grow):
    bsz, seq, _ = q.shape
    tb = GDN_CHUNK
    nb = seq // tb
    qkv_spec_f = pl.BlockSpec((None, tb, GDN_WIDTH), lambda b, i: (b, i, 0))
    qkv_spec_b = pl.BlockSpec((None, tb, GDN_WIDTH), lambda b, i: (b, nb - 1 - i, 0))
    gc_spec_f = pl.BlockSpec((None, tb, LANES), lambda b, i: (b, i, 0))
    gc_spec_b = pl.BlockSpec((None, tb, LANES), lambda b, i: (b, nb - 1 - i, 0))
    gr_spec_f = pl.BlockSpec((None, LANES, tb), lambda b, i: (b, 0, i))
    gr_spec_b = pl.BlockSpec((None, LANES, tb), lambda b, i: (b, 0, nb - 1 - i))
    blocks = 2 * (3 * _nbytes((tb, GDN_WIDTH), BF16) + 2 * _nbytes((tb, LANES), F32) + _nbytes((tb, GDN_WIDTH), F32))
    scratch = _nbytes((2, GDN_HEADS, HEAD_DIM, HEAD_DIM), F32)
    temps = 2 * GDN_HEADS * 24 * _nbytes((GDN_CHUNK, LANES), F32)
    return pl.pallas_call(
        _gdn_kernel,
        grid=(bsz, nb),
        in_specs=[qkv_spec_f, qkv_spec_f, qkv_spec_f, gc_spec_f, gr_spec_f,
                  qkv_spec_b, qkv_spec_b, qkv_spec_b, gc_spec_b, gr_spec_b],
        out_specs=[qkv_spec_f, qkv_spec_b],
        out_shape=[jax.ShapeDtypeStruct((bsz, seq, GDN_WIDTH), F32),
                   jax.ShapeDtypeStruct((bsz, seq, GDN_WIDTH), F32)],
        scratch_shapes=[pltpu.VMEM((2, GDN_HEADS, HEAD_DIM, HEAD_DIM), F32)],
        compiler_params=pltpu.CompilerParams(
            dimension_semantics=("parallel", "arbitrary"),
            vmem_limit_bytes=_vmem_limit(blocks, scratch, temps)),
        name="gdn",
    )(q, k, v, gcol, grow, q, k, v, gcol, grow)


def _rope_tables(seq):
    t = jnp.arange(seq)
    row = (t // GRID_W).astype(F32)
    col = (t % GRID_W).astype(F32)
    half = HEAD_DIM // 2
    inv_freq = ROPE_THETA ** (-jnp.arange(0, half, 2, dtype=F32) / half)
    ang_r = row[:, None] * inv_freq
    ang_c = col[:, None] * inv_freq
    ang = jnp.concatenate([ang_r, ang_r, ang_c, ang_c], -1)
    cos, sin = jnp.cos(ang), jnp.sin(ang)
    lane = jnp.arange(HEAD_DIM)
    first = (lane // (HEAD_DIM // 4)) % 2 == 0
    sin_up = jnp.where(first, -sin, 0.0)
    sin_dn = jnp.where(first, 0.0, sin)
    return cos, sin_up, sin_dn


def _norm_rope(x, w, cos, sin_up, sin_dn, scale):
    xn = x * lax.rsqrt(jnp.mean(x * x, axis=-1, keepdims=True) + NORM_EPS) * w
    quarter = HEAD_DIM // 4
    up = pltpu.roll(xn, HEAD_DIM - quarter, 1)
    dn = pltpu.roll(xn, quarter, 1)
    y = xn * cos + up * sin_up + dn * sin_dn
    return y * scale if scale != 1.0 else y


ATT_TK = 1024
FLASH_SUB = 256
LOG2_E = 1.4426950408889634


def _attprep_kernel(aq_ref, ak_ref, av_ref, cos_ref, su_ref, sd_ref, qw_ref, kw_ref, qt_ref, k_ref, vt_ref):
    cos, su, sd = cos_ref[...], su_ref[...], sd_ref[...]
    qw, kw = qw_ref[...], kw_ref[...]
    for h in range(ATT_HEADS):
        lanes = slice(h * HEAD_DIM, (h + 1) * HEAD_DIM)
        q = _norm_rope(aq_ref[:, lanes].astype(F32), qw, cos, su, sd, HEAD_DIM ** -0.5 * LOG2_E)
        qt_ref[lanes, :] = q.T.astype(qt_ref.dtype)
    for h in range(ATT_KV_HEADS):
        lanes = slice(h * HEAD_DIM, (h + 1) * HEAD_DIM)
        k_ref[:, lanes] = _norm_rope(ak_ref[:, lanes].astype(F32), kw, cos, su, sd, 1.0).astype(k_ref.dtype)
        vt_ref[h] = av_ref[:, lanes].astype(F32).T.astype(vt_ref.dtype)


def _attprep_call(pm, cos, sin_up, sin_dn, q_norm_w, k_norm_w):
    bsz, seq, _ = pm.shape
    tt = ATT_TK
    tab = pl.BlockSpec((tt, HEAD_DIM), lambda b, i: (i, 0))
    vec = pl.BlockSpec((1, HEAD_DIM), lambda b, i: (0, 0))
    blocks = (2 * _nbytes((tt, ATT_WIDTH), BF16) + 4 * _nbytes((tt, ATT_KV_WIDTH), BF16)
              + 3 * _nbytes((tt, HEAD_DIM), F32))
    return pl.pallas_call(
        _attprep_kernel,
        grid=(bsz, seq // tt),
        in_specs=[
            pl.BlockSpec((None, tt, ATT_WIDTH), lambda b, i: (b, i, PM_AQ // ATT_WIDTH)),
            pl.BlockSpec((None, tt, ATT_KV_WIDTH), lambda b, i: (b, i, PM_AK // ATT_KV_WIDTH)),
            pl.BlockSpec((None, tt, ATT_KV_WIDTH), lambda b, i: (b, i, PM_AV // ATT_KV_WIDTH)),
            tab, tab, tab, vec, vec,
        ],
        out_specs=[
            pl.BlockSpec((None, ATT_WIDTH, tt), lambda b, i: (b, 0, i)),
            pl.BlockSpec((None, tt, ATT_KV_WIDTH), lambda b, i: (b, i, 0)),
            pl.BlockSpec((None, ATT_KV_HEADS, None, HEAD_DIM, tt), lambda b, i: (b, 0, i, 0, 0)),
        ],
        out_shape=[jax.ShapeDtypeStruct((bsz, ATT_WIDTH, seq), BF16),
                   jax.ShapeDtypeStruct((bsz, seq, ATT_KV_WIDTH), BF16),
                   jax.ShapeDtypeStruct((bsz, ATT_KV_HEADS, seq // tt, HEAD_DIM, tt), BF16)],
        compiler_params=pltpu.CompilerParams(
            dimension_semantics=("parallel", "parallel"),
            vmem_limit_bytes=_vmem_limit(blocks, temp_bytes=8 * _nbytes((tt, HEAD_DIM), F32))),
        name="attprep",
    )(pm, pm, pm, cos, sin_up, sin_dn, q_norm_w.reshape(1, HEAD_DIM), k_norm_w.reshape(1, HEAD_DIM))


def _flash_kernel(qt_ref, k_ref, vt_ref, o_ref, m_ref, l_ref, acc_ref, qc_ref, sa_ref, sb_ref, mca_ref, mcb_ref,
                  *, tq, tk, ts, n_k):
    m_ref[...] = jnp.full_like(m_ref, -jnp.inf)
    l_ref[...] = jnp.zeros_like(l_ref)
    acc_ref[...] = jnp.zeros_like(acc_ref)
    for g in range(ATT_GROUP):
        qc_ref[:, g * tq:(g + 1) * tq] = qt_ref[g * HEAD_DIM:(g + 1) * HEAD_DIM, :]

    n_sub = tk // ts

    bufs = ((sa_ref, mca_ref), (sb_ref, mcb_ref))

    def scores(j, s_ref, i):
        start = pl.multiple_of(j * tk + i * ts, ts)
        st = _dot(k_ref[pl.ds(start, ts), :], qc_ref[...])
        s_ref[i * ts:(i + 1) * ts, :] = st
        return jnp.max(st, axis=0, keepdims=True)

    def weigh(j, s_ref, i, m_new):
        p = jnp.exp2(s_ref[i * ts:(i + 1) * ts, :] - m_new)
        return jnp.sum(p, axis=0, keepdims=True), _dot(vt_ref[j, :, i * ts:(i + 1) * ts], p.astype(BF16))

    def step(j, parity, with_next):
        s_cur, mc_cur = bufs[parity]
        s_nxt, mc_nxt = bufs[1 - parity]
        m_prev = m_ref[...]
        m_new = jnp.maximum(m_prev, mc_cur[...])
        alpha = jnp.exp2(m_prev - m_new)
        mc_next, l_new, pv = None, None, None
        for i in range(n_sub):
            if with_next:
                mx = scores(j + 1, s_nxt, i)
                mc_next = mx if mc_next is None else jnp.maximum(mc_next, mx)
            ls, pvi = weigh(j, s_cur, i, m_new)
            l_new = ls if l_new is None else l_new + ls
            pv = pvi if pv is None else pv + pvi
        if with_next:
            mc_nxt[...] = mc_next
        l_ref[...] = alpha * l_ref[...] + l_new
        acc_ref[...] = alpha * acc_ref[...] + pv
        m_ref[...] = m_new

    mc0 = None
    for i in range(n_sub):
        mx = scores(0, sa_ref, i)
        mc0 = mx if mc0 is None else jnp.maximum(mc0, mx)
    mca_ref[...] = mc0

    def body(jj, carry):
        step(2 * jj, 0, True)
        step(2 * jj + 1, 1, True)
        return carry

    n_pairs = (n_k - 1) // 2
    lax.fori_loop(0, n_pairs, body, 0)
    for j in range(2 * n_pairs, n_k):
        step(j, j % 2, j + 1 < n_k)
    for g in range(ATT_GROUP):
        lanes = slice(g * tq, (g + 1) * tq)
        out_t = acc_ref[:, lanes] / l_ref[:, lanes]
        o_ref[:, g * HEAD_DIM:(g + 1) * HEAD_DIM] = out_t.T.astype(o_ref.dtype)


def _flash_call(qt, kr, vt):
    bsz, _, seq = qt.shape
    tq, tk = 256, ATT_TK
    gw = ATT_GROUP * HEAD_DIM
    nq = ATT_GROUP * tq
    blocks = 2 * _nbytes((gw, tq), BF16) + 2 * _nbytes((seq, HEAD_DIM), BF16)
    scratch = (4 * _nbytes((SUBLANES, nq), F32) + _nbytes((HEAD_DIM, nq), F32) + _nbytes((HEAD_DIM, nq), BF16)
               + 2 * _nbytes((tk, nq), F32))
    temps = 2 * _nbytes((tk, nq), F32)
    return pl.pallas_call(
        functools.partial(_flash_kernel, tq=tq, tk=tk, ts=FLASH_SUB, n_k=seq // tk),
        grid=(bsz, ATT_KV_HEADS, seq // tq),
        in_specs=[
            pl.BlockSpec((None, gw, tq), lambda b, h, i: (b, h, i)),
            pl.BlockSpec((None, seq, HEAD_DIM), lambda b, h, i: (b, 0, h)),
            pl.BlockSpec((None, None, seq // tk, HEAD_DIM, tk), lambda b, h, i: (b, h, 0, 0, 0)),
        ],
        out_specs=pl.BlockSpec((None, tq, gw), lambda b, h, i: (b, i, h)),
        out_shape=jax.ShapeDtypeStruct((bsz, seq, ATT_WIDTH), BF16),
        scratch_shapes=[pltpu.VMEM((1, nq), F32), pltpu.VMEM((1, nq), F32),
                        pltpu.VMEM((HEAD_DIM, nq), F32), pltpu.VMEM((HEAD_DIM, nq), BF16),
                        pltpu.VMEM((tk, nq), F32), pltpu.VMEM((tk, nq), F32),
                        pltpu.VMEM((1, nq), F32), pltpu.VMEM((1, nq), F32)],
        compiler_params=pltpu.CompilerParams(
            dimension_semantics=("parallel", "parallel", "arbitrary"),
            vmem_limit_bytes=_vmem_limit(blocks, scratch, temps)),
        name="flash",
    )(qt, kr, vt)


def _outproj_kernel(of_ref, ob_ref, z_ref, oa_ref, x_ref, mod_ref, w_ref, nw_ref, lg_ref, lb_ref, o_ref):
    nw = nw_ref[...]
    heads = []
    for h in range(GDN_HEADS):
        lanes = slice(h * HEAD_DIM, (h + 1) * HEAD_DIM)
        o = of_ref[:, lanes] + ob_ref[:, lanes]
        on = o * lax.rsqrt(jnp.mean(o * o, axis=-1, keepdims=True) + NORM_EPS) * nw
        heads.append((on * _silu(z_ref[:, lanes].astype(F32))).astype(BF16))
    og = jnp.concatenate(heads, axis=1)
    mix = _dot(og, w_ref[:GDN_WIDTH, :]) + _dot(oa_ref[...], w_ref[GDN_WIDTH:, :])
    m = mod_ref[...]
    y = DEEPNORM_ALPHA * x_ref[...] + m[MOD_G_M:MOD_G_M + 1, :] * mix
    o_ref[...] = _layer_norm_rows(y, lg_ref[...], lb_ref[...])


def _outproj_call(o_f, o_b, pm2, o_att, x2, mod, w_out, gdn_norm_w, ln_g, ln_b, seq):
    m = x2.shape[0]
    tm = 256
    blocks = (2 * _nbytes((tm, GDN_WIDTH), F32) + 2 * _nbytes((tm, GDN_WIDTH), BF16)
              + 2 * _nbytes((tm, D_MODEL), F32) + _nbytes((MOD_ROWS, D_MODEL), F32)
              + _nbytes((D_MODEL, D_MODEL), BF16))
    vec = pl.BlockSpec((1, D_MODEL), lambda i: (0, 0))
    return pl.pallas_call(
        _outproj_kernel,
        grid=(m // tm,),
        in_specs=[
            pl.BlockSpec((tm, GDN_WIDTH), lambda i: (i, 0)),
            pl.BlockSpec((tm, GDN_WIDTH), lambda i: (i, 0)),
            pl.BlockSpec((tm, GDN_WIDTH), lambda i: (i, PM_Z // GDN_WIDTH)),
            pl.BlockSpec((tm, ATT_WIDTH), lambda i: (i, 0)),
            pl.BlockSpec((tm, D_MODEL), lambda i: (i, 0)),
            pl.BlockSpec((None, MOD_ROWS, D_MODEL), lambda i: ((i * tm) // seq, 0, 0)),
            pl.BlockSpec((D_MODEL, D_MODEL), lambda i: (0, 0)),
            pl.BlockSpec((1, HEAD_DIM), lambda i: (0, 0)),
            vec, vec,
        ],
        out_specs=pl.BlockSpec((tm, D_MODEL), lambda i: (i, 0)),
        out_shape=jax.ShapeDtypeStruct((m, D_MODEL), F32),
        compiler_params=pltpu.CompilerParams(
            dimension_semantics=("parallel",),
            vmem_limit_bytes=_vmem_limit(blocks, temp_bytes=4 * _nbytes((tm, D_MODEL), F32))),
        name="outproj",
    )(o_f, o_b, pm2, o_att, x2, mod, w_out, gdn_norm_w.reshape(1, HEAD_DIM),
      ln_g.reshape(1, D_MODEL), ln_b.reshape(1, D_MODEL))


POOL_HALO = 128


def _pool_kernel(x_ref, xp_ref, xn_ref, mod_ref, pw_ref, ps_ref, lg_ref, lb_ref, o_ref, *, tt, seq):
    t0 = pl.program_id(1) * tt
    m = mod_ref[...]
    sc = 1.0 + m[MOD_SC_M:MOD_SC_M + 1, :]
    sh = m[MOD_SH_M:MOD_SH_M + 1, :]
    x = x_ref[...]
    h_main = x * sc + sh
    h_ext = jnp.concatenate([xp_ref[...] * sc + sh, h_main, xn_ref[...] * sc + sh], axis=0)
    ke = tt + 2 * POOL_HALO
    pr = t0 + lax.broadcasted_iota(jnp.int32, (tt, ke), 0)
    pc = t0 - POOL_HALO + lax.broadcasted_iota(jnp.int32, (tt, ke), 1)
    pos = t0 + lax.broadcasted_iota(jnp.int32, (tt, 1), 0)
    valid = (pc >= 0) & (pc < seq)
    outs = []
    for gi, win in enumerate(POOL_WINDOWS):
        back, fwd = win // 2, win - 1 - win // 2
        band = jnp.where(valid & (pc >= pr - back) & (pc <= pr + fwd), 1.0, 0.0).astype(BF16)
        cnt = (jnp.minimum(pos + fwd, seq - 1) - jnp.maximum(pos - back, 0) + 1).astype(F32)
        lanes = slice(gi * POOL_GROUP, (gi + 1) * POOL_GROUP)
        he = h_ext[:, lanes]
        hi = he.astype(BF16)
        lo = (he - hi.astype(F32)).astype(BF16)
        wsum = _dot(band, hi) + _dot(band, lo)
        diff = wsum / cnt - h_main[:, lanes]
        outs.append(_dot(diff.astype(BF16), pw_ref[gi]))
    mix = jnp.concatenate(outs, axis=1) * ps_ref[...]
    y = DEEPNORM_ALPHA * x + m[MOD_G_M:MOD_G_M + 1, :] * mix
    o_ref[...] = _layer_norm_rows(y, lg_ref[...], lb_ref[...])


def _pool_call(x, mod, pool_w, pool_scale, ln_g, ln_b):
    bsz, seq, _ = x.shape
    tt = 256
    hb = tt // POOL_HALO
    n_halo = seq // POOL_HALO
    vec = pl.BlockSpec((1, D_MODEL), lambda b, i: (0, 0))
    blocks = (2 * _nbytes((tt, D_MODEL), F32) + 2 * _nbytes((POOL_HALO, D_MODEL), F32)
              + _nbytes((MOD_ROWS, D_MODEL), F32) + _nbytes(pool_w.shape, BF16))
    temps = 4 * _nbytes((tt + 2 * POOL_HALO, D_MODEL), F32)
    return pl.pallas_call(
        functools.partial(_pool_kernel, tt=tt, seq=seq),
        grid=(bsz, seq // tt),
        in_specs=[
            pl.BlockSpec((None, tt, D_MODEL), lambda b, i: (b, i, 0)),
            pl.BlockSpec((None, POOL_HALO, D_MODEL), lambda b, i: (b, jnp.maximum(i * hb - 1, 0), 0)),
            pl.BlockSpec((None, POOL_HALO, D_MODEL), lambda b, i: (b, jnp.minimum((i + 1) * hb, n_halo - 1), 0)),
            pl.BlockSpec((None, MOD_ROWS, D_MODEL), lambda b, i: (b, 0, 0)),
            pl.BlockSpec(pool_w.shape, lambda b, i: (0, 0, 0)),
            vec, vec, vec,
        ],
        out_specs=pl.BlockSpec((None, tt, D_MODEL), lambda b, i: (b, i, 0)),
        out_shape=jax.ShapeDtypeStruct((bsz, seq, D_MODEL), F32),
        compiler_params=pltpu.CompilerParams(
            dimension_semantics=("parallel", "parallel"),
            vmem_limit_bytes=_vmem_limit(blocks, temp_bytes=temps)),
        name="pool",
    )(x, x, x, mod, pool_w, pool_scale.reshape(1, D_MODEL), ln_g.reshape(1, D_MODEL), ln_b.reshape(1, D_MODEL))


def _mlp_kernel(x_ref, mod_ref, w1_ref, w2_ref, lg_ref, lb_ref, o_ref, h_ref, acc_ref):
    j = pl.program_id(1)

    @pl.when(j == 0)
    def _():
        m = mod_ref[...]
        h = x_ref[...] * (1.0 + m[MOD_SC_F:MOD_SC_F + 1, :]) + m[MOD_SH_F:MOD_SH_F + 1, :]
        h_ref[...] = h.astype(BF16)
        acc_ref[...] = jnp.zeros_like(acc_ref)

    u = jnp.maximum(_dot(h_ref[...], w1_ref[...]), 0.0)
    acc_ref[...] += _dot((u * u).astype(BF16), w2_ref[...])

    @pl.when(j == pl.num_programs(1) - 1)
    def _():
        m = mod_ref[...]
        y = DEEPNORM_ALPHA * x_ref[...] + m[MOD_G_F:MOD_G_F + 1, :] * acc_ref[...]
        o_ref[...] = _layer_norm_rows(y, lg_ref[...], lb_ref[...])


def _mlp_call(x2, mod, w1, w2, ln_g, ln_b, seq):
    m = x2.shape[0]
    tm, tf = 512, 1024
    vec = pl.BlockSpec((1, D_MODEL), lambda i, j: (0, 0))
    blocks = (2 * _nbytes((tm, D_MODEL), F32) + _nbytes((MOD_ROWS, D_MODEL), F32)
              + 2 * _nbytes((D_MODEL, tf), BF16))
    scratch = _nbytes((tm, D_MODEL), BF16) + _nbytes((tm, D_MODEL), F32)
    temps = 2 * _nbytes((tm, tf), F32) + _nbytes((tm, D_MODEL), F32)
    return pl.pallas_call(
        _mlp_kernel,
        grid=(m // tm, D_FF // tf),
        in_specs=[
            pl.BlockSpec((tm, D_MODEL), lambda i, j: (i, 0)),
            pl.BlockSpec((None, MOD_ROWS, D_MODEL), lambda i, j: ((i * tm) // seq, 0, 0)),
            pl.BlockSpec((D_MODEL, tf), lambda i, j: (0, j)),
            pl.BlockSpec((tf, D_MODEL), lambda i, j: (j, 0)),
            vec, vec,
        ],
        out_specs=pl.BlockSpec((tm, D_MODEL), lambda i, j: (i, 0)),
        out_shape=jax.ShapeDtypeStruct((m, D_MODEL), F32),
        scratch_shapes=[pltpu.VMEM((tm, D_MODEL), BF16), pltpu.VMEM((tm, D_MODEL), F32)],
        compiler_params=pltpu.CompilerParams(
            dimension_semantics=("parallel", "arbitrary"),
            vmem_limit_bytes=_vmem_limit(blocks, scratch, temps)),
        name="mlp",
    )(x2, mod, w1, w2, ln_g.reshape(1, D_MODEL), ln_b.reshape(1, D_MODEL))


def _mod_table(cond_rows):
    bsz = cond_rows.shape[0]
    t = cond_rows.reshape(bsz, 6, D_MODEL)
    return jnp.pad(t, ((0, 0), (0, MOD_ROWS - 6), (0, 0)))


def _pack_weights(w_in, conv_w, a_log, dt_bias, w_out, pool_w, mlp_w1, mlp_w2):
    wi = w_in[0]
    w_main = jnp.concatenate(
        [wi[:, :OFF_BETA], wi[:, OFF_AQ:]], axis=1).astype(BF16)
    w_gate = jnp.pad(wi[:, OFF_BETA:OFF_AQ], ((0, 0), (0, LANES - 2 * N_GATE_COLS))).astype(BF16)
    conv_w8 = jnp.pad(conv_w[0], ((0, SUBLANES - CONV_W), (0, 0)))
    gate_params = jnp.zeros((SUBLANES, LANES), F32)
    gate_params = gate_params.at[0, N_GATE_COLS:2 * N_GATE_COLS].set(dt_bias[0].reshape(-1))
    gate_params = gate_params.at[1, N_GATE_COLS:2 * N_GATE_COLS].set(a_log[0].reshape(-1))
    return dict(w_main=w_main, w_gate=w_gate, conv_w8=conv_w8, gate_params=gate_params,
                w_out=w_out[0].astype(BF16), pool_w=pool_w[0].astype(BF16),
                w1=mlp_w1.astype(BF16), w2=mlp_w2.astype(BF16))


def _trunk(x, cond, pk, gdn_norm_w, q_norm_w, k_norm_w, pool_scale, ln_g, ln_b):
    bsz, seq, _ = x.shape
    m = bsz * seq
    mod0 = _mod_table(cond[0])
    mod1 = _mod_table(cond[1])
    x2 = x.reshape(m, D_MODEL)

    pm2, gates2 = _inproj_call(x2, mod0, pk["w_main"], pk["w_gate"], seq)
    pm = pm2.reshape(bsz, seq, PM_WIDTH)
    gq, gk, gv, gcol, grow = _gdnprep_call(pm, gates2.reshape(bsz, seq, LANES), pk["conv_w8"], pk["gate_params"])
    o_f, o_b = _gdn_call(gq, gk, gv, gcol, grow)
    cos, sin_up, sin_dn = _rope_tables(seq)
    qt, kr, vt = _attprep_call(pm, cos, sin_up, sin_dn, q_norm_w[0], k_norm_w[0])
    o_att = _flash_call(qt, kr, vt)
    x2 = _outproj_call(o_f.reshape(m, GDN_WIDTH), o_b.reshape(m, GDN_WIDTH), pm2, o_att.reshape(m, ATT_WIDTH),
                       x2, mod0, pk["w_out"], gdn_norm_w[0], ln_g[0, 0], ln_b[0, 0], seq)
    x2 = _mlp_call(x2, mod0, pk["w1"][0], pk["w2"][0], ln_g[0, 1], ln_b[0, 1], seq)

    x3 = _pool_call(x2.reshape(bsz, seq, D_MODEL), mod1, pk["pool_w"], pool_scale[0], ln_g[1, 0], ln_b[1, 0])
    x2 = _mlp_call(x3.reshape(m, D_MODEL), mod1, pk["w1"][1], pk["w2"][1], ln_g[1, 1], ln_b[1, 1], seq)
    return x2.reshape(bsz, seq, D_MODEL)


def kernel(x_prompt, x_sample, c_prompt, c_sample, w_in, conv_w, a_log, dt_bias, gdn_norm_w, q_norm_w,
           k_norm_w, w_out, pool_w, pool_scale, mlp_w1, mlp_w2, ada_w, ada_b, ln_g, ln_b):
    bp, bs = c_prompt.shape[0], c_sample.shape[0]
    c_all = jnp.concatenate([c_prompt, c_sample], axis=0)
    c_all = jnp.pad(c_all, ((0, (-c_all.shape[0]) % SUBLANES), (0, 0)))
    cond = _ada_call(c_all, ada_w, ada_b)
    pk = _pack_weights(w_in, conv_w, a_log, dt_bias, w_out, pool_w, mlp_w1, mlp_w2)
    y_prompt = _trunk(x_prompt, cond[:, :bp], pk, gdn_norm_w, q_norm_w, k_norm_w, pool_scale, ln_g, ln_b)
    y_sample = _trunk(x_sample, cond[:, bp:bp + bs], pk, gdn_norm_w, q_norm_w, k_norm_w, pool_scale, ln_g, ln_b)
    return (y_prompt, y_sample)
```

```python
import functools

import jax
import jax.numpy as jnp
from jax import lax
from jax.experimental import pallas as pl
from jax.experimental.pallas import tpu as pltpu

F32 = jnp.float32
BF16 = jnp.bfloat16

D_MODEL = 2048
DEPTH = 2
GRID_W = 64
HEAD_DIM = 128
GDN_HEADS = 8
ATT_HEADS = 8
ATT_KV_HEADS = 2
ATT_GROUP = ATT_HEADS // ATT_KV_HEADS
GDN_WIDTH = GDN_HEADS * HEAD_DIM
ATT_WIDTH = ATT_HEADS * HEAD_DIM
ATT_KV_WIDTH = ATT_KV_HEADS * HEAD_DIM
CONV_W = 5
ROPE_THETA = 10000.0
POOL_WINDOWS = (2, 4, 8, 16)
N_POOL_GROUPS = 4
POOL_GROUP = D_MODEL // N_POOL_GROUPS
D_FF = 4 * D_MODEL
DEEPNORM_ALPHA = (2 * DEPTH) ** 0.25
NORM_EPS = 1e-6
LN_EPS = 1e-5

GDN_QKV = 3 * GDN_WIDTH
OFF_Z = GDN_QKV
OFF_BETA = OFF_Z + GDN_WIDTH
OFF_A = OFF_BETA + 2 * GDN_HEADS
OFF_AQ = OFF_A + 2 * GDN_HEADS
OFF_AK = OFF_AQ + ATT_WIDTH
OFF_AV = OFF_AK + ATT_KV_WIDTH
D_IN = OFF_AV + ATT_KV_WIDTH

PM_QKV = 0
PM_Z = GDN_QKV
PM_AQ = PM_Z + GDN_WIDTH
PM_AK = PM_AQ + ATT_WIDTH
PM_AV = PM_AK + ATT_KV_WIDTH
PM_WIDTH = PM_AV + ATT_KV_WIDTH

LANES = 128
SUBLANES = 8
BF16_ROWS = 16
V7X_VMEM_BYTES = 64 * 1024 * 1024

GDN_CHUNK = 128
N_GATE_COLS = 2 * GDN_HEADS

MOD_SH_M, MOD_SC_M, MOD_G_M, MOD_SH_F, MOD_SC_F, MOD_G_F = range(6)
MOD_ROWS = 8


def _vmem_limit(block_bytes, scratch_bytes=0, temp_bytes=0):
    need = 2 * block_bytes + scratch_bytes + temp_bytes
    return int(min(need + need // 4, V7X_VMEM_BYTES - 8 * 1024 * 1024))


def _nbytes(shape, dtype):
    n = 1
    for s in shape:
        n *= s
    return n * jnp.dtype(dtype).itemsize


def _sigmoid(x):
    return 1.0 / (1.0 + jnp.exp(-x))


def _silu(x):
    return x * _sigmoid(x)


def _layer_norm_rows(y, g, b):
    mu = jnp.mean(y, axis=-1, keepdims=True)
    yc = y - mu
    var = jnp.mean(yc * yc, axis=-1, keepdims=True)
    return yc * lax.rsqrt(var + LN_EPS) * g + b


def _dot(a, b):
    return jnp.dot(a, b, preferred_element_type=F32)


def _dot_nt(a, b):
    return lax.dot_general(a, b, (((1,), (1,)), ((), ())), preferred_element_type=F32)


def _ada_kernel(c_ref, w_ref, b_ref, o_ref):
    c = c_ref[...]
    s = _silu(c).astype(BF16)
    o_ref[...] = _dot(s, w_ref[...].astype(BF16)) + b_ref[...]


def _ada_call(c_all, ada_w, ada_b):
    rows = c_all.shape[0]
    n = ada_w.shape[-1]
    tn = 1024
    blocks = _nbytes((D_MODEL, tn), F32) + _nbytes((rows, D_MODEL), F32) + 2 * _nbytes((rows, tn), F32)
    return pl.pallas_call(
        _ada_kernel,
        grid=(DEPTH, n // tn),
        in_specs=[
            pl.BlockSpec((rows, D_MODEL), lambda l, j: (0, 0)),
            pl.BlockSpec((None, D_MODEL, tn), lambda l, j: (l, 0, j)),
            pl.BlockSpec((None, 1, tn), lambda l, j: (l, 0, j)),
        ],
        out_specs=pl.BlockSpec((None, rows, tn), lambda l, j: (l, 0, j)),
        out_shape=jax.ShapeDtypeStruct((DEPTH, rows, n), F32),
        compiler_params=pltpu.CompilerParams(
            dimension_semantics=("parallel", "parallel"),
            vmem_limit_bytes=_vmem_limit(blocks, temp_bytes=_nbytes((D_MODEL, tn), BF16))),
        name="ada",
    )(c_all, ada_w, ada_b.reshape(DEPTH, 1, n))


def _inproj_kernel(x_ref, mod_ref, w_ref, wg_ref, o_ref, g_ref, h_ref):
    @pl.when(pl.program_id(1) == 0)
    def _():
        m = mod_ref[...]
        h = x_ref[...] * (1.0 + m[MOD_SC_M:MOD_SC_M + 1, :]) + m[MOD_SH_M:MOD_SH_M + 1, :]
        hb = h.astype(BF16)
        h_ref[...] = hb
        g_ref[...] = _dot(hb, wg_ref[...])

    o_ref[...] = _dot(h_ref[...], w_ref[...]).astype(o_ref.dtype)


def _inproj_call(x2, mod, w_main, w_gate, seq):
    m = x2.shape[0]
    tm, tn = 512, PM_WIDTH // 2
    blocks = (_nbytes((tm, D_MODEL), F32) + _nbytes((MOD_ROWS, D_MODEL), F32) + _nbytes((D_MODEL, tn), BF16)
              + _nbytes((D_MODEL, LANES), BF16) + _nbytes((tm, tn), BF16) + _nbytes((tm, LANES), F32))
    scratch = _nbytes((tm, D_MODEL), BF16)
    return pl.pallas_call(
        _inproj_kernel,
        grid=(m // tm, PM_WIDTH // tn),
        in_specs=[
            pl.BlockSpec((tm, D_MODEL), lambda i, j: (i, 0)),
            pl.BlockSpec((None, MOD_ROWS, D_MODEL), lambda i, j: ((i * tm) // seq, 0, 0)),
            pl.BlockSpec((D_MODEL, tn), lambda i, j: (0, j)),
            pl.BlockSpec((D_MODEL, LANES), lambda i, j: (0, 0)),
        ],
        out_specs=[
            pl.BlockSpec((tm, tn), lambda i, j: (i, j)),
            pl.BlockSpec((tm, LANES), lambda i, j: (i, 0)),
        ],
        out_shape=[
            jax.ShapeDtypeStruct((m, PM_WIDTH), BF16),
            jax.ShapeDtypeStruct((m, LANES), F32),
        ],
        scratch_shapes=[pltpu.VMEM((tm, D_MODEL), BF16)],
        compiler_params=pltpu.CompilerParams(
            dimension_semantics=("parallel", "arbitrary"),
            vmem_limit_bytes=_vmem_limit(blocks, scratch, _nbytes((tm, D_MODEL), F32))),
        name="inproj",
    )(x2, mod, w_main, w_gate)


def _shifted_rows(x, halo8, shift, n_rows):
    k = abs(shift)
    rolled = pltpu.roll(x, (-shift) % n_rows, 0)
    r8 = lax.broadcasted_iota(jnp.int32, (SUBLANES, x.shape[1]), 0)
    if shift < 0:
        fix = pltpu.roll(halo8, k, 0)
        head = jnp.where(r8 < k, fix, rolled[:SUBLANES])
        return jnp.concatenate([head, rolled[SUBLANES:]], axis=0)
    fix = pltpu.roll(halo8, SUBLANES - k, 0)
    tail = jnp.where(r8 >= SUBLANES - k, fix, rolled[n_rows - SUBLANES:])
    return jnp.concatenate([rolled[:n_rows - SUBLANES], tail], axis=0)


def _gdnprep_kernel(x_ref, xp_ref, xn_ref, gate_ref, cw_ref, gp_ref,
                    q_ref, k_ref, v_ref, gcol_ref, grow_ref, ext_ref, *, tt):
    i = pl.program_id(1)
    first = i == 0
    last = i == pl.num_programs(1) - 1
    cw = cw_ref[...]
    for part, out_ref in enumerate((q_ref, k_ref, v_ref)):
        cols = slice(part * GDN_WIDTH, (part + 1) * GDN_WIDTH)
        x = x_ref[:, cols].astype(F32)
        prev8 = xp_ref[:, cols].astype(F32)[BF16_ROWS - SUBLANES:]
        next8 = xn_ref[:, cols].astype(F32)[:SUBLANES]
        prev8 = jnp.where(first, 0.0, prev8)
        next8 = jnp.where(last, 0.0, next8)
        w = cw[:, cols]
        ext_ref[:SUBLANES, :] = prev8
        ext_ref[SUBLANES:SUBLANES + tt, :] = x
        ext_ref[SUBLANES + tt:, :] = next8
        acc = x * w[CONV_W // 2:CONV_W // 2 + 1, :]
        for tap in range(CONV_W):
            shift = tap - CONV_W // 2
            if shift != 0:
                acc = acc + ext_ref[SUBLANES + shift:SUBLANES + shift + tt, :] * w[tap:tap + 1, :]
        y = _silu(acc)
        if part < 2:
            heads = []
            for h in range(GDN_HEADS):
                yh = y[:, h * HEAD_DIM:(h + 1) * HEAD_DIM]
                inv = lax.rsqrt(jnp.sum(yh * yh, axis=-1, keepdims=True) + NORM_EPS)
                if part == 0:
                    inv = inv * (HEAD_DIM ** -0.5)
                heads.append(yh * inv)
            y = jnp.concatenate(heads, axis=1)
        out_ref[...] = y.astype(out_ref.dtype)

    raw = gate_ref[...]
    gp = gp_ref[...]
    col = lax.broadcasted_iota(jnp.int32, raw.shape, 1)
    beta = _sigmoid(raw)
    z = raw + gp[0:1, :]
    softplus = jnp.maximum(z, 0.0) + jnp.log(1.0 + jnp.exp(-jnp.abs(z)))
    logdec = -jnp.exp(gp[1:2, :]) * softplus
    is_dec = (col >= N_GATE_COLS) & (col < 2 * N_GATE_COLS)
    gsrc = jnp.where(is_dec, logdec, 0.0)
    r = lax.broadcasted_iota(jnp.int32, (tt, tt), 0)
    c = lax.broadcasted_iota(jnp.int32, (tt, tt), 1)
    same = (r // GDN_CHUNK) == (c // GDN_CHUNK)
    p_lo = jnp.where(same & (c <= r), 1.0, 0.0).astype(F32)
    p_up = jnp.where(same & (c >= r), 1.0, 0.0).astype(F32)
    cum_lo = jnp.dot(p_lo, gsrc, precision=lax.Precision.HIGHEST, preferred_element_type=F32)
    cum_up = jnp.dot(p_up, gsrc, precision=lax.Precision.HIGHEST, preferred_element_type=F32)
    total = cum_lo + cum_up - gsrc
    fwd_col = col < N_GATE_COLS + GDN_HEADS
    gc = jnp.where(fwd_col, cum_lo, cum_up)
    tot_shift = pltpu.roll(total, N_GATE_COLS, 1)
    gcol = jnp.where(col < N_GATE_COLS, beta,
                     jnp.where(col < 2 * N_GATE_COLS, gc,
                               jnp.where(col < 3 * N_GATE_COLS, tot_shift, 0.0)))
    gcol_ref[...] = gcol
    grow_ref[...] = gcol.T


def _gdnprep_call(pm, gates, conv_w8, gate_params):
    bsz, seq, _ = pm.shape
    tt = 256
    hb = tt // BF16_ROWS
    n_halo = seq // BF16_ROWS
    blocks = (_nbytes((tt, GDN_QKV), BF16) + 2 * _nbytes((BF16_ROWS, GDN_QKV), BF16) + _nbytes((tt, LANES), F32)
              + _nbytes((SUBLANES, GDN_QKV), F32) + 3 * _nbytes((tt, GDN_WIDTH), BF16) + 2 * _nbytes((tt, LANES), F32))
    temps = 8 * _nbytes((tt, GDN_WIDTH), F32) + 4 * _nbytes((tt, tt), F32)
    return pl.pallas_call(
        functools.partial(_gdnprep_kernel, tt=tt),
        grid=(bsz, seq // tt),
        in_specs=[
            pl.BlockSpec((None, tt, GDN_QKV), lambda b, i: (b, i, 0)),
            pl.BlockSpec((None, BF16_ROWS, GDN_QKV), lambda b, i: (b, jnp.maximum(i * hb - 1, 0), 0)),
            pl.BlockSpec((None, BF16_ROWS, GDN_QKV), lambda b, i: (b, jnp.minimum((i + 1) * hb, n_halo - 1), 0)),
            pl.BlockSpec((None, tt, LANES), lambda b, i: (b, i, 0)),
            pl.BlockSpec((SUBLANES, GDN_QKV), lambda b, i: (0, 0)),
            pl.BlockSpec((SUBLANES, LANES), lambda b, i: (0, 0)),
        ],
        out_specs=[
            pl.BlockSpec((None, tt, GDN_WIDTH), lambda b, i: (b, i, 0)),
            pl.BlockSpec((None, tt, GDN_WIDTH), lambda b, i: (b, i, 0)),
            pl.BlockSpec((None, tt, GDN_WIDTH), lambda b, i: (b, i, 0)),
            pl.BlockSpec((None, tt, LANES), lambda b, i: (b, i, 0)),
            pl.BlockSpec((None, LANES, tt), lambda b, i: (b, 0, i)),
        ],
        out_shape=[
            jax.ShapeDtypeStruct((bsz, seq, GDN_WIDTH), BF16),
            jax.ShapeDtypeStruct((bsz, seq, GDN_WIDTH), BF16),
            jax.ShapeDtypeStruct((bsz, seq, GDN_WIDTH), BF16),
            jax.ShapeDtypeStruct((bsz, seq, LANES), F32),
            jax.ShapeDtypeStruct((bsz, LANES, seq), F32),
        ],
        scratch_shapes=[pltpu.VMEM((tt + 2 * SUBLANES, GDN_WIDTH), F32)],
        compiler_params=pltpu.CompilerParams(
            dimension_semantics=("parallel", "parallel"),
            vmem_limit_bytes=_vmem_limit(blocks, _nbytes((tt + 2 * SUBLANES, GDN_WIDTH), F32), temps)),
        name="gdnprep",
    )(pm, pm, pm, gates, conv_w8, gate_params)


def _gdn_kernel(qf_ref, kf_ref, vf_ref, gcf_ref, grf_ref,
                qb_ref, kb_ref, vb_ref, gcb_ref, grb_ref,
                of_ref, ob_ref, state_ref):
    @pl.when(pl.program_id(1) == 0)
    def _():
        state_ref[...] = jnp.zeros_like(state_ref)

    n = GDN_CHUNK
    r = lax.broadcasted_iota(jnp.int32, (n, n), 0)
    c = lax.broadcasted_iota(jnp.int32, (n, n), 1)
    eye = jnp.where(r == c, 1.0, 0.0).astype(F32)

    units = []
    for d, (q_ref, k_ref, v_ref, gc_ref, gr_ref, o_ref) in enumerate((
            (qf_ref, kf_ref, vf_ref, gcf_ref, grf_ref, of_ref),
            (qb_ref, kb_ref, vb_ref, gcb_ref, grb_ref, ob_ref))):
        gcol = gc_ref[...]
        grow = gr_ref[...]
        reverse = d == 1
        incl = (r <= c) if reverse else (r >= c)
        strict = (r < c) if reverse else (r > c)
        for h in range(GDN_HEADS):
            lanes = slice(h * HEAD_DIM, (h + 1) * HEAD_DIM)
            j = d * GDN_HEADS + h
            beta = gcol[:, j:j + 1]
            gcc = gcol[:, N_GATE_COLS + j:N_GATE_COLS + j + 1]
            gtc = gcol[:, 2 * N_GATE_COLS + j:2 * N_GATE_COLS + j + 1]
            gcr = grow[N_GATE_COLS + j:N_GATE_COLS + j + 1, :]
            gtr = grow[2 * N_GATE_COLS + j:2 * N_GATE_COLS + j + 1, :]
            q = q_ref[:, lanes]
            k = k_ref[:, lanes]
            kf = k.astype(F32)
            kb = kf * beta
            egc = jnp.exp(gcc)
            units.append(dict(
                d=d, h=h, lanes=lanes, o_ref=o_ref, strict=strict, q=q, k=k,
                decay=jnp.where(incl, jnp.exp(jnp.where(incl, gcc - gcr, 0.0)), 0.0),
                kbb=kb.astype(BF16),
                vb=(v_ref[:, lanes].astype(F32) * beta).astype(BF16),
                kbg=(kb * egc).astype(BF16),
                qg=(q.astype(F32) * egc).astype(BF16),
                kdec_t=(kf * jnp.exp(gtc - gcc)).T.astype(BF16),
                sdec=jnp.exp(gtr)))

    for u in units:
        aq = _dot_nt(jnp.concatenate([u["kbb"], u["q"]], axis=0), u["k"])
        low = jnp.where(u["strict"], aq[:n] * u["decay"], 0.0)
        u["qk"] = (aq[n:] * u["decay"]).astype(BF16)
        u["tinv"] = eye - low
        u["lb"] = low.astype(BF16)

    for u in units:
        u["power"] = _dot(u["lb"], u["lb"])
    n_factors = (n - 1).bit_length() - 1
    for it in range(n_factors):
        for u in units:
            pb = u["power"].astype(BF16)
            if it + 1 < n_factors:
                both = _dot(jnp.concatenate([u["tinv"].astype(BF16), pb], axis=0), pb)
                u["tinv"] = u["tinv"] + both[:n]
                u["power"] = both[n:]
            else:
                u["tinv"] = u["tinv"] + _dot(u["tinv"].astype(BF16), pb)

    for u in units:
        u["uw"] = _dot(u["tinv"].astype(BF16), jnp.concatenate([u["vb"], u["kbg"]], axis=1))
    for u in units:
        u["state"] = state_ref[u["d"], u["h"]]
        w = u["uw"][:, HEAD_DIM:].astype(BF16)
        u["ws"] = _dot(jnp.concatenate([w, u["qg"]], axis=0), u["state"].astype(BF16))
    for u in units:
        v_new = (u["uw"][:, :HEAD_DIM] - u["ws"][:n]).astype(BF16)
        u["os"] = _dot(jnp.concatenate([u["qk"], u["kdec_t"]], axis=0), v_new)
    for u in units:
        u["o_ref"][:, u["lanes"]] = u["ws"][n:] + u["os"][:n]
        state_ref[u["d"], u["h"]] = u["state"] * u["sdec"] + u["os"][n:]


def _gdn_call(q, k, v, gcol, grow):
    bsz, seq, _ = q.shape
    tb = GDN_CHUNK
    nb = seq // tb
    qkv_spec_f = pl.BlockSpec((None, tb, GDN_WIDTH), lambda b, i: (b, i, 0))
    qkv_spec_b = pl.BlockSpec((None, tb, GDN_WIDTH), lambda b, i: (b, nb - 1 - i, 0))
    gc_spec_f = pl.BlockSpec((None, tb, LANES), lambda b, i: (b, i, 0))
    gc_spec_b = pl.BlockSpec((None, tb, LANES), lambda b, i: (b, nb - 1 - i, 0))
    gr_spec_f = pl.BlockSpec((None, LANES, tb), lambda b, i: (b, 0, i))
    gr_spec_b = pl.BlockSpec((None, LANES, tb), lambda b, i: (b, 0, nb - 1 - i))
    blocks = 2 * (3 * _nbytes((tb, GDN_WIDTH), BF16) + 2 * _nbytes((tb, LANES), F32) + _nbytes((tb, GDN_WIDTH), F32))
    scratch = _nbytes((2, GDN_HEADS, HEAD_DIM, HEAD_DIM), F32)
    temps = 2 * GDN_HEADS * 24 * _nbytes((GDN_CHUNK, LANES), F32)
    return pl.pallas_call(
        _gdn_kernel,
        grid=(bsz, nb),
        in_specs=[qkv_spec_f, qkv_spec_f, qkv_spec_f, gc_spec_f, gr_spec_f,
                  qkv_spec_b, qkv_spec_b, qkv_spec_b, gc_spec_b, gr_spec_b],
        out_specs=[qkv_spec_f, qkv_spec_b],
        out_shape=[jax.ShapeDtypeStruct((bsz, seq, GDN_WIDTH), F32),
                   jax.ShapeDtypeStruct((bsz, seq, GDN_WIDTH), F32)],
        scratch_shapes=[pltpu.VMEM((2, GDN_HEADS, HEAD_DIM, HEAD_DIM), F32)],
        compiler_params=pltpu.CompilerParams(
            dimension_semantics=("parallel", "arbitrary"),
            vmem_limit_bytes=_vmem_limit(blocks, scratch, temps)),
        name="gdn",
    )(q, k, v, gcol, grow, q, k, v, gcol, grow)


def _rope_tables(seq):
    t = jnp.arange(seq)
    row = (t // GRID_W).astype(F32)
    col = (t % GRID_W).astype(F32)
    half = HEAD_DIM // 2
    inv_freq = ROPE_THETA ** (-jnp.arange(0, half, 2, dtype=F32) / half)
    ang_r = row[:, None] * inv_freq
    ang_c = col[:, None] * inv_freq
    ang = jnp.concatenate([ang_r, ang_r, ang_c, ang_c], -1)
    cos, sin = jnp.cos(ang), jnp.sin(ang)
    lane = jnp.arange(HEAD_DIM)
    first = (lane // (HEAD_DIM // 4)) % 2 == 0
    sin_up = jnp.where(first, -sin, 0.0)
    sin_dn = jnp.where(first, 0.0, sin)
    return cos, sin_up, sin_dn


def _norm_rope(x, w, cos, sin_up, sin_dn, scale):
    xn = x * lax.rsqrt(jnp.mean(x * x, axis=-1, keepdims=True) + NORM_EPS) * w
    quarter = HEAD_DIM // 4
    up = pltpu.roll(xn, HEAD_DIM - quarter, 1)
    dn = pltpu.roll(xn, quarter, 1)
    y = xn * cos + up * sin_up + dn * sin_dn
    return y * scale if scale != 1.0 else y


ATT_TK = 1024
FLASH_SUB = 256
LOG2_E = 1.4426950408889634


def _attprep_kernel(aq_ref, ak_ref, av_ref, cos_ref, su_ref, sd_ref, qw_ref, kw_ref, qt_ref, k_ref, vt_ref):
    cos, su, sd = cos_ref[...], su_ref[...], sd_ref[...]
    qw, kw = qw_ref[...], kw_ref[...]
    for h in range(ATT_HEADS):
        lanes = slice(h * HEAD_DIM, (h + 1) * HEAD_DIM)
        q = _norm_rope(aq_ref[:, lanes].astype(F32), qw, cos, su, sd, HEAD_DIM ** -0.5 * LOG2_E)
        qt_ref[lanes, :] = q.T.astype(qt_ref.dtype)
    for h in range(ATT_KV_HEADS):
        lanes = slice(h * HEAD_DIM, (h + 1) * HEAD_DIM)
        k_ref[:, lanes] = _norm_rope(ak_ref[:, lanes].astype(F32), kw, cos, su, sd, 1.0).astype(k_ref.dtype)
        vt_ref[h] = av_ref[:, lanes].astype(F32).T.astype(vt_ref.dtype)


def _attprep_call(pm, cos, sin_up, sin_dn, q_norm_w, k_norm_w):
    bsz, seq, _ = pm.shape
    tt = ATT_TK
    tab = pl.BlockSpec((tt, HEAD_DIM), lambda b, i: (i, 0))
    vec = pl.BlockSpec((1, HEAD_DIM), lambda b, i: (0, 0))
    blocks = (2 * _nbytes((tt, ATT_WIDTH), BF16) + 4 * _nbytes((tt, ATT_KV_WIDTH), BF16)
              + 3 * _nbytes((tt, HEAD_DIM), F32))
    return pl.pallas_call(
        _attprep_kernel,
        grid=(bsz, seq // tt),
        in_specs=[
            pl.BlockSpec((None, tt, ATT_WIDTH), lambda b, i: (b, i, PM_AQ // ATT_WIDTH)),
            pl.BlockSpec((None, tt, ATT_KV_WIDTH), lambda b, i: (b, i, PM_AK // ATT_KV_WIDTH)),
            pl.BlockSpec((None, tt, ATT_KV_WIDTH), lambda b, i: (b, i, PM_AV // ATT_KV_WIDTH)),
            tab, tab, tab, vec, vec,
        ],
        out_specs=[
            pl.BlockSpec((None, ATT_WIDTH, tt), lambda b, i: (b, 0, i)),
            pl.BlockSpec((None, tt, ATT_KV_WIDTH), lambda b, i: (b, i, 0)),
            pl.BlockSpec((None, ATT_KV_HEADS, None, HEAD_DIM, tt), lambda b, i: (b, 0, i, 0, 0)),
        ],
        out_shape=[jax.ShapeDtypeStruct((bsz, ATT_WIDTH, seq), BF16),
                   jax.ShapeDtypeStruct((bsz, seq, ATT_KV_WIDTH), BF16),
                   jax.ShapeDtypeStruct((bsz, ATT_KV_HEADS, seq // tt, HEAD_DIM, tt), BF16)],
        compiler_params=pltpu.CompilerParams(
            dimension_semantics=("parallel", "parallel"),
            vmem_limit_bytes=_vmem_limit(blocks, temp_bytes=8 * _nbytes((tt, HEAD_DIM), F32))),
        name="attprep",
    )(pm, pm, pm, cos, sin_up, sin_dn, q_norm_w.reshape(1, HEAD_DIM), k_norm_w.reshape(1, HEAD_DIM))


def _flash_kernel(qt_ref, qtn_ref, k_ref, vt_ref, o_ref, m_ref, l_ref, acc_ref, qc_ref, qn_ref,
                  sa_ref, sb_ref, mca_ref, mcb_ref, *, tq, tk, ts, n_k):
    m_ref[...] = jnp.full_like(m_ref, -jnp.inf)
    l_ref[...] = jnp.zeros_like(l_ref)
    acc_ref[...] = jnp.zeros_like(acc_ref)
    for g in range(ATT_GROUP):
        qc_ref[:, g * tq:(g + 1) * tq] = qt_ref[g * HEAD_DIM:(g + 1) * HEAD_DIM, :]
        qn_ref[:, g * tq:(g + 1) * tq] = qtn_ref[g * HEAD_DIM:(g + 1) * HEAD_DIM, :]

    n_sub = tk // ts
    bufs = ((sa_ref, mca_ref), (sb_ref, mcb_ref))

    def scores(j, q_ref, s_ref, i):
        start = pl.multiple_of(j * tk + i * ts, ts)
        st = _dot(k_ref[pl.ds(start, ts), :], q_ref[...])
        s_ref[i * ts:(i + 1) * ts, :] = st
        return jnp.max(st, axis=0, keepdims=True)

    def weigh(j, s_ref, i, m_new):
        p = jnp.exp2(s_ref[i * ts:(i + 1) * ts, :] - m_new)
        return jnp.sum(p, axis=0, keepdims=True), _dot(vt_ref[j, :, i * ts:(i + 1) * ts], p.astype(BF16))

    def step(j, parity, j_next, q_next):
        s_cur, mc_cur = bufs[parity]
        s_nxt, mc_nxt = bufs[1 - parity]
        m_prev = m_ref[...]
        m_new = jnp.maximum(m_prev, mc_cur[...])
        alpha = jnp.exp2(m_prev - m_new)
        mc_next, l_new, pv = None, None, None
        for i in range(n_sub):
            mx = scores(j_next, q_next, s_nxt, i)
            mc_next = mx if mc_next is None else jnp.maximum(mc_next, mx)
            ls, pvi = weigh(j, s_cur, i, m_new)
            l_new = ls if l_new is None else l_new + ls
            pv = pvi if pv is None else pv + pvi
        mc_nxt[...] = mc_next
        l_ref[...] = alpha * l_ref[...] + l_new
        acc_ref[...] = alpha * acc_ref[...] + pv
        m_ref[...] = m_new

    @pl.when(pl.program_id(2) == 0)
    def _():
        mc0 = None
        for i in range(n_sub):
            mx = scores(0, qc_ref, sa_ref, i)
            mc0 = mx if mc0 is None else jnp.maximum(mc0, mx)
        mca_ref[...] = mc0

    def body(jj, carry):
        step(2 * jj, 0, 2 * jj + 1, qc_ref)
        step(2 * jj + 1, 1, 2 * jj + 2, qc_ref)
        return carry

    lax.fori_loop(0, n_k // 2 - 1, body, 0)
    step(n_k - 2, 0, n_k - 1, qc_ref)
    step(n_k - 1, 1, 0, qn_ref)
    for g in range(ATT_GROUP):
        lanes = slice(g * tq, (g + 1) * tq)
        out_t = acc_ref[:, lanes] / l_ref[:, lanes]
        o_ref[:, g * HEAD_DIM:(g + 1) * HEAD_DIM] = out_t.T.astype(o_ref.dtype)


def _flash_call(qt, kr, vt):
    bsz, _, seq = qt.shape
    tq, tk = 256, ATT_TK
    gw = ATT_GROUP * HEAD_DIM
    nq = ATT_GROUP * tq
    n_q, n_k = seq // tq, seq // tk
    assert n_k % 2 == 0, "the score buffers alternate by key-tile parity across query tiles"
    blocks = 3 * _nbytes((gw, tq), BF16) + 2 * _nbytes((seq, HEAD_DIM), BF16)
    scratch = (4 * _nbytes((SUBLANES, nq), F32) + _nbytes((HEAD_DIM, nq), F32) + 2 * _nbytes((HEAD_DIM, nq), BF16)
               + 2 * _nbytes((tk, nq), F32))
    temps = 2 * _nbytes((tk, nq), F32)
    return pl.pallas_call(
        functools.partial(_flash_kernel, tq=tq, tk=tk, ts=FLASH_SUB, n_k=n_k),
        grid=(bsz, ATT_KV_HEADS, n_q),
        in_specs=[
            pl.BlockSpec((None, gw, tq), lambda b, h, i: (b, h, i)),
            pl.BlockSpec((None, gw, tq), lambda b, h, i: (b, h, jnp.minimum(i + 1, n_q - 1))),
            pl.BlockSpec((None, seq, HEAD_DIM), lambda b, h, i: (b, 0, h)),
            pl.BlockSpec((None, None, n_k, HEAD_DIM, tk), lambda b, h, i: (b, h, 0, 0, 0)),
        ],
        out_specs=pl.BlockSpec((None, tq, gw), lambda b, h, i: (b, i, h)),
        out_shape=jax.ShapeDtypeStruct((bsz, seq, ATT_WIDTH), BF16),
        scratch_shapes=[pltpu.VMEM((1, nq), F32), pltpu.VMEM((1, nq), F32),
                        pltpu.VMEM((HEAD_DIM, nq), F32), pltpu.VMEM((HEAD_DIM, nq), BF16),
                        pltpu.VMEM((HEAD_DIM, nq), BF16),
                        pltpu.VMEM((tk, nq), F32), pltpu.VMEM((tk, nq), F32),
                        pltpu.VMEM((1, nq), F32), pltpu.VMEM((1, nq), F32)],
        compiler_params=pltpu.CompilerParams(
            dimension_semantics=("parallel", "parallel", "arbitrary"),
            vmem_limit_bytes=_vmem_limit(blocks, scratch, temps)),
        name="flash",
    )(qt, qt, kr, vt)


def _outproj_kernel(of_ref, ob_ref, z_ref, oa_ref, x_ref, mod_ref, w_ref, nw_ref, lg_ref, lb_ref, o_ref):
    nw = nw_ref[...]
    heads = []
    for h in range(GDN_HEADS):
        lanes = slice(h * HEAD_DIM, (h + 1) * HEAD_DIM)
        o = of_ref[:, lanes] + ob_ref[:, lanes]
        on = o * lax.rsqrt(jnp.mean(o * o, axis=-1, keepdims=True) + NORM_EPS) * nw
        heads.append((on * _silu(z_ref[:, lanes].astype(F32))).astype(BF16))
    og = jnp.concatenate(heads, axis=1)
    mix = _dot(og, w_ref[:GDN_WIDTH, :]) + _dot(oa_ref[...], w_ref[GDN_WIDTH:, :])
    m = mod_ref[...]
    y = DEEPNORM_ALPHA * x_ref[...] + m[MOD_G_M:MOD_G_M + 1, :] * mix
    o_ref[...] = _layer_norm_rows(y, lg_ref[...], lb_ref[...])


def _outproj_call(o_f, o_b, pm2, o_att, x2, mod, w_out, gdn_norm_w, ln_g, ln_b, seq):
    m = x2.shape[0]
    tm = 256
    blocks = (2 * _nbytes((tm, GDN_WIDTH), F32) + 2 * _nbytes((tm, GDN_WIDTH), BF16)
              + 2 * _nbytes((tm, D_MODEL), F32) + _nbytes((MOD_ROWS, D_MODEL), F32)
              + _nbytes((D_MODEL, D_MODEL), BF16))
    vec = pl.BlockSpec((1, D_MODEL), lambda i: (0, 0))
    return pl.pallas_call(
        _outproj_kernel,
        grid=(m // tm,),
        in_specs=[
            pl.BlockSpec((tm, GDN_WIDTH), lambda i: (i, 0)),
            pl.BlockSpec((tm, GDN_WIDTH), lambda i: (i, 0)),
            pl.BlockSpec((tm, GDN_WIDTH), lambda i: (i, PM_Z // GDN_WIDTH)),
            pl.BlockSpec((tm, ATT_WIDTH), lambda i: (i, 0)),
            pl.BlockSpec((tm, D_MODEL), lambda i: (i, 0)),
            pl.BlockSpec((None, MOD_ROWS, D_MODEL), lambda i: ((i * tm) // seq, 0, 0)),
            pl.BlockSpec((D_MODEL, D_MODEL), lambda i: (0, 0)),
            pl.BlockSpec((1, HEAD_DIM), lambda i: (0, 0)),
            vec, vec,
        ],
        out_specs=pl.BlockSpec((tm, D_MODEL), lambda i: (i, 0)),
        out_shape=jax.ShapeDtypeStruct((m, D_MODEL), F32),
        compiler_params=pltpu.CompilerParams(
            dimension_semantics=("parallel",),
            vmem_limit_bytes=_vmem_limit(blocks, temp_bytes=4 * _nbytes((tm, D_MODEL), F32))),
        name="outproj",
    )(o_f, o_b, pm2, o_att, x2, mod, w_out, gdn_norm_w.reshape(1, HEAD_DIM),
      ln_g.reshape(1, D_MODEL), ln_b.reshape(1, D_MODEL))


POOL_REACH = SUBLANES
assert max(POOL_WINDOWS) // 2 <= POOL_REACH


def _pool_kernel(x_ref, xp_ref, xn_ref, mod_ref, pw_ref, ps_ref, lg_ref, lb_ref, o_ref, *, tt, seq):
    t0 = pl.program_id(1) * tt
    m = mod_ref[...]
    sc = 1.0 + m[MOD_SC_M:MOD_SC_M + 1, :]
    sh = m[MOD_SH_M:MOD_SH_M + 1, :]
    x = x_ref[...]
    h_main = x * sc + sh
    h_prev = xp_ref[...] * sc + sh
    h_next = xn_ref[...] * sc + sh
    r = lax.broadcasted_iota(jnp.int32, (tt, tt), 0)
    c = lax.broadcasted_iota(jnp.int32, (tt, tt), 1)
    pos = t0 + lax.broadcasted_iota(jnp.int32, (tt, 1), 0)
    r8 = lax.broadcasted_iota(jnp.int32, (POOL_REACH, 1), 0)
    outs = []
    for gi, win in enumerate(POOL_WINDOWS):
        back, fwd = win // 2, win - 1 - win // 2
        lanes = slice(gi * POOL_GROUP, (gi + 1) * POOL_GROUP)
        band = jnp.where((c >= r - back) & (c <= r + fwd), 1.0, 0.0).astype(BF16)
        hm = h_main[:, lanes]
        wsum = _dot(band, hm.astype(BF16))
        top = jnp.zeros((POOL_REACH, POOL_GROUP), F32)
        bot = jnp.zeros((POOL_REACH, POOL_GROUP), F32)
        for k in range(POOL_REACH):
            if POOL_REACH - k <= back:
                use = (r8 <= k - POOL_REACH + back) & (t0 - POOL_REACH + k >= 0)
                top = top + jnp.where(use, 1.0, 0.0) * h_prev[k:k + 1, lanes]
            if k < fwd:
                use = (r8 >= k + POOL_REACH - fwd) & (t0 + tt + k < seq)
                bot = bot + jnp.where(use, 1.0, 0.0) * h_next[k:k + 1, lanes]
        wsum = jnp.concatenate([wsum[:POOL_REACH] + top, wsum[POOL_REACH:tt - POOL_REACH],
                                wsum[tt - POOL_REACH:] + bot], axis=0)
        cnt = (jnp.minimum(pos + fwd, seq - 1) - jnp.maximum(pos - back, 0) + 1).astype(F32)
        diff = wsum / cnt - hm
        outs.append(_dot(diff.astype(BF16), pw_ref[gi]))
    mix = jnp.concatenate(outs, axis=1) * ps_ref[...]
    y = DEEPNORM_ALPHA * x + m[MOD_G_M:MOD_G_M + 1, :] * mix
    o_ref[...] = _layer_norm_rows(y, lg_ref[...], lb_ref[...])


def _pool_call(x, mod, pool_w, pool_scale, ln_g, ln_b):
    bsz, seq, _ = x.shape
    tt = 256
    hb = tt // POOL_REACH
    n_halo = seq // POOL_REACH
    vec = pl.BlockSpec((1, D_MODEL), lambda b, i: (0, 0))
    blocks = (2 * _nbytes((tt, D_MODEL), F32) + 2 * _nbytes((POOL_REACH, D_MODEL), F32)
              + _nbytes((MOD_ROWS, D_MODEL), F32) + _nbytes(pool_w.shape, BF16))
    temps = 5 * _nbytes((tt, D_MODEL), F32)
    return pl.pallas_call(
        functools.partial(_pool_kernel, tt=tt, seq=seq),
        grid=(bsz, seq // tt),
        in_specs=[
            pl.BlockSpec((None, tt, D_MODEL), lambda b, i: (b, i, 0)),
            pl.BlockSpec((None, POOL_REACH, D_MODEL), lambda b, i: (b, jnp.maximum(i * hb - 1, 0), 0)),
            pl.BlockSpec((None, POOL_REACH, D_MODEL), lambda b, i: (b, jnp.minimum((i + 1) * hb, n_halo - 1), 0)),
            pl.BlockSpec((None, MOD_ROWS, D_MODEL), lambda b, i: (b, 0, 0)),
            pl.BlockSpec(pool_w.shape, lambda b, i: (0, 0, 0)),
            vec, vec, vec,
        ],
        out_specs=pl.BlockSpec((None, tt, D_MODEL), lambda b, i: (b, i, 0)),
        out_shape=jax.ShapeDtypeStruct((bsz, seq, D_MODEL), F32),
        compiler_params=pltpu.CompilerParams(
            dimension_semantics=("parallel", "parallel"),
            vmem_limit_bytes=_vmem_limit(blocks, temp_bytes=temps)),
        name="pool",
    )(x, x, x, mod, pool_w, pool_scale.reshape(1, D_MODEL), ln_g.reshape(1, D_MODEL), ln_b.reshape(1, D_MODEL))


def _mlp_kernel(x_ref, mod_ref, w1_ref, w2_ref, lg_ref, lb_ref, o_ref, h_ref, acc_ref):
    j = pl.program_id(1)

    @pl.when(j == 0)
    def _():
        m = mod_ref[...]
        h = x_ref[...] * (1.0 + m[MOD_SC_F:MOD_SC_F + 1, :]) + m[MOD_SH_F:MOD_SH_F + 1, :]
        h_ref[...] = h.astype(BF16)
        acc_ref[...] = jnp.zeros_like(acc_ref)

    u = jnp.maximum(_dot(h_ref[...], w1_ref[...]), 0.0)
    acc_ref[...] += _dot((u * u).astype(BF16), w2_ref[...])

    @pl.when(j == pl.num_programs(1) - 1)
    def _():
        m = mod_ref[...]
        y = DEEPNORM_ALPHA * x_ref[...] + m[MOD_G_F:MOD_G_F + 1, :] * acc_ref[...]
        o_ref[...] = _layer_norm_rows(y, lg_ref[...], lb_ref[...])


def _mlp_call(x2, mod, w1, w2, ln_g, ln_b, seq):
    m = x2.shape[0]
    tm, tf = 512, 1024
    vec = pl.BlockSpec((1, D_MODEL), lambda i, j: (0, 0))
    blocks = (2 * _nbytes((tm, D_MODEL), F32) + _nbytes((MOD_ROWS, D_MODEL), F32)
              + 2 * _nbytes((D_MODEL, tf), BF16))
    scratch = _nbytes((tm, D_MODEL), BF16) + _nbytes((tm, D_MODEL), F32)
    temps = 2 * _nbytes((tm, tf), F32) + _nbytes((tm, D_MODEL), F32)
    return pl.pallas_call(
        _mlp_kernel,
        grid=(m // tm, D_FF // tf),
        in_specs=[
            pl.BlockSpec((tm, D_MODEL), lambda i, j: (i, 0)),
            pl.BlockSpec((None, MOD_ROWS, D_MODEL), lambda i, j: ((i * tm) // seq, 0, 0)),
            pl.BlockSpec((D_MODEL, tf), lambda i, j: (0, j)),
            pl.BlockSpec((tf, D_MODEL), lambda i, j: (j, 0)),
            vec, vec,
        ],
        out_specs=pl.BlockSpec((tm, D_MODEL), lambda i, j: (i, 0)),
        out_shape=jax.ShapeDtypeStruct((m, D_MODEL), F32),
        scratch_shapes=[pltpu.VMEM((tm, D_MODEL), BF16), pltpu.VMEM((tm, D_MODEL), F32)],
        compiler_params=pltpu.CompilerParams(
            dimension_semantics=("parallel", "arbitrary"),
            vmem_limit_bytes=_vmem_limit(blocks, scratch, temps)),
        name="mlp",
    )(x2, mod, w1, w2, ln_g.reshape(1, D_MODEL), ln_b.reshape(1, D_MODEL))


def _mod_table(cond_rows):
    bsz = cond_rows.shape[0]
    t = cond_rows.reshape(bsz, 6, D_MODEL)
    return jnp.pad(t, ((0, 0), (0, MOD_ROWS - 6), (0, 0)))


def _pack_weights(w_in, conv_w, a_log, dt_bias, w_out, pool_w, mlp_w1, mlp_w2):
    wi = w_in[0]
    w_main = jnp.concatenate(
        [wi[:, :OFF_BETA], wi[:, OFF_AQ:]], axis=1).astype(BF16)
    w_gate = jnp.pad(wi[:, OFF_BETA:OFF_AQ], ((0, 0), (0, LANES - 2 * N_GATE_COLS))).astype(BF16)
    conv_w8 = jnp.pad(conv_w[0], ((0, SUBLANES - CONV_W), (0, 0)))
    gate_params = jnp.zeros((SUBLANES, LANES), F32)
    gate_params = gate_params.at[0, N_GATE_COLS:2 * N_GATE_COLS].set(dt_bias[0].reshape(-1))
    gate_params = gate_params.at[1, N_GATE_COLS:2 * N_GATE_COLS].set(a_log[0].reshape(-1))
    return dict(w_main=w_main, w_gate=w_gate, conv_w8=conv_w8, gate_params=gate_params,
                w_out=w_out[0].astype(BF16), pool_w=pool_w[0].astype(BF16),
                w1=mlp_w1.astype(BF16), w2=mlp_w2.astype(BF16))


def _trunk(x, cond, pk, gdn_norm_w, q_norm_w, k_norm_w, pool_scale, ln_g, ln_b):
    bsz, seq, _ = x.shape
    m = bsz * seq
    mod0 = _mod_table(cond[0])
    mod1 = _mod_table(cond[1])
    x2 = x.reshape(m, D_MODEL)

    pm2, gates2 = _inproj_call(x2, mod0, pk["w_main"], pk["w_gate"], seq)
    pm = pm2.reshape(bsz, seq, PM_WIDTH)
    gq, gk, gv, gcol, grow = _gdnprep_call(pm, gates2.reshape(bsz, seq, LANES), pk["conv_w8"], pk["gate_params"])
    o_f, o_b = _gdn_call(gq, gk, gv, gcol, grow)
    cos, sin_up, sin_dn = _rope_tables(seq)
    qt, kr, vt = _attprep_call(pm, cos, sin_up, sin_dn, q_norm_w[0], k_norm_w[0])
    o_att = _flash_call(qt, kr, vt)
    x2 = _outproj_call(o_f.reshape(m, GDN_WIDTH), o_b.reshape(m, GDN_WIDTH), pm2, o_att.reshape(m, ATT_WIDTH),
                       x2, mod0, pk["w_out"], gdn_norm_w[0], ln_g[0, 0], ln_b[0, 0], seq)
    x2 = _mlp_call(x2, mod0, pk["w1"][0], pk["w2"][0], ln_g[0, 1], ln_b[0, 1], seq)

    x3 = _pool_call(x2.reshape(bsz, seq, D_MODEL), mod1, pk["pool_w"], pool_scale[0], ln_g[1, 0], ln_b[1, 0])
    x2 = _mlp_call(x3.reshape(m, D_MODEL), mod1, pk["w1"][1], pk["w2"][1], ln_g[1, 1], ln_b[1, 1], seq)
    return x2.reshape(bsz, seq, D_MODEL)


def kernel(x_prompt, x_sample, c_prompt, c_sample, w_in, conv_w, a_log, dt_bias, gdn_norm_w, q_norm_w,
           k_norm_w, w_out, pool_w, pool_scale, mlp_w1, mlp_w2, ada_w, ada_b, ln_g, ln_b):
    bp, bs = c_prompt.shape[0], c_sample.shape[0]
    c_all = jnp.concatenate([c_prompt, c_sample], axis=0)
    c_all = jnp.pad(c_all, ((0, (-c_all.shape[0]) % SUBLANES), (0, 0)))
    cond = _ada_call(c_all, ada_w, ada_b)
    pk = _pack_weights(w_in, conv_w, a_log, dt_bias, w_out, pool_w, mlp_w1, mlp_w2)
    y_prompt = _trunk(x_prompt, cond[:, :bp], pk, gdn_norm_w, q_norm_w, k_norm_w, pool_scale, ln_g, ln_b)
    y_sample = _trunk(x_sample, cond[:, bp:bp + bs], pk, gdn_norm_w, q_norm_w, k_norm_w, pool_scale, ln_g, ln_b)
    return (y_prompt, y_sample)
```

```python
import functools

import jax
import jax.numpy as jnp
from jax import lax
from jax.experimental import pallas as pl
from jax.experimental.pallas import tpu as pltpu

F32 = jnp.float32
BF16 = jnp.bfloat16

D_MODEL = 2048
DEPTH = 2
GRID_W = 64
HEAD_DIM = 128
GDN_HEADS = 8
ATT_HEADS = 8
ATT_KV_HEADS = 2
ATT_GROUP = ATT_HEADS // ATT_KV_HEADS
GDN_WIDTH = GDN_HEADS * HEAD_DIM
ATT_WIDTH = ATT_HEADS * HEAD_DIM
ATT_KV_WIDTH = ATT_KV_HEADS * HEAD_DIM
CONV_W = 5
ROPE_THETA = 10000.0
POOL_WINDOWS = (2, 4, 8, 16)
N_POOL_GROUPS = 4
POOL_GROUP = D_MODEL // N_POOL_GROUPS
D_FF = 4 * D_MODEL
DEEPNORM_ALPHA = (2 * DEPTH) ** 0.25
NORM_EPS = 1e-6
LN_EPS = 1e-5

GDN_QKV = 3 * GDN_WIDTH
OFF_Z = GDN_QKV
OFF_BETA = OFF_Z + GDN_WIDTH
OFF_A = OFF_BETA + 2 * GDN_HEADS
OFF_AQ = OFF_A + 2 * GDN_HEADS
OFF_AK = OFF_AQ + ATT_WIDTH
OFF_AV = OFF_AK + ATT_KV_WIDTH
D_IN = OFF_AV + ATT_KV_WIDTH

PM_QKV = 0
PM_Z = GDN_QKV
PM_AQ = PM_Z + GDN_WIDTH
PM_AK = PM_AQ + ATT_WIDTH
PM_AV = PM_AK + ATT_KV_WIDTH
PM_WIDTH = PM_AV + ATT_KV_WIDTH

LANES = 128
SUBLANES = 8
BF16_ROWS = 16
V7X_VMEM_BYTES = 64 * 1024 * 1024

GDN_CHUNK = 128
N_GATE_COLS = 2 * GDN_HEADS

MOD_SH_M, MOD_SC_M, MOD_G_M, MOD_SH_F, MOD_SC_F, MOD_G_F = range(6)
MOD_ROWS = 8


def _vmem_limit(block_bytes, scratch_bytes=0, temp_bytes=0):
    need = 2 * block_bytes + scratch_bytes + temp_bytes
    return int(min(need + need // 4, V7X_VMEM_BYTES - 8 * 1024 * 1024))


def _nbytes(shape, dtype):
    n = 1
    for s in shape:
        n *= s
    return n * jnp.dtype(dtype).itemsize


def _sigmoid(x):
    return 1.0 / (1.0 + jnp.exp(-x))


def _silu(x):
    return x * _sigmoid(x)


def _layer_norm_rows(y, g, b):
    mu = jnp.mean(y, axis=-1, keepdims=True)
    yc = y - mu
    var = jnp.mean(yc * yc, axis=-1, keepdims=True)
    return yc * lax.rsqrt(var + LN_EPS) * g + b


def _dot(a, b):
    return jnp.dot(a, b, preferred_element_type=F32)


def _dot_nt(a, b):
    return lax.dot_general(a, b, (((1,), (1,)), ((), ())), preferred_element_type=F32)


def _ada_kernel(c_ref, w_ref, b_ref, o_ref):
    c = c_ref[...]
    s = _silu(c).astype(BF16)
    o_ref[...] = _dot(s, w_ref[...].astype(BF16)) + b_ref[...]


def _ada_call(c_all, ada_w, ada_b):
    rows = c_all.shape[0]
    n = ada_w.shape[-1]
    tn = 1024
    blocks = _nbytes((D_MODEL, tn), F32) + _nbytes((rows, D_MODEL), F32) + 2 * _nbytes((rows, tn), F32)
    return pl.pallas_call(
        _ada_kernel,
        grid=(DEPTH, n // tn),
        in_specs=[
            pl.BlockSpec((rows, D_MODEL), lambda l, j: (0, 0)),
            pl.BlockSpec((None, D_MODEL, tn), lambda l, j: (l, 0, j)),
            pl.BlockSpec((None, 1, tn), lambda l, j: (l, 0, j)),
        ],
        out_specs=pl.BlockSpec((None, rows, tn), lambda l, j: (l, 0, j)),
        out_shape=jax.ShapeDtypeStruct((DEPTH, rows, n), F32),
        compiler_params=pltpu.CompilerParams(
            dimension_semantics=("parallel", "parallel"),
            vmem_limit_bytes=_vmem_limit(blocks, temp_bytes=_nbytes((D_MODEL, tn), BF16))),
        name="ada",
    )(c_all, ada_w, ada_b.reshape(DEPTH, 1, n))


def _inproj_kernel(x_ref, mod_ref, w_ref, wg_ref, o_ref, g_ref, h_ref):
    @pl.when(pl.program_id(1) == 0)
    def _():
        m = mod_ref[...]
        h = x_ref[...] * (1.0 + m[MOD_SC_M:MOD_SC_M + 1, :]) + m[MOD_SH_M:MOD_SH_M + 1, :]
        hb = h.astype(BF16)
        h_ref[...] = hb
        g_ref[...] = _dot(hb, wg_ref[...])

    o_ref[...] = _dot(h_ref[...], w_ref[...]).astype(o_ref.dtype)


def _inproj_call(x2, mod, w_main, w_gate, seq):
    m = x2.shape[0]
    tm, tn = 512, PM_WIDTH // 2
    blocks = (_nbytes((tm, D_MODEL), F32) + _nbytes((MOD_ROWS, D_MODEL), F32) + _nbytes((D_MODEL, tn), BF16)
              + _nbytes((D_MODEL, LANES), BF16) + _nbytes((tm, tn), BF16) + _nbytes((tm, LANES), F32))
    scratch = _nbytes((tm, D_MODEL), BF16)
    return pl.pallas_call(
        _inproj_kernel,
        grid=(m // tm, PM_WIDTH // tn),
        in_specs=[
            pl.BlockSpec((tm, D_MODEL), lambda i, j: (i, 0)),
            pl.BlockSpec((None, MOD_ROWS, D_MODEL), lambda i, j: ((i * tm) // seq, 0, 0)),
            pl.BlockSpec((D_MODEL, tn), lambda i, j: (0, j)),
            pl.BlockSpec((D_MODEL, LANES), lambda i, j: (0, 0)),
        ],
        out_specs=[
            pl.BlockSpec((tm, tn), lambda i, j: (i, j)),
            pl.BlockSpec((tm, LANES), lambda i, j: (i, 0)),
        ],
        out_shape=[
            jax.ShapeDtypeStruct((m, PM_WIDTH), BF16),
            jax.ShapeDtypeStruct((m, LANES), F32),
        ],
        scratch_shapes=[pltpu.VMEM((tm, D_MODEL), BF16)],
        compiler_params=pltpu.CompilerParams(
            dimension_semantics=("parallel", "arbitrary"),
            vmem_limit_bytes=_vmem_limit(blocks, scratch, _nbytes((tm, D_MODEL), F32))),
        name="inproj",
    )(x2, mod, w_main, w_gate)


def _shifted_rows(x, halo8, shift, n_rows):
    k = abs(shift)
    rolled = pltpu.roll(x, (-shift) % n_rows, 0)
    r8 = lax.broadcasted_iota(jnp.int32, (SUBLANES, x.shape[1]), 0)
    if shift < 0:
        fix = pltpu.roll(halo8, k, 0)
        head = jnp.where(r8 < k, fix, rolled[:SUBLANES])
        return jnp.concatenate([head, rolled[SUBLANES:]], axis=0)
    fix = pltpu.roll(halo8, SUBLANES - k, 0)
    tail = jnp.where(r8 >= SUBLANES - k, fix, rolled[n_rows - SUBLANES:])
    return jnp.concatenate([rolled[:n_rows - SUBLANES], tail], axis=0)


def _gdnprep_kernel(x_ref, xp_ref, xn_ref, gate_ref, cw_ref, gp_ref,
                    q_ref, k_ref, v_ref, gcol_ref, grow_ref, ext_ref, *, tt):
    i = pl.program_id(1)
    first = i == 0
    last = i == pl.num_programs(1) - 1
    cw = cw_ref[...]
    for part, out_ref in enumerate((q_ref, k_ref, v_ref)):
        cols = slice(part * GDN_WIDTH, (part + 1) * GDN_WIDTH)
        x = x_ref[:, cols].astype(F32)
        prev8 = xp_ref[:, cols].astype(F32)[BF16_ROWS - SUBLANES:]
        next8 = xn_ref[:, cols].astype(F32)[:SUBLANES]
        prev8 = jnp.where(first, 0.0, prev8)
        next8 = jnp.where(last, 0.0, next8)
        w = cw[:, cols]
        ext_ref[:SUBLANES, :] = prev8
        ext_ref[SUBLANES:SUBLANES + tt, :] = x
        ext_ref[SUBLANES + tt:, :] = next8
        acc = x * w[CONV_W // 2:CONV_W // 2 + 1, :]
        for tap in range(CONV_W):
            shift = tap - CONV_W // 2
            if shift != 0:
                acc = acc + ext_ref[SUBLANES + shift:SUBLANES + shift + tt, :] * w[tap:tap + 1, :]
        y = _silu(acc)
        if part < 2:
            heads = []
            for h in range(GDN_HEADS):
                yh = y[:, h * HEAD_DIM:(h + 1) * HEAD_DIM]
                inv = lax.rsqrt(jnp.sum(yh * yh, axis=-1, keepdims=True) + NORM_EPS)
                if part == 0:
                    inv = inv * (HEAD_DIM ** -0.5)
                heads.append(yh * inv)
            y = jnp.concatenate(heads, axis=1)
        out_ref[...] = y.astype(out_ref.dtype)

    raw = gate_ref[...]
    gp = gp_ref[...]
    col = lax.broadcasted_iota(jnp.int32, raw.shape, 1)
    beta = _sigmoid(raw)
    z = raw + gp[0:1, :]
    softplus = jnp.maximum(z, 0.0) + jnp.log(1.0 + jnp.exp(-jnp.abs(z)))
    logdec = -jnp.exp(gp[1:2, :]) * softplus
    is_dec = (col >= N_GATE_COLS) & (col < 2 * N_GATE_COLS)
    gsrc = jnp.where(is_dec, logdec, 0.0)
    r = lax.broadcasted_iota(jnp.int32, (tt, tt), 0)
    c = lax.broadcasted_iota(jnp.int32, (tt, tt), 1)
    same = (r // GDN_CHUNK) == (c // GDN_CHUNK)
    p_lo = jnp.where(same & (c <= r), 1.0, 0.0).astype(F32)
    p_up = jnp.where(same & (c >= r), 1.0, 0.0).astype(F32)
    cum_lo = jnp.dot(p_lo, gsrc, precision=lax.Precision.HIGHEST, preferred_element_type=F32)
    cum_up = jnp.dot(p_up, gsrc, precision=lax.Precision.HIGHEST, preferred_element_type=F32)
    total = cum_lo + cum_up - gsrc
    fwd_col = col < N_GATE_COLS + GDN_HEADS
    gc = jnp.where(fwd_col, cum_lo, cum_up)
    tot_shift = pltpu.roll(total, N_GATE_COLS, 1)
    gcol = jnp.where(col < N_GATE_COLS, beta,
                     jnp.where(col < 2 * N_GATE_COLS, gc,
                               jnp.where(col < 3 * N_GATE_COLS, tot_shift, 0.0)))
    gcol_ref[...] = gcol
    grow_ref[...] = gcol.T


def _gdnprep_call(pm, gates, conv_w8, gate_params):
    bsz, seq, _ = pm.shape
    tt = 256
    hb = tt // BF16_ROWS
    n_halo = seq // BF16_ROWS
    blocks = (_nbytes((tt, GDN_QKV), BF16) + 2 * _nbytes((BF16_ROWS, GDN_QKV), BF16) + _nbytes((tt, LANES), F32)
              + _nbytes((SUBLANES, GDN_QKV), F32) + 3 * _nbytes((tt, GDN_WIDTH), BF16) + 2 * _nbytes((tt, LANES), F32))
    temps = 8 * _nbytes((tt, GDN_WIDTH), F32) + 4 * _nbytes((tt, tt), F32)
    return pl.pallas_call(
        functools.partial(_gdnprep_kernel, tt=tt),
        grid=(bsz, seq // tt),
        in_specs=[
            pl.BlockSpec((None, tt, GDN_QKV), lambda b, i: (b, i, 0)),
            pl.BlockSpec((None, BF16_ROWS, GDN_QKV), lambda b, i: (b, jnp.maximum(i * hb - 1, 0), 0)),
            pl.BlockSpec((None, BF16_ROWS, GDN_QKV), lambda b, i: (b, jnp.minimum((i + 1) * hb, n_halo - 1), 0)),
            pl.BlockSpec((None, tt, LANES), lambda b, i: (b, i, 0)),
            pl.BlockSpec((SUBLANES, GDN_QKV), lambda b, i: (0, 0)),
            pl.BlockSpec((SUBLANES, LANES), lambda b, i: (0, 0)),
        ],
        out_specs=[
            pl.BlockSpec((None, tt, GDN_WIDTH), lambda b, i: (b, i, 0)),
            pl.BlockSpec((None, tt, GDN_WIDTH), lambda b, i: (b, i, 0)),
            pl.BlockSpec((None, tt, GDN_WIDTH), lambda b, i: (b, i, 0)),
            pl.BlockSpec((None, tt, LANES), lambda b, i: (b, i, 0)),
            pl.BlockSpec((None, LANES, tt), lambda b, i: (b, 0, i)),
        ],
        out_shape=[
            jax.ShapeDtypeStruct((bsz, seq, GDN_WIDTH), BF16),
            jax.ShapeDtypeStruct((bsz, seq, GDN_WIDTH), BF16),
            jax.ShapeDtypeStruct((bsz, seq, GDN_WIDTH), BF16),
            jax.ShapeDtypeStruct((bsz, seq, LANES), F32),
            jax.ShapeDtypeStruct((bsz, LANES, seq), F32),
        ],
        scratch_shapes=[pltpu.VMEM((tt + 2 * SUBLANES, GDN_WIDTH), F32)],
        compiler_params=pltpu.CompilerParams(
            dimension_semantics=("parallel", "parallel"),
            vmem_limit_bytes=_vmem_limit(blocks, _nbytes((tt + 2 * SUBLANES, GDN_WIDTH), F32), temps)),
        name="gdnprep",
    )(pm, pm, pm, gates, conv_w8, gate_params)


def _gdn_kernel(qf_ref, kf_ref, vf_ref, gcf_ref, grf_ref,
                qb_ref, kb_ref, vb_ref, gcb_ref, grb_ref,
                of_ref, ob_ref, state_ref):
    @pl.when(pl.program_id(1) == 0)
    def _():
        state_ref[...] = jnp.zeros_like(state_ref)

    n = GDN_CHUNK
    r = lax.broadcasted_iota(jnp.int32, (n, n), 0)
    c = lax.broadcasted_iota(jnp.int32, (n, n), 1)
    eye = jnp.where(r == c, 1.0, 0.0).astype(F32)

    units = []
    for d, (q_ref, k_ref, v_ref, gc_ref, gr_ref, o_ref) in enumerate((
            (qf_ref, kf_ref, vf_ref, gcf_ref, grf_ref, of_ref),
            (qb_ref, kb_ref, vb_ref, gcb_ref, grb_ref, ob_ref))):
        gcol = gc_ref[...]
        grow = gr_ref[...]
        reverse = d == 1
        incl = (r <= c) if reverse else (r >= c)
        strict = (r < c) if reverse else (r > c)
        for h in range(GDN_HEADS):
            lanes = slice(h * HEAD_DIM, (h + 1) * HEAD_DIM)
            j = d * GDN_HEADS + h
            beta = gcol[:, j:j + 1]
            gcc = gcol[:, N_GATE_COLS + j:N_GATE_COLS + j + 1]
            gtc = gcol[:, 2 * N_GATE_COLS + j:2 * N_GATE_COLS + j + 1]
            gcr = grow[N_GATE_COLS + j:N_GATE_COLS + j + 1, :]
            gtr = grow[2 * N_GATE_COLS + j:2 * N_GATE_COLS + j + 1, :]
            q = q_ref[:, lanes]
            k = k_ref[:, lanes]
            kf = k.astype(F32)
            kb = kf * beta
            egc = jnp.exp(gcc)
            units.append(dict(
                d=d, h=h, lanes=lanes, o_ref=o_ref, strict=strict, q=q, k=k,
                decay=jnp.where(incl, jnp.exp(jnp.where(incl, gcc - gcr, 0.0)), 0.0),
                kbb=kb.astype(BF16),
                vb=(v_ref[:, lanes].astype(F32) * beta).astype(BF16),
                kbg=(kb * egc).astype(BF16),
                qg=(q.astype(F32) * egc).astype(BF16),
                kdec_t=(kf * jnp.exp(gtc - gcc)).T.astype(BF16),
                sdec=jnp.exp(gtr)))

    for u in units:
        aq = _dot_nt(jnp.concatenate([u["kbb"], u["q"]], axis=0), u["k"])
        low = jnp.where(u["strict"], aq[:n] * u["decay"], 0.0)
        u["qk"] = (aq[n:] * u["decay"]).astype(BF16)
        u["tinv"] = eye - low
        u["lb"] = low.astype(BF16)

    for u in units:
        u["power"] = _dot(u["lb"], u["lb"])
    n_factors = (n - 1).bit_length() - 1
    for it in range(n_factors):
        for u in units:
            pb = u["power"].astype(BF16)
            if it + 1 < n_factors:
                both = _dot(jnp.concatenate([u["tinv"].astype(BF16), pb], axis=0), pb)
                u["tinv"] = u["tinv"] + both[:n]
                u["power"] = both[n:]
            else:
                u["tinv"] = u["tinv"] + _dot(u["tinv"].astype(BF16), pb)

    for u in units:
        u["uw"] = _dot(u["tinv"].astype(BF16), jnp.concatenate([u["vb"], u["kbg"]], axis=1))
    for u in units:
        u["state"] = state_ref[u["d"], u["h"]]
        w = u["uw"][:, HEAD_DIM:].astype(BF16)
        u["ws"] = _dot(jnp.concatenate([w, u["qg"]], axis=0), u["state"].astype(BF16))
    for u in units:
        v_new = (u["uw"][:, :HEAD_DIM] - u["ws"][:n]).astype(BF16)
        u["os"] = _dot(jnp.concatenate([u["qk"], u["kdec_t"]], axis=0), v_new)
    for u in units:
        u["o_ref"][:, u["lanes"]] = u["ws"][n:] + u["os"][:n]
        state_ref[u["d"], u["h"]] = u["state"] * u["sdec"] + u["os"][n:]


def _gdn_call(q, k, v, gcol, grow):
    bsz, seq, _ = q.shape
    tb = GDN_CHUNK
    nb = seq // tb
    qkv_spec_f = pl.BlockSpec((None, tb, GDN_WIDTH), lambda b, i: (b, i, 0))
    qkv_spec_b = pl.BlockSpec((None, tb, GDN_WIDTH), lambda b, i: (b, nb - 1 - i, 0))
    gc_spec_f = pl.BlockSpec((None, tb, LANES), lambda b, i: (b, i, 0))
    gc_spec_b = pl.BlockSpec((None, tb, LANES), lambda b, i: (b, nb - 1 - i, 0))
    gr_spec_f = pl.BlockSpec((None, LANES, tb), lambda b, i: (b, 0, i))
    gr_spec_b = pl.BlockSpec((None, LANES, tb), lambda b, i: (b, 0, nb - 1 - i))
    blocks = 2 * (3 * _nbytes((tb, GDN_WIDTH), BF16) + 2 * _nbytes((tb, LANES), F32) + _nbytes((tb, GDN_WIDTH), F32))
    scratch = _nbytes((2, GDN_HEADS, HEAD_DIM, HEAD_DIM), F32)
    temps = 2 * GDN_HEADS * 24 * _nbytes((GDN_CHUNK, LANES), F32)
    return pl.pallas_call(
        _gdn_kernel,
        grid=(bsz, nb),
        in_specs=[qkv_spec_f, qkv_spec_f, qkv_spec_f, gc_spec_f, gr_spec_f,
                  qkv_spec_b, qkv_spec_b, qkv_spec_b, gc_spec_b, gr_spec_b],
        out_specs=[qkv_spec_f, qkv_spec_b],
        out_shape=[jax.ShapeDtypeStruct((bsz, seq, GDN_WIDTH), F32),
                   jax.ShapeDtypeStruct((bsz, seq, GDN_WIDTH), F32)],
        scratch_shapes=[pltpu.VMEM((2, GDN_HEADS, HEAD_DIM, HEAD_DIM), F32)],
        compiler_params=pltpu.CompilerParams(
            dimension_semantics=("parallel", "arbitrary"),
            vmem_limit_bytes=_vmem_limit(blocks, scratch, temps)),
        name="gdn",
    )(q, k, v, gcol, grow, q, k, v, gcol, grow)


def _rope_tables(seq):
    t = jnp.arange(seq)
    row = (t // GRID_W).astype(F32)
    col = (t % GRID_W).astype(F32)
    half = HEAD_DIM // 2
    inv_freq = ROPE_THETA ** (-jnp.arange(0, half, 2, dtype=F32) / half)
    ang_r = row[:, None] * inv_freq
    ang_c = col[:, None] * inv_freq
    ang = jnp.concatenate([ang_r, ang_r, ang_c, ang_c], -1)
    cos, sin = jnp.cos(ang), jnp.sin(ang)
    lane = jnp.arange(HEAD_DIM)
    first = (lane // (HEAD_DIM // 4)) % 2 == 0
    sin_up = jnp.where(first, -sin, 0.0)
    sin_dn = jnp.where(first, 0.0, sin)
    return cos, sin_up, sin_dn


def _norm_rope(x, w, cos, sin_up, sin_dn, scale):
    xn = x * lax.rsqrt(jnp.mean(x * x, axis=-1, keepdims=True) + NORM_EPS) * w
    quarter = HEAD_DIM // 4
    up = pltpu.roll(xn, HEAD_DIM - quarter, 1)
    dn = pltpu.roll(xn, quarter, 1)
    y = xn * cos + up * sin_up + dn * sin_dn
    return y * scale if scale != 1.0 else y


ATT_TK = 1024
FLASH_SUB = 256
VT_ROWS = HEAD_DIM + BF16_ROWS
LOG2_E = 1.4426950408889634


def _attprep_kernel(aq_ref, ak_ref, av_ref, cos_ref, su_ref, sd_ref, qw_ref, kw_ref, qt_ref, k_ref, vt_ref):
    cos, su, sd = cos_ref[...], su_ref[...], sd_ref[...]
    qw, kw = qw_ref[...], kw_ref[...]
    for h in range(ATT_HEADS):
        lanes = slice(h * HEAD_DIM, (h + 1) * HEAD_DIM)
        q = _norm_rope(aq_ref[:, lanes].astype(F32), qw, cos, su, sd, HEAD_DIM ** -0.5 * LOG2_E)
        qt_ref[lanes, :] = q.T.astype(qt_ref.dtype)
    for h in range(ATT_KV_HEADS):
        lanes = slice(h * HEAD_DIM, (h + 1) * HEAD_DIM)
        k_ref[:, lanes] = _norm_rope(ak_ref[:, lanes].astype(F32), kw, cos, su, sd, 1.0).astype(k_ref.dtype)
        vt_ref[h, :HEAD_DIM, :] = av_ref[:, lanes].astype(F32).T.astype(vt_ref.dtype)
        vt_ref[h, HEAD_DIM:, :] = jnp.ones((BF16_ROWS, vt_ref.shape[-1]), vt_ref.dtype)


def _attprep_call(pm, cos, sin_up, sin_dn, q_norm_w, k_norm_w):
    bsz, seq, _ = pm.shape
    tt = ATT_TK
    tab = pl.BlockSpec((tt, HEAD_DIM), lambda b, i: (i, 0))
    vec = pl.BlockSpec((1, HEAD_DIM), lambda b, i: (0, 0))
    blocks = (2 * _nbytes((tt, ATT_WIDTH), BF16) + 4 * _nbytes((tt, ATT_KV_WIDTH), BF16)
              + 3 * _nbytes((tt, HEAD_DIM), F32))
    return pl.pallas_call(
        _attprep_kernel,
        grid=(bsz, seq // tt),
        in_specs=[
            pl.BlockSpec((None, tt, ATT_WIDTH), lambda b, i: (b, i, PM_AQ // ATT_WIDTH)),
            pl.BlockSpec((None, tt, ATT_KV_WIDTH), lambda b, i: (b, i, PM_AK // ATT_KV_WIDTH)),
            pl.BlockSpec((None, tt, ATT_KV_WIDTH), lambda b, i: (b, i, PM_AV // ATT_KV_WIDTH)),
            tab, tab, tab, vec, vec,
        ],
        out_specs=[
            pl.BlockSpec((None, ATT_WIDTH, tt), lambda b, i: (b, 0, i)),
            pl.BlockSpec((None, tt, ATT_KV_WIDTH), lambda b, i: (b, i, 0)),
            pl.BlockSpec((None, ATT_KV_HEADS, None, VT_ROWS, tt), lambda b, i: (b, 0, i, 0, 0)),
        ],
        out_shape=[jax.ShapeDtypeStruct((bsz, ATT_WIDTH, seq), BF16),
                   jax.ShapeDtypeStruct((bsz, seq, ATT_KV_WIDTH), BF16),
                   jax.ShapeDtypeStruct((bsz, ATT_KV_HEADS, seq // tt, VT_ROWS, tt), BF16)],
        compiler_params=pltpu.CompilerParams(
            dimension_semantics=("parallel", "parallel"),
            vmem_limit_bytes=_vmem_limit(blocks, temp_bytes=8 * _nbytes((tt, HEAD_DIM), F32))),
        name="attprep",
    )(pm, pm, pm, cos, sin_up, sin_dn, q_norm_w.reshape(1, HEAD_DIM), k_norm_w.reshape(1, HEAD_DIM))


def _flash_kernel(qt_ref, qtn_ref, k_ref, vt_ref, o_ref, m_ref, acc_ref, qc_ref, qn_ref,
                  sa_ref, sb_ref, mca_ref, mcb_ref, *, tq, tk, ts, n_k):
    m_ref[...] = jnp.full_like(m_ref, -jnp.inf)
    acc_ref[...] = jnp.zeros_like(acc_ref)
    for g in range(ATT_GROUP):
        qc_ref[:, g * tq:(g + 1) * tq] = qt_ref[g * HEAD_DIM:(g + 1) * HEAD_DIM, :]
        qn_ref[:, g * tq:(g + 1) * tq] = qtn_ref[g * HEAD_DIM:(g + 1) * HEAD_DIM, :]

    n_sub = tk // ts
    bufs = ((sa_ref, mca_ref), (sb_ref, mcb_ref))

    def scores(j, q_ref, s_ref, i):
        start = pl.multiple_of(j * tk + i * ts, ts)
        st = _dot(k_ref[pl.ds(start, ts), :], q_ref[...])
        s_ref[i * ts:(i + 1) * ts, :] = st
        return jnp.max(st, axis=0, keepdims=True)

    def weigh(j, s_ref, i, m_new):
        p = jnp.exp2(s_ref[i * ts:(i + 1) * ts, :] - m_new)
        return _dot(vt_ref[j, :, i * ts:(i + 1) * ts], p.astype(BF16))

    def step(j, parity, j_next, q_next):
        s_cur, mc_cur = bufs[parity]
        s_nxt, mc_nxt = bufs[1 - parity]
        m_prev = m_ref[...]
        m_new = jnp.maximum(m_prev, mc_cur[...])
        alpha = jnp.exp2(m_prev - m_new)
        mc_next, pv = None, None
        for i in range(n_sub):
            mx = scores(j_next, q_next, s_nxt, i)
            mc_next = mx if mc_next is None else jnp.maximum(mc_next, mx)
            pvi = weigh(j, s_cur, i, m_new)
            pv = pvi if pv is None else pv + pvi
        mc_nxt[...] = mc_next
        acc_ref[...] = alpha * acc_ref[...] + pv
        m_ref[...] = m_new

    @pl.when(pl.program_id(2) == 0)
    def _():
        mc0 = None
        for i in range(n_sub):
            mx = scores(0, qc_ref, sa_ref, i)
            mc0 = mx if mc0 is None else jnp.maximum(mc0, mx)
        mca_ref[...] = mc0

    def body(jj, carry):
        step(2 * jj, 0, 2 * jj + 1, qc_ref)
        step(2 * jj + 1, 1, 2 * jj + 2, qc_ref)
        return carry

    lax.fori_loop(0, n_k // 2 - 1, body, 0)
    step(n_k - 2, 0, n_k - 1, qc_ref)
    step(n_k - 1, 1, 0, qn_ref)
    for g in range(ATT_GROUP):
        lanes = slice(g * tq, (g + 1) * tq)
        out_t = acc_ref[:HEAD_DIM, lanes] / acc_ref[HEAD_DIM:HEAD_DIM + 1, lanes]
        o_ref[:, g * HEAD_DIM:(g + 1) * HEAD_DIM] = out_t.T.astype(o_ref.dtype)


def _flash_call(qt, kr, vt):
    bsz, _, seq = qt.shape
    tq, tk = 256, ATT_TK
    gw = ATT_GROUP * HEAD_DIM
    nq = ATT_GROUP * tq
    n_q, n_k = seq // tq, seq // tk
    assert n_k % 2 == 0, "the score buffers alternate by key-tile parity across query tiles"
    blocks = 3 * _nbytes((gw, tq), BF16) + _nbytes((seq, HEAD_DIM), BF16) + _nbytes((seq, VT_ROWS), BF16)
    scratch = (3 * _nbytes((SUBLANES, nq), F32) + _nbytes((VT_ROWS, nq), F32) + 2 * _nbytes((HEAD_DIM, nq), BF16)
               + 2 * _nbytes((tk, nq), F32))
    temps = 2 * _nbytes((tk, nq), F32)
    return pl.pallas_call(
        functools.partial(_flash_kernel, tq=tq, tk=tk, ts=FLASH_SUB, n_k=n_k),
        grid=(bsz, ATT_KV_HEADS, n_q),
        in_specs=[
            pl.BlockSpec((None, gw, tq), lambda b, h, i: (b, h, i)),
            pl.BlockSpec((None, gw, tq), lambda b, h, i: (b, h, jnp.minimum(i + 1, n_q - 1))),
            pl.BlockSpec((None, seq, HEAD_DIM), lambda b, h, i: (b, 0, h)),
            pl.BlockSpec((None, None, n_k, VT_ROWS, tk), lambda b, h, i: (b, h, 0, 0, 0)),
        ],
        out_specs=pl.BlockSpec((None, tq, gw), lambda b, h, i: (b, i, h)),
        out_shape=jax.ShapeDtypeStruct((bsz, seq, ATT_WIDTH), BF16),
        scratch_shapes=[pltpu.VMEM((1, nq), F32),
                        pltpu.VMEM((VT_ROWS, nq), F32), pltpu.VMEM((HEAD_DIM, nq), BF16),
                        pltpu.VMEM((HEAD_DIM, nq), BF16),
                        pltpu.VMEM((tk, nq), F32), pltpu.VMEM((tk, nq), F32),
                        pltpu.VMEM((1, nq), F32), pltpu.VMEM((1, nq), F32)],
        compiler_params=pltpu.CompilerParams(
            dimension_semantics=("parallel", "parallel", "arbitrary"),
            vmem_limit_bytes=_vmem_limit(blocks, scratch, temps)),
        name="flash",
    )(qt, qt, kr, vt)


def _outproj_kernel(of_ref, ob_ref, z_ref, oa_ref, x_ref, mod_ref, w_ref, nw_ref, lg_ref, lb_ref, o_ref):
    nw = nw_ref[...]
    heads = []
    for h in range(GDN_HEADS):
        lanes = slice(h * HEAD_DIM, (h + 1) * HEAD_DIM)
        o = of_ref[:, lanes] + ob_ref[:, lanes]
        on = o * lax.rsqrt(jnp.mean(o * o, axis=-1, keepdims=True) + NORM_EPS) * nw
        heads.append((on * _silu(z_ref[:, lanes].astype(F32))).astype(BF16))
    og = jnp.concatenate(heads, axis=1)
    mix = _dot(og, w_ref[:GDN_WIDTH, :]) + _dot(oa_ref[...], w_ref[GDN_WIDTH:, :])
    m = mod_ref[...]
    y = DEEPNORM_ALPHA * x_ref[...] + m[MOD_G_M:MOD_G_M + 1, :] * mix
    o_ref[...] = _layer_norm_rows(y, lg_ref[...], lb_ref[...])


def _outproj_call(o_f, o_b, pm2, o_att, x2, mod, w_out, gdn_norm_w, ln_g, ln_b, seq):
    m = x2.shape[0]
    tm = 256
    blocks = (2 * _nbytes((tm, GDN_WIDTH), F32) + 2 * _nbytes((tm, GDN_WIDTH), BF16)
              + 2 * _nbytes((tm, D_MODEL), F32) + _nbytes((MOD_ROWS, D_MODEL), F32)
              + _nbytes((D_MODEL, D_MODEL), BF16))
    vec = pl.BlockSpec((1, D_MODEL), lambda i: (0, 0))
    return pl.pallas_call(
        _outproj_kernel,
        grid=(m // tm,),
        in_specs=[
            pl.BlockSpec((tm, GDN_WIDTH), lambda i: (i, 0)),
            pl.BlockSpec((tm, GDN_WIDTH), lambda i: (i, 0)),
            pl.BlockSpec((tm, GDN_WIDTH), lambda i: (i, PM_Z // GDN_WIDTH)),
            pl.BlockSpec((tm, ATT_WIDTH), lambda i: (i, 0)),
            pl.BlockSpec((tm, D_MODEL), lambda i: (i, 0)),
            pl.BlockSpec((None, MOD_ROWS, D_MODEL), lambda i: ((i * tm) // seq, 0, 0)),
            pl.BlockSpec((D_MODEL, D_MODEL), lambda i: (0, 0)),
            pl.BlockSpec((1, HEAD_DIM), lambda i: (0, 0)),
            vec, vec,
        ],
        out_specs=pl.BlockSpec((tm, D_MODEL), lambda i: (i, 0)),
        out_shape=jax.ShapeDtypeStruct((m, D_MODEL), F32),
        compiler_params=pltpu.CompilerParams(
            dimension_semantics=("parallel",),
            vmem_limit_bytes=_vmem_limit(blocks, temp_bytes=4 * _nbytes((tm, D_MODEL), F32))),
        name="outproj",
    )(o_f, o_b, pm2, o_att, x2, mod, w_out, gdn_norm_w.reshape(1, HEAD_DIM),
      ln_g.reshape(1, D_MODEL), ln_b.reshape(1, D_MODEL))


POOL_REACH = SUBLANES
assert max(POOL_WINDOWS) // 2 <= POOL_REACH


def _pool_kernel(x_ref, xp_ref, xn_ref, mod_ref, pw_ref, ps_ref, lg_ref, lb_ref, o_ref, *, tt, seq):
    t0 = pl.program_id(1) * tt
    m = mod_ref[...]
    sc = 1.0 + m[MOD_SC_M:MOD_SC_M + 1, :]
    sh = m[MOD_SH_M:MOD_SH_M + 1, :]
    x = x_ref[...]
    h_main = x * sc + sh
    h_prev = xp_ref[...] * sc + sh
    h_next = xn_ref[...] * sc + sh
    r = lax.broadcasted_iota(jnp.int32, (tt, tt), 0)
    c = lax.broadcasted_iota(jnp.int32, (tt, tt), 1)
    pos = t0 + lax.broadcasted_iota(jnp.int32, (tt, 1), 0)
    r8 = lax.broadcasted_iota(jnp.int32, (POOL_REACH, 1), 0)
    outs = []
    for gi, win in enumerate(POOL_WINDOWS):
        back, fwd = win // 2, win - 1 - win // 2
        lanes = slice(gi * POOL_GROUP, (gi + 1) * POOL_GROUP)
        band = jnp.where((c >= r - back) & (c <= r + fwd), 1.0, 0.0).astype(BF16)
        hm = h_main[:, lanes]
        wsum = _dot(band, hm.astype(BF16))
        top = jnp.zeros((POOL_REACH, POOL_GROUP), F32)
        bot = jnp.zeros((POOL_REACH, POOL_GROUP), F32)
        for k in range(POOL_REACH):
            if POOL_REACH - k <= back:
                use = (r8 <= k - POOL_REACH + back) & (t0 - POOL_REACH + k >= 0)
                top = top + jnp.where(use, 1.0, 0.0) * h_prev[k:k + 1, lanes]
            if k < fwd:
                use = (r8 >= k + POOL_REACH - fwd) & (t0 + tt + k < seq)
                bot = bot + jnp.where(use, 1.0, 0.0) * h_next[k:k + 1, lanes]
        wsum = jnp.concatenate([wsum[:POOL_REACH] + top, wsum[POOL_REACH:tt - POOL_REACH],
                                wsum[tt - POOL_REACH:] + bot], axis=0)
        cnt = (jnp.minimum(pos + fwd, seq - 1) - jnp.maximum(pos - back, 0) + 1).astype(F32)
        diff = wsum / cnt - hm
        outs.append(_dot(diff.astype(BF16), pw_ref[gi]))
    mix = jnp.concatenate(outs, axis=1) * ps_ref[...]
    y = DEEPNORM_ALPHA * x + m[MOD_G_M:MOD_G_M + 1, :] * mix
    o_ref[...] = _layer_norm_rows(y, lg_ref[...], lb_ref[...])


def _pool_call(x, mod, pool_w, pool_scale, ln_g, ln_b):
    bsz, seq, _ = x.shape
    tt = 256
    hb = tt // POOL_REACH
    n_halo = seq // POOL_REACH
    vec = pl.BlockSpec((1, D_MODEL), lambda b, i: (0, 0))
    blocks = (2 * _nbytes((tt, D_MODEL), F32) + 2 * _nbytes((POOL_REACH, D_MODEL), F32)
              + _nbytes((MOD_ROWS, D_MODEL), F32) + _nbytes(pool_w.shape, BF16))
    temps = 5 * _nbytes((tt, D_MODEL), F32)
    return pl.pallas_call(
        functools.partial(_pool_kernel, tt=tt, seq=seq),
        grid=(bsz, seq // tt),
        in_specs=[
            pl.BlockSpec((None, tt, D_MODEL), lambda b, i: (b, i, 0)),
            pl.BlockSpec((None, POOL_REACH, D_MODEL), lambda b, i: (b, jnp.maximum(i * hb - 1, 0), 0)),
            pl.BlockSpec((None, POOL_REACH, D_MODEL), lambda b, i: (b, jnp.minimum((i + 1) * hb, n_halo - 1), 0)),
            pl.BlockSpec((None, MOD_ROWS, D_MODEL), lambda b, i: (b, 0, 0)),
            pl.BlockSpec(pool_w.shape, lambda b, i: (0, 0, 0)),
            vec, vec, vec,
        ],
        out_specs=pl.BlockSpec((None, tt, D_MODEL), lambda b, i: (b, i, 0)),
        out_shape=jax.ShapeDtypeStruct((bsz, seq, D_MODEL), F32),
        compiler_params=pltpu.CompilerParams(
            dimension_semantics=("parallel", "parallel"),
            vmem_limit_bytes=_vmem_limit(blocks, temp_bytes=temps)),
        name="pool",
    )(x, x, x, mod, pool_w, pool_scale.reshape(1, D_MODEL), ln_g.reshape(1, D_MODEL), ln_b.reshape(1, D_MODEL))


MLP_TAIL_PARTS = 2


def _mlp_kernel(x_ref, mod_ref, w1_ref, w2_ref, lg_ref, lb_ref, o_ref, h_ref, acc_ref):
    j = pl.program_id(1)

    @pl.when(j == 0)
    def _():
        m = mod_ref[...]
        h = x_ref[...] * (1.0 + m[MOD_SC_F:MOD_SC_F + 1, :]) + m[MOD_SH_F:MOD_SH_F + 1, :]
        h_ref[...] = h.astype(BF16)
        acc_ref[...] = jnp.zeros_like(acc_ref)

    def partial_out(rows):
        u = jnp.maximum(_dot(h_ref[rows, :], w1_ref[...]), 0.0)
        return _dot((u * u).astype(BF16), w2_ref[...])

    last = pl.num_programs(1) - 1

    @pl.when(j < last)
    def _():
        acc_ref[...] += partial_out(slice(None))

    @pl.when(j == last)
    def _():
        m = mod_ref[...]
        tm = x_ref.shape[0]
        for part in range(MLP_TAIL_PARTS):
            rows = slice(part * tm // MLP_TAIL_PARTS, (part + 1) * tm // MLP_TAIL_PARTS)
            total = acc_ref[rows, :] + partial_out(rows)
            y = DEEPNORM_ALPHA * x_ref[rows, :] + m[MOD_G_F:MOD_G_F + 1, :] * total
            o_ref[rows, :] = _layer_norm_rows(y, lg_ref[...], lb_ref[...])


def _mlp_call(x2, mod, w1, w2, ln_g, ln_b, seq):
    m = x2.shape[0]
    tm, tf = 512, 1024
    vec = pl.BlockSpec((1, D_MODEL), lambda i, j: (0, 0))
    blocks = (2 * _nbytes((tm, D_MODEL), F32) + _nbytes((MOD_ROWS, D_MODEL), F32)
              + 2 * _nbytes((D_MODEL, tf), BF16))
    scratch = _nbytes((tm, D_MODEL), BF16) + _nbytes((tm, D_MODEL), F32)
    temps = 2 * _nbytes((tm, tf), F32) + _nbytes((tm, D_MODEL), F32)
    return pl.pallas_call(
        _mlp_kernel,
        grid=(m // tm, D_FF // tf),
        in_specs=[
            pl.BlockSpec((tm, D_MODEL), lambda i, j: (i, 0)),
            pl.BlockSpec((None, MOD_ROWS, D_MODEL), lambda i, j: ((i * tm) // seq, 0, 0)),
            pl.BlockSpec((D_MODEL, tf), lambda i, j: (0, j)),
            pl.BlockSpec((tf, D_MODEL), lambda i, j: (j, 0)),
            vec, vec,
        ],
        out_specs=pl.BlockSpec((tm, D_MODEL), lambda i, j: (i, 0)),
        out_shape=jax.ShapeDtypeStruct((m, D_MODEL), F32),
        scratch_shapes=[pltpu.VMEM((tm, D_MODEL), BF16), pltpu.VMEM((tm, D_MODEL), F32)],
        compiler_params=pltpu.CompilerParams(
            dimension_semantics=("parallel", "arbitrary"),
            vmem_limit_bytes=_vmem_limit(blocks, scratch, temps)),
        name="mlp",
    )(x2, mod, w1, w2, ln_g.reshape(1, D_MODEL), ln_b.reshape(1, D_MODEL))


def _mod_table(cond_rows):
    bsz = cond_rows.shape[0]
    t = cond_rows.reshape(bsz, 6, D_MODEL)
    return jnp.pad(t, ((0, 0), (0, MOD_ROWS - 6), (0, 0)))


def _pack_weights(w_in, conv_w, a_log, dt_bias, w_out, pool_w, mlp_w1, mlp_w2):
    wi = w_in[0]
    w_main = jnp.concatenate(
        [wi[:, :OFF_BETA], wi[:, OFF_AQ:]], axis=1).astype(BF16)
    w_gate = jnp.pad(wi[:, OFF_BETA:OFF_AQ], ((0, 0), (0, LANES - 2 * N_GATE_COLS))).astype(BF16)
    conv_w8 = jnp.pad(conv_w[0], ((0, SUBLANES - CONV_W), (0, 0)))
    gate_params = jnp.zeros((SUBLANES, LANES), F32)
    gate_params = gate_params.at[0, N_GATE_COLS:2 * N_GATE_COLS].set(dt_bias[0].reshape(-1))
    gate_params = gate_params.at[1, N_GATE_COLS:2 * N_GATE_COLS].set(a_log[0].reshape(-1))
    return dict(w_main=w_main, w_gate=w_gate, conv_w8=conv_w8, gate_params=gate_params,
                w_out=w_out[0].astype(BF16), pool_w=pool_w[0].astype(BF16),
                w1=mlp_w1.astype(BF16), w2=mlp_w2.astype(BF16))


def _trunk(x, cond, pk, gdn_norm_w, q_norm_w, k_norm_w, pool_scale, ln_g, ln_b):
    bsz, seq, _ = x.shape
    m = bsz * seq
    mod0 = _mod_table(cond[0])
    mod1 = _mod_table(cond[1])
    x2 = x.reshape(m, D_MODEL)

    pm2, gates2 = _inproj_call(x2, mod0, pk["w_main"], pk["w_gate"], seq)
    pm = pm2.reshape(bsz, seq, PM_WIDTH)
    gq, gk, gv, gcol, grow = _gdnprep_call(pm, gates2.reshape(bsz, seq, LANES), pk["conv_w8"], pk["gate_params"])
    o_f, o_b = _gdn_call(gq, gk, gv, gcol, grow)
    cos, sin_up, sin_dn = _rope_tables(seq)
    qt, kr, vt = _attprep_call(pm, cos, sin_up, sin_dn, q_norm_w[0], k_norm_w[0])
    o_att = _flash_call(qt, kr, vt)
    x2 = _outproj_call(o_f.reshape(m, GDN_WIDTH), o_b.reshape(m, GDN_WIDTH), pm2, o_att.reshape(m, ATT_WIDTH),
                       x2, mod0, pk["w_out"], gdn_norm_w[0], ln_g[0, 0], ln_b[0, 0], seq)
    x2 = _mlp_call(x2, mod0, pk["w1"][0], pk["w2"][0], ln_g[0, 1], ln_b[0, 1], seq)

    x3 = _pool_call(x2.reshape(bsz, seq, D_MODEL), mod1, pk["pool_w"], pool_scale[0], ln_g[1, 0], ln_b[1, 0])
    x2 = _mlp_call(x3.reshape(m, D_MODEL), mod1, pk["w1"][1], pk["w2"][1], ln_g[1, 1], ln_b[1, 1], seq)
    return x2.reshape(bsz, seq, D_MODEL)


def kernel(x_prompt, x_sample, c_prompt, c_sample, w_in, conv_w, a_log, dt_bias, gdn_norm_w, q_norm_w,
           k_norm_w, w_out, pool_w, pool_scale, mlp_w1, mlp_w2, ada_w, ada_b, ln_g, ln_b):
    bp, bs = c_prompt.shape[0], c_sample.shape[0]
    c_all = jnp.concatenate([c_prompt, c_sample], axis=0)
    c_all = jnp.pad(c_all, ((0, (-c_all.shape[0]) % SUBLANES), (0, 0)))
    cond = _ada_call(c_all, ada_w, ada_b)
    pk = _pack_weights(w_in, conv_w, a_log, dt_bias, w_out, pool_w, mlp_w1, mlp_w2)
    y_prompt = _trunk(x_prompt, cond[:, :bp], pk, gdn_norm_w, q_norm_w, k_norm_w, pool_scale, ln_g, ln_b)
    y_sample = _trunk(x_sample, cond[:, bp:bp + bs], pk, gdn_norm_w, q_norm_w, k_norm_w, pool_scale, ln_g, ln_b)
    return (y_prompt, y_sample)
```

```python
import functools

import jax
import jax.numpy as jnp
from jax import lax
from jax.experimental import pallas as pl
from jax.experimental.pallas import tpu as pltpu

F32 = jnp.float32
BF16 = jnp.bfloat16

D_MODEL = 2048
DEPTH = 2
GRID_W = 64
HEAD_DIM = 128
GDN_HEADS = 8
ATT_HEADS = 8
ATT_KV_HEADS = 2
ATT_GROUP = ATT_HEADS // ATT_KV_HEADS
GDN_WIDTH = GDN_HEADS * HEAD_DIM
ATT_WIDTH = ATT_HEADS * HEAD_DIM
ATT_KV_WIDTH = ATT_KV_HEADS * HEAD_DIM
CONV_W = 5
ROPE_THETA = 10000.0
POOL_WINDOWS = (2, 4, 8, 16)
N_POOL_GROUPS = 4
POOL_GROUP = D_MODEL // N_POOL_GROUPS
D_FF = 4 * D_MODEL
DEEPNORM_ALPHA = (2 * DEPTH) ** 0.25
NORM_EPS = 1e-6
LN_EPS = 1e-5

GDN_QKV = 3 * GDN_WIDTH
OFF_Z = GDN_QKV
OFF_BETA = OFF_Z + GDN_WIDTH
OFF_A = OFF_BETA + 2 * GDN_HEADS
OFF_AQ = OFF_A + 2 * GDN_HEADS
OFF_AK = OFF_AQ + ATT_WIDTH
OFF_AV = OFF_AK + ATT_KV_WIDTH
D_IN = OFF_AV + ATT_KV_WIDTH

PM_QKV = 0
PM_Z = GDN_QKV
PM_AQ = PM_Z + GDN_WIDTH
PM_AK = PM_AQ + ATT_WIDTH
PM_AV = PM_AK + ATT_KV_WIDTH
PM_WIDTH = PM_AV + ATT_KV_WIDTH

LANES = 128
SUBLANES = 8
BF16_ROWS = 16
V7X_VMEM_BYTES = 64 * 1024 * 1024

GDN_CHUNK = 128
N_GATE_COLS = 2 * GDN_HEADS

MOD_SH_M, MOD_SC_M, MOD_G_M, MOD_SH_F, MOD_SC_F, MOD_G_F = range(6)
MOD_ROWS = 8


def _vmem_limit(block_bytes, scratch_bytes=0, temp_bytes=0):
    need = 2 * block_bytes + scratch_bytes + temp_bytes
    return int(min(need + need // 4, V7X_VMEM_BYTES - 8 * 1024 * 1024))


def _nbytes(shape, dtype):
    n = 1
    for s in shape:
        n *= s
    return n * jnp.dtype(dtype).itemsize


def _sigmoid(x):
    return 1.0 / (1.0 + jnp.exp(-x))


def _silu(x):
    return x * _sigmoid(x)


def _layer_norm_rows(y, g, b):
    mu = jnp.mean(y, axis=-1, keepdims=True)
    yc = y - mu
    var = jnp.mean(yc * yc, axis=-1, keepdims=True)
    return yc * lax.rsqrt(var + LN_EPS) * g + b


def _dot(a, b):
    return jnp.dot(a, b, preferred_element_type=F32)


def _dot_nt(a, b):
    return lax.dot_general(a, b, (((1,), (1,)), ((), ())), preferred_element_type=F32)


def _ada_kernel(c_ref, w_ref, b_ref, o_ref):
    c = c_ref[...]
    s = _silu(c).astype(BF16)
    o_ref[...] = _dot(s, w_ref[...].astype(BF16)) + b_ref[...]


def _ada_call(c_all, ada_w, ada_b):
    rows = c_all.shape[0]
    n = ada_w.shape[-1]
    tn = 1024
    blocks = _nbytes((D_MODEL, tn), F32) + _nbytes((rows, D_MODEL), F32) + 2 * _nbytes((rows, tn), F32)
    return pl.pallas_call(
        _ada_kernel,
        grid=(DEPTH, n // tn),
        in_specs=[
            pl.BlockSpec((rows, D_MODEL), lambda l, j: (0, 0)),
            pl.BlockSpec((None, D_MODEL, tn), lambda l, j: (l, 0, j)),
            pl.BlockSpec((None, 1, tn), lambda l, j: (l, 0, j)),
        ],
        out_specs=pl.BlockSpec((None, rows, tn), lambda l, j: (l, 0, j)),
        out_shape=jax.ShapeDtypeStruct((DEPTH, rows, n), F32),
        compiler_params=pltpu.CompilerParams(
            dimension_semantics=("parallel", "parallel"),
            vmem_limit_bytes=_vmem_limit(blocks, temp_bytes=_nbytes((D_MODEL, tn), BF16))),
        name="ada",
    )(c_all, ada_w, ada_b.reshape(DEPTH, 1, n))


INPROJ_TN = PM_WIDTH // 2
ATT_COL0 = PM_AQ - INPROJ_TN
assert PM_AQ >= INPROJ_TN and ATT_COL0 % LANES == 0


def _inproj_kernel(x_ref, mod_ref, w_ref, wg_ref, cos_ref, su_ref, sd_ref, qw_ref, kw_ref,
                   o_ref, g_ref, qt_ref, k_ref, vt_ref, h_ref):
    j = pl.program_id(1)

    @pl.when(j == 0)
    def _():
        m = mod_ref[...]
        h = x_ref[...] * (1.0 + m[MOD_SC_M:MOD_SC_M + 1, :]) + m[MOD_SH_M:MOD_SH_M + 1, :]
        hb = h.astype(BF16)
        h_ref[...] = hb
        g_ref[...] = _dot(hb, wg_ref[...])
        o_ref[...] = _dot(hb, w_ref[...]).astype(o_ref.dtype)

    @pl.when(j == 1)
    def _():
        hb = h_ref[...]
        att = _dot(hb, w_ref[:, ATT_COL0:])
        o_ref[:, :ATT_COL0] = _dot(hb, w_ref[:, :ATT_COL0]).astype(o_ref.dtype)
        o_ref[:, ATT_COL0:] = att.astype(o_ref.dtype)
        cos, su, sd = cos_ref[...], su_ref[...], sd_ref[...]
        qw, kw = qw_ref[...], kw_ref[...]
        for hd in range(ATT_HEADS):
            lanes = slice(hd * HEAD_DIM, (hd + 1) * HEAD_DIM)
            q = _norm_rope(att[:, lanes], qw, cos, su, sd, HEAD_DIM ** -0.5 * LOG2_E)
            qt_ref[lanes, :] = q.T.astype(qt_ref.dtype)
        for hd in range(ATT_KV_HEADS):
            lanes = slice(hd * HEAD_DIM, (hd + 1) * HEAD_DIM)
            ak = att[:, ATT_WIDTH + hd * HEAD_DIM:ATT_WIDTH + (hd + 1) * HEAD_DIM]
            av = att[:, ATT_WIDTH + ATT_KV_WIDTH + hd * HEAD_DIM:ATT_WIDTH + ATT_KV_WIDTH + (hd + 1) * HEAD_DIM]
            k_ref[:, lanes] = _norm_rope(ak, kw, cos, su, sd, 1.0).astype(k_ref.dtype)
            vt_ref[hd, :HEAD_DIM, :] = av.T.astype(vt_ref.dtype)
            vt_ref[hd, HEAD_DIM:, :] = jnp.ones((BF16_ROWS, vt_ref.shape[-1]), vt_ref.dtype)


def _inproj_call(x2, mod, w_main, w_gate, cos, sin_up, sin_dn, q_norm_w, k_norm_w, bsz, seq):
    m = x2.shape[0]
    tm, tn = 512, INPROJ_TN
    nt = seq // tm
    per_key_tile = ATT_TK // tm
    assert PM_WIDTH == 2 * tn and ATT_TK % tm == 0
    tab = pl.BlockSpec((tm, HEAD_DIM), lambda i, j: (i % nt, 0))
    vec = pl.BlockSpec((1, HEAD_DIM), lambda i, j: (0, 0))
    blocks = (_nbytes((tm, D_MODEL), F32) + _nbytes((MOD_ROWS, D_MODEL), F32) + _nbytes((D_MODEL, tn), BF16)
              + _nbytes((D_MODEL, LANES), BF16) + _nbytes((tm, tn), BF16) + _nbytes((tm, LANES), F32)
              + 3 * _nbytes((tm, HEAD_DIM), F32) + _nbytes((ATT_WIDTH, tm), BF16)
              + _nbytes((tm, ATT_KV_WIDTH), BF16) + _nbytes((ATT_KV_HEADS, VT_ROWS, tm), BF16))
    scratch = _nbytes((tm, D_MODEL), BF16)
    temps = _nbytes((tm, D_MODEL), F32) + 2 * _nbytes((tm, tn), F32)
    return pl.pallas_call(
        _inproj_kernel,
        grid=(m // tm, PM_WIDTH // tn),
        in_specs=[
            pl.BlockSpec((tm, D_MODEL), lambda i, j: (i, 0)),
            pl.BlockSpec((None, MOD_ROWS, D_MODEL), lambda i, j: (i // nt, 0, 0)),
            pl.BlockSpec((D_MODEL, tn), lambda i, j: (0, j)),
            pl.BlockSpec((D_MODEL, LANES), lambda i, j: (0, 0)),
            tab, tab, tab, vec, vec,
        ],
        out_specs=[
            pl.BlockSpec((tm, tn), lambda i, j: (i, j)),
            pl.BlockSpec((tm, LANES), lambda i, j: (i, 0)),
            pl.BlockSpec((None, ATT_WIDTH, tm), lambda i, j: (i // nt, 0, i % nt)),
            pl.BlockSpec((tm, ATT_KV_WIDTH), lambda i, j: (i, 0)),
            pl.BlockSpec((None, ATT_KV_HEADS, None, VT_ROWS, tm),
                         lambda i, j: (i // nt, 0, (i % nt) // per_key_tile, 0, i % per_key_tile)),
        ],
        out_shape=[
            jax.ShapeDtypeStruct((m, PM_WIDTH), BF16),
            jax.ShapeDtypeStruct((m, LANES), F32),
            jax.ShapeDtypeStruct((bsz, ATT_WIDTH, seq), BF16),
            jax.ShapeDtypeStruct((m, ATT_KV_WIDTH), BF16),
            jax.ShapeDtypeStruct((bsz, ATT_KV_HEADS, seq // ATT_TK, VT_ROWS, ATT_TK), BF16),
        ],
        scratch_shapes=[pltpu.VMEM((tm, D_MODEL), BF16)],
        compiler_params=pltpu.CompilerParams(
            dimension_semantics=("parallel", "arbitrary"),
            vmem_limit_bytes=_vmem_limit(blocks, scratch, temps)),
        name="inproj",
    )(x2, mod, w_main, w_gate, cos, sin_up, sin_dn, q_norm_w.reshape(1, HEAD_DIM), k_norm_w.reshape(1, HEAD_DIM))


def _shifted_rows(x, halo8, shift, n_rows):
    k = abs(shift)
    rolled = pltpu.roll(x, (-shift) % n_rows, 0)
    r8 = lax.broadcasted_iota(jnp.int32, (SUBLANES, x.shape[1]), 0)
    if shift < 0:
        fix = pltpu.roll(halo8, k, 0)
        head = jnp.where(r8 < k, fix, rolled[:SUBLANES])
        return jnp.concatenate([head, rolled[SUBLANES:]], axis=0)
    fix = pltpu.roll(halo8, SUBLANES - k, 0)
    tail = jnp.where(r8 >= SUBLANES - k, fix, rolled[n_rows - SUBLANES:])
    return jnp.concatenate([rolled[:n_rows - SUBLANES], tail], axis=0)


def _gdnprep_kernel(x_ref, xp_ref, xn_ref, gate_ref, cw_ref, gp_ref,
                    q_ref, k_ref, v_ref, gcol_ref, grow_ref, ext_ref, *, tt):
    i = pl.program_id(1)
    first = i == 0
    last = i == pl.num_programs(1) - 1
    cw = cw_ref[...]
    for part, out_ref in enumerate((q_ref, k_ref, v_ref)):
        cols = slice(part * GDN_WIDTH, (part + 1) * GDN_WIDTH)
        x = x_ref[:, cols].astype(F32)
        prev8 = xp_ref[:, cols].astype(F32)[BF16_ROWS - SUBLANES:]
        next8 = xn_ref[:, cols].astype(F32)[:SUBLANES]
        prev8 = jnp.where(first, 0.0, prev8)
        next8 = jnp.where(last, 0.0, next8)
        w = cw[:, cols]
        ext_ref[:SUBLANES, :] = prev8
        ext_ref[SUBLANES:SUBLANES + tt, :] = x
        ext_ref[SUBLANES + tt:, :] = next8
        acc = x * w[CONV_W // 2:CONV_W // 2 + 1, :]
        for tap in range(CONV_W):
            shift = tap - CONV_W // 2
            if shift != 0:
                acc = acc + ext_ref[SUBLANES + shift:SUBLANES + shift + tt, :] * w[tap:tap + 1, :]
        y = _silu(acc)
        if part < 2:
            heads = []
            for h in range(GDN_HEADS):
                yh = y[:, h * HEAD_DIM:(h + 1) * HEAD_DIM]
                inv = lax.rsqrt(jnp.sum(yh * yh, axis=-1, keepdims=True) + NORM_EPS)
                if part == 0:
                    inv = inv * (HEAD_DIM ** -0.5)
                heads.append(yh * inv)
            y = jnp.concatenate(heads, axis=1)
        out_ref[...] = y.astype(out_ref.dtype)

    raw = gate_ref[...]
    gp = gp_ref[...]
    col = lax.broadcasted_iota(jnp.int32, raw.shape, 1)
    beta = _sigmoid(raw)
    z = raw + gp[0:1, :]
    softplus = jnp.maximum(z, 0.0) + jnp.log(1.0 + jnp.exp(-jnp.abs(z)))
    logdec = -jnp.exp(gp[1:2, :]) * softplus
    is_dec = (col >= N_GATE_COLS) & (col < 2 * N_GATE_COLS)
    gsrc = jnp.where(is_dec, logdec, 0.0)
    r = lax.broadcasted_iota(jnp.int32, (tt, tt), 0)
    c = lax.broadcasted_iota(jnp.int32, (tt, tt), 1)
    same = (r // GDN_CHUNK) == (c // GDN_CHUNK)
    p_lo = jnp.where(same & (c <= r), 1.0, 0.0).astype(F32)
    p_up = jnp.where(same & (c >= r), 1.0, 0.0).astype(F32)
    cum_lo = jnp.dot(p_lo, gsrc, precision=lax.Precision.HIGHEST, preferred_element_type=F32)
    cum_up = jnp.dot(p_up, gsrc, precision=lax.Precision.HIGHEST, preferred_element_type=F32)
    total = cum_lo + cum_up - gsrc
    fwd_col = col < N_GATE_COLS + GDN_HEADS
    gc = jnp.where(fwd_col, cum_lo, cum_up)
    tot_shift = pltpu.roll(total, N_GATE_COLS, 1)
    gcol = jnp.where(col < N_GATE_COLS, beta,
                     jnp.where(col < 2 * N_GATE_COLS, gc,
                               jnp.where(col < 3 * N_GATE_COLS, tot_shift, 0.0)))
    gcol_ref[...] = gcol
    grow_ref[...] = gcol.T


def _gdnprep_call(pm, gates, conv_w8, gate_params):
    bsz, seq, _ = pm.shape
    tt = 256
    hb = tt // BF16_ROWS
    n_halo = seq // BF16_ROWS
    blocks = (_nbytes((tt, GDN_QKV), BF16) + 2 * _nbytes((BF16_ROWS, GDN_QKV), BF16) + _nbytes((tt, LANES), F32)
              + _nbytes((SUBLANES, GDN_QKV), F32) + 3 * _nbytes((tt, GDN_WIDTH), BF16) + 2 * _nbytes((tt, LANES), F32))
    temps = 8 * _nbytes((tt, GDN_WIDTH), F32) + 4 * _nbytes((tt, tt), F32)
    return pl.pallas_call(
        functools.partial(_gdnprep_kernel, tt=tt),
        grid=(bsz, seq // tt),
        in_specs=[
            pl.BlockSpec((None, tt, GDN_QKV), lambda b, i: (b, i, 0)),
            pl.BlockSpec((None, BF16_ROWS, GDN_QKV), lambda b, i: (b, jnp.maximum(i * hb - 1, 0), 0)),
            pl.BlockSpec((None, BF16_ROWS, GDN_QKV), lambda b, i: (b, jnp.minimum((i + 1) * hb, n_halo - 1), 0)),
            pl.BlockSpec((None, tt, LANES), lambda b, i: (b, i, 0)),
            pl.BlockSpec((SUBLANES, GDN_QKV), lambda b, i: (0, 0)),
            pl.BlockSpec((SUBLANES, LANES), lambda b, i: (0, 0)),
        ],
        out_specs=[
            pl.BlockSpec((None, tt, GDN_WIDTH), lambda b, i: (b, i, 0)),
            pl.BlockSpec((None, tt, GDN_WIDTH), lambda b, i: (b, i, 0)),
            pl.BlockSpec((None, tt, GDN_WIDTH), lambda b, i: (b, i, 0)),
            pl.BlockSpec((None, tt, LANES), lambda b, i: (b, i, 0)),
            pl.BlockSpec((None, LANES, tt), lambda b, i: (b, 0, i)),
        ],
        out_shape=[
            jax.ShapeDtypeStruct((bsz, seq, GDN_WIDTH), BF16),
            jax.ShapeDtypeStruct((bsz, seq, GDN_WIDTH), BF16),
            jax.ShapeDtypeStruct((bsz, seq, GDN_WIDTH), BF16),
            jax.ShapeDtypeStruct((bsz, seq, LANES), F32),
            jax.ShapeDtypeStruct((bsz, LANES, seq), F32),
        ],
        scratch_shapes=[pltpu.VMEM((tt + 2 * SUBLANES, GDN_WIDTH), F32)],
        compiler_params=pltpu.CompilerParams(
            dimension_semantics=("parallel", "parallel"),
            vmem_limit_bytes=_vmem_limit(blocks, _nbytes((tt + 2 * SUBLANES, GDN_WIDTH), F32), temps)),
        name="gdnprep",
    )(pm, pm, pm, gates, conv_w8, gate_params)


def _gdn_kernel(qf_ref, kf_ref, vf_ref, gcf_ref, grf_ref,
                qb_ref, kb_ref, vb_ref, gcb_ref, grb_ref,
                of_ref, ob_ref, state_ref):
    @pl.when(pl.program_id(1) == 0)
    def _():
        state_ref[...] = jnp.zeros_like(state_ref)

    n = GDN_CHUNK
    r = lax.broadcasted_iota(jnp.int32, (n, n), 0)
    c = lax.broadcasted_iota(jnp.int32, (n, n), 1)
    eye = jnp.where(r == c, 1.0, 0.0).astype(F32)

    units = []
    for d, (q_ref, k_ref, v_ref, gc_ref, gr_ref, o_ref) in enumerate((
            (qf_ref, kf_ref, vf_ref, gcf_ref, grf_ref, of_ref),
            (qb_ref, kb_ref, vb_ref, gcb_ref, grb_ref, ob_ref))):
        gcol = gc_ref[...]
        grow = gr_ref[...]
        reverse = d == 1
        incl = (r <= c) if reverse else (r >= c)
        strict = (r < c) if reverse else (r > c)
        for h in range(GDN_HEADS):
            lanes = slice(h * HEAD_DIM, (h + 1) * HEAD_DIM)
            j = d * GDN_HEADS + h
            beta = gcol[:, j:j + 1]
            gcc = gcol[:, N_GATE_COLS + j:N_GATE_COLS + j + 1]
            gtc = gcol[:, 2 * N_GATE_COLS + j:2 * N_GATE_COLS + j + 1]
            gcr = grow[N_GATE_COLS + j:N_GATE_COLS + j + 1, :]
            gtr = grow[2 * N_GATE_COLS + j:2 * N_GATE_COLS + j + 1, :]
            q = q_ref[:, lanes]
            k = k_ref[:, lanes]
            kf = k.astype(F32)
            kb = kf * beta
            egc = jnp.exp(gcc)
            units.append(dict(
                d=d, h=h, lanes=lanes, o_ref=o_ref, strict=strict, q=q, k=k,
                decay=jnp.where(incl, jnp.exp(jnp.where(incl, gcc - gcr, 0.0)), 0.0),
                kbb=kb.astype(BF16),
                vb=(v_ref[:, lanes].astype(F32) * beta).astype(BF16),
                kbg=(kb * egc).astype(BF16),
                qg=(q.astype(F32) * egc).astype(BF16),
                kdec_t=(kf * jnp.exp(gtc - gcc)).T.astype(BF16),
                sdec=jnp.exp(gtr)))

    for u in units:
        aq = _dot_nt(jnp.concatenate([u["kbb"], u["q"]], axis=0), u["k"])
        low = jnp.where(u["strict"], aq[:n] * u["decay"], 0.0)
        u["qk"] = (aq[n:] * u["decay"]).astype(BF16)
        u["tinv"] = eye - low
        u["lb"] = low.astype(BF16)

    for u in units:
        u["power"] = _dot(u["lb"], u["lb"])
    n_factors = (n - 1).bit_length() - 1
    for it in range(n_factors):
        for u in units:
            pb = u["power"].astype(BF16)
            if it + 1 < n_factors:
                both = _dot(jnp.concatenate([u["tinv"].astype(BF16), pb], axis=0), pb)
                u["tinv"] = u["tinv"] + both[:n]
                u["power"] = both[n:]
            else:
                u["tinv"] = u["tinv"] + _dot(u["tinv"].astype(BF16), pb)

    for u in units:
        u["uw"] = _dot(u["tinv"].astype(BF16), jnp.concatenate([u["vb"], u["kbg"]], axis=1))
    for u in units:
        u["state"] = state_ref[u["d"], u["h"]]
        w = u["uw"][:, HEAD_DIM:].astype(BF16)
        u["ws"] = _dot(jnp.concatenate([w, u["qg"]], axis=0), u["state"].astype(BF16))
    for u in units:
        v_new = (u["uw"][:, :HEAD_DIM] - u["ws"][:n]).astype(BF16)
        u["os"] = _dot(jnp.concatenate([u["qk"], u["kdec_t"]], axis=0), v_new)
    for u in units:
        u["o_ref"][:, u["lanes"]] = u["ws"][n:] + u["os"][:n]
        state_ref[u["d"], u["h"]] = u["state"] * u["sdec"] + u["os"][n:]


def _gdn_call(q, k, v, gcol, grow):
    bsz, seq, _ = q.shape
    tb = GDN_CHUNK
    nb = seq // tb
    qkv_spec_f = pl.BlockSpec((None, tb, GDN_WIDTH), lambda b, i: (b, i, 0))
    qkv_spec_b = pl.BlockSpec((None, tb, GDN_WIDTH), lambda b, i: (b, nb - 1 - i, 0))
    gc_spec_f = pl.BlockSpec((None, tb, LANES), lambda b, i: (b, i, 0))
    gc_spec_b = pl.BlockSpec((None, tb, LANES), lambda b, i: (b, nb - 1 - i, 0))
    gr_spec_f = pl.BlockSpec((None, LANES, tb), lambda b, i: (b, 0, i))
    gr_spec_b = pl.BlockSpec((None, LANES, tb), lambda b, i: (b, 0, nb - 1 - i))
    blocks = 2 * (3 * _nbytes((tb, GDN_WIDTH), BF16) + 2 * _nbytes((tb, LANES), F32) + _nbytes((tb, GDN_WIDTH), F32))
    scratch = _nbytes((2, GDN_HEADS, HEAD_DIM, HEAD_DIM), F32)
    temps = 2 * GDN_HEADS * 24 * _nbytes((GDN_CHUNK, LANES), F32)
    return pl.pallas_call(
        _gdn_kernel,
        grid=(bsz, nb),
        in_specs=[qkv_spec_f, qkv_spec_f, qkv_spec_f, gc_spec_f, gr_spec_f,
                  qkv_spec_b, qkv_spec_b, qkv_spec_b, gc_spec_b, gr_spec_b],
        out_specs=[qkv_spec_f, qkv_spec_b],
        out_shape=[jax.ShapeDtypeStruct((bsz, seq, GDN_WIDTH), F32),
                   jax.ShapeDtypeStruct((bsz, seq, GDN_WIDTH), F32)],
        scratch_shapes=[pltpu.VMEM((2, GDN_HEADS, HEAD_DIM, HEAD_DIM), F32)],
        compiler_params=pltpu.CompilerParams(
            dimension_semantics=("parallel", "arbitrary"),
            vmem_limit_bytes=_vmem_limit(blocks, scratch, temps)),
        name="gdn",
    )(q, k, v, gcol, grow, q, k, v, gcol, grow)


def _rope_tables(seq):
    rows = seq // GRID_W
    half = HEAD_DIM // 2
    inv_freq = ROPE_THETA ** (-jnp.arange(0, half, 2, dtype=F32) / half)
    ang_r = jnp.arange(rows, dtype=F32)[:, None] * inv_freq
    ang_c = jnp.arange(GRID_W, dtype=F32)[:, None] * inv_freq

    def table(fn):
        tr = jnp.broadcast_to(fn(ang_r)[:, None, :], (rows, GRID_W, half // 2))
        tc = jnp.broadcast_to(fn(ang_c)[None, :, :], (rows, GRID_W, half // 2))
        return jnp.concatenate([tr, tr, tc, tc], -1).reshape(seq, HEAD_DIM)

    cos, sin = table(jnp.cos), table(jnp.sin)
    lane = jnp.arange(HEAD_DIM)
    first = (lane // (HEAD_DIM // 4)) % 2 == 0
    sin_up = jnp.where(first, -sin, 0.0)
    sin_dn = jnp.where(first, 0.0, sin)
    return cos, sin_up, sin_dn


def _norm_rope(x, w, cos, sin_up, sin_dn, scale):
    xn = x * lax.rsqrt(jnp.mean(x * x, axis=-1, keepdims=True) + NORM_EPS) * w
    quarter = HEAD_DIM // 4
    up = pltpu.roll(xn, HEAD_DIM - quarter, 1)
    dn = pltpu.roll(xn, quarter, 1)
    y = xn * cos + up * sin_up + dn * sin_dn
    return y * scale if scale != 1.0 else y


ATT_TK = 1024
FLASH_SUB = 256
VT_ROWS = HEAD_DIM + BF16_ROWS
LOG2_E = 1.4426950408889634


def _flash_kernel(qt_ref, qtn_ref, k_ref, vt_ref, o_ref, m_ref, acc_ref, qc_ref, qn_ref,
                  sa_ref, sb_ref, mca_ref, mcb_ref, *, tq, tk, ts, n_k):
    m_ref[...] = jnp.full_like(m_ref, -jnp.inf)
    acc_ref[...] = jnp.zeros_like(acc_ref)
    for g in range(ATT_GROUP):
        qc_ref[:, g * tq:(g + 1) * tq] = qt_ref[g * HEAD_DIM:(g + 1) * HEAD_DIM, :]
        qn_ref[:, g * tq:(g + 1) * tq] = qtn_ref[g * HEAD_DIM:(g + 1) * HEAD_DIM, :]

    n_sub = tk // ts
    bufs = ((sa_ref, mca_ref), (sb_ref, mcb_ref))

    def scores(j, q_ref, s_ref, i):
        start = pl.multiple_of(j * tk + i * ts, ts)
        st = _dot(k_ref[pl.ds(start, ts), :], q_ref[...])
        s_ref[i * ts:(i + 1) * ts, :] = st
        return jnp.max(st, axis=0, keepdims=True)

    def weigh(j, s_ref, i, m_new):
        p = jnp.exp2(s_ref[i * ts:(i + 1) * ts, :] - m_new)
        return _dot(vt_ref[j, :, i * ts:(i + 1) * ts], p.astype(BF16))

    def step(j, parity, j_next, q_next):
        s_cur, mc_cur = bufs[parity]
        s_nxt, mc_nxt = bufs[1 - parity]
        m_prev = m_ref[...]
        m_new = jnp.maximum(m_prev, mc_cur[...])
        alpha = jnp.exp2(m_prev - m_new)
        mc_next, pv = None, None
        for i in range(n_sub):
            mx = scores(j_next, q_next, s_nxt, i)
            mc_next = mx if mc_next is None else jnp.maximum(mc_next, mx)
            pvi = weigh(j, s_cur, i, m_new)
            pv = pvi if pv is None else pv + pvi
        mc_nxt[...] = mc_next
        acc_ref[...] = alpha * acc_ref[...] + pv
        m_ref[...] = m_new

    @pl.when(pl.program_id(2) == 0)
    def _():
        mc0 = None
        for i in range(n_sub):
            mx = scores(0, qc_ref, sa_ref, i)
            mc0 = mx if mc0 is None else jnp.maximum(mc0, mx)
        mca_ref[...] = mc0

    def body(jj, carry):
        step(2 * jj, 0, 2 * jj + 1, qc_ref)
        step(2 * jj + 1, 1, 2 * jj + 2, qc_ref)
        return carry

    lax.fori_loop(0, n_k // 2 - 1, body, 0)
    step(n_k - 2, 0, n_k - 1, qc_ref)
    step(n_k - 1, 1, 0, qn_ref)
    for g in range(ATT_GROUP):
        lanes = slice(g * tq, (g + 1) * tq)
        out_t = acc_ref[:HEAD_DIM, lanes] / acc_ref[HEAD_DIM:HEAD_DIM + 1, lanes]
        o_ref[:, g * HEAD_DIM:(g + 1) * HEAD_DIM] = out_t.T.astype(o_ref.dtype)


def _flash_call(qt, kr, vt):
    bsz, _, seq = qt.shape
    tq, tk = 256, ATT_TK
    gw = ATT_GROUP * HEAD_DIM
    nq = ATT_GROUP * tq
    n_q, n_k = seq // tq, seq // tk
    assert n_k % 2 == 0, "the score buffers alternate by key-tile parity across query tiles"
    blocks = 3 * _nbytes((gw, tq), BF16) + _nbytes((seq, HEAD_DIM), BF16) + _nbytes((seq, VT_ROWS), BF16)
    scratch = (3 * _nbytes((SUBLANES, nq), F32) + _nbytes((VT_ROWS, nq), F32) + 2 * _nbytes((HEAD_DIM, nq), BF16)
               + 2 * _nbytes((tk, nq), F32))
    temps = 2 * _nbytes((tk, nq), F32)
    return pl.pallas_call(
        functools.partial(_flash_kernel, tq=tq, tk=tk, ts=FLASH_SUB, n_k=n_k),
        grid=(bsz, ATT_KV_HEADS, n_q),
        in_specs=[
            pl.BlockSpec((None, gw, tq), lambda b, h, i: (b, h, i)),
            pl.BlockSpec((None, gw, tq), lambda b, h, i: (b, h, jnp.minimum(i + 1, n_q - 1))),
            pl.BlockSpec((None, seq, HEAD_DIM), lambda b, h, i: (b, 0, h)),
            pl.BlockSpec((None, None, n_k, VT_ROWS, tk), lambda b, h, i: (b, h, 0, 0, 0)),
        ],
        out_specs=pl.BlockSpec((None, tq, gw), lambda b, h, i: (b, i, h)),
        out_shape=jax.ShapeDtypeStruct((bsz, seq, ATT_WIDTH), BF16),
        scratch_shapes=[pltpu.VMEM((1, nq), F32),
                        pltpu.VMEM((VT_ROWS, nq), F32), pltpu.VMEM((HEAD_DIM, nq), BF16),
                        pltpu.VMEM((HEAD_DIM, nq), BF16),
                        pltpu.VMEM((tk, nq), F32), pltpu.VMEM((tk, nq), F32),
                        pltpu.VMEM((1, nq), F32), pltpu.VMEM((1, nq), F32)],
        compiler_params=pltpu.CompilerParams(
            dimension_semantics=("parallel", "parallel", "arbitrary"),
            vmem_limit_bytes=_vmem_limit(blocks, scratch, temps)),
        name="flash",
    )(qt, qt, kr, vt)


def _outproj_kernel(of_ref, ob_ref, z_ref, oa_ref, x_ref, mod_ref, w_ref, nw_ref, lg_ref, lb_ref, o_ref):
    nw = nw_ref[...]
    heads = []
    for h in range(GDN_HEADS):
        lanes = slice(h * HEAD_DIM, (h + 1) * HEAD_DIM)
        o = of_ref[:, lanes] + ob_ref[:, lanes]
        on = o * lax.rsqrt(jnp.mean(o * o, axis=-1, keepdims=True) + NORM_EPS) * nw
        heads.append((on * _silu(z_ref[:, lanes].astype(F32))).astype(BF16))
    og = jnp.concatenate(heads, axis=1)
    mix = _dot(og, w_ref[:GDN_WIDTH, :]) + _dot(oa_ref[...], w_ref[GDN_WIDTH:, :])
    m = mod_ref[...]
    y = DEEPNORM_ALPHA * x_ref[...] + m[MOD_G_M:MOD_G_M + 1, :] * mix
    o_ref[...] = _layer_norm_rows(y, lg_ref[...], lb_ref[...])


def _outproj_call(o_f, o_b, pm2, o_att, x2, mod, w_out, gdn_norm_w, ln_g, ln_b, seq):
    m = x2.shape[0]
    tm = 256
    blocks = (2 * _nbytes((tm, GDN_WIDTH), F32) + 2 * _nbytes((tm, GDN_WIDTH), BF16)
              + 2 * _nbytes((tm, D_MODEL), F32) + _nbytes((MOD_ROWS, D_MODEL), F32)
              + _nbytes((D_MODEL, D_MODEL), BF16))
    vec = pl.BlockSpec((1, D_MODEL), lambda i: (0, 0))
    return pl.pallas_call(
        _outproj_kernel,
        grid=(m // tm,),
        in_specs=[
            pl.BlockSpec((tm, GDN_WIDTH), lambda i: (i, 0)),
            pl.BlockSpec((tm, GDN_WIDTH), lambda i: (i, 0)),
            pl.BlockSpec((tm, GDN_WIDTH), lambda i: (i, PM_Z // GDN_WIDTH)),
            pl.BlockSpec((tm, ATT_WIDTH), lambda i: (i, 0)),
            pl.BlockSpec((tm, D_MODEL), lambda i: (i, 0)),
            pl.BlockSpec((None, MOD_ROWS, D_MODEL), lambda i: ((i * tm) // seq, 0, 0)),
            pl.BlockSpec((D_MODEL, D_MODEL), lambda i: (0, 0)),
            pl.BlockSpec((1, HEAD_DIM), lambda i: (0, 0)),
            vec, vec,
        ],
        out_specs=pl.BlockSpec((tm, D_MODEL), lambda i: (i, 0)),
        out_shape=jax.ShapeDtypeStruct((m, D_MODEL), F32),
        compiler_params=pltpu.CompilerParams(
            dimension_semantics=("parallel",),
            vmem_limit_bytes=_vmem_limit(blocks, temp_bytes=4 * _nbytes((tm, D_MODEL), F32))),
        name="outproj",
    )(o_f, o_b, pm2, o_att, x2, mod, w_out, gdn_norm_w.reshape(1, HEAD_DIM),
      ln_g.reshape(1, D_MODEL), ln_b.reshape(1, D_MODEL))


POOL_REACH = SUBLANES
assert max(POOL_WINDOWS) // 2 <= POOL_REACH


def _pool_kernel(x_ref, xp_ref, xn_ref, mod_ref, pw_ref, ps_ref, lg_ref, lb_ref, o_ref, *, tt, seq):
    t0 = pl.program_id(1) * tt
    m = mod_ref[...]
    sc = 1.0 + m[MOD_SC_M:MOD_SC_M + 1, :]
    sh = m[MOD_SH_M:MOD_SH_M + 1, :]
    x = x_ref[...]
    h_main = x * sc + sh
    h_prev = xp_ref[...] * sc + sh
    h_next = xn_ref[...] * sc + sh
    r = lax.broadcasted_iota(jnp.int32, (tt, tt), 0)
    c = lax.broadcasted_iota(jnp.int32, (tt, tt), 1)
    pos = t0 + lax.broadcasted_iota(jnp.int32, (tt, 1), 0)
    r8 = lax.broadcasted_iota(jnp.int32, (POOL_REACH, 1), 0)
    outs = []
    for gi, win in enumerate(POOL_WINDOWS):
        back, fwd = win // 2, win - 1 - win // 2
        lanes = slice(gi * POOL_GROUP, (gi + 1) * POOL_GROUP)
        band = jnp.where((c >= r - back) & (c <= r + fwd), 1.0, 0.0).astype(BF16)
        hm = h_main[:, lanes]
        wsum = _dot(band, hm.astype(BF16))
        top = jnp.zeros((POOL_REACH, POOL_GROUP), F32)
        bot = jnp.zeros((POOL_REACH, POOL_GROUP), F32)
        for k in range(POOL_REACH):
            if POOL_REACH - k <= back:
                use = (r8 <= k - POOL_REACH + back) & (t0 - POOL_REACH + k >= 0)
                top = top + jnp.where(use, 1.0, 0.0) * h_prev[k:k + 1, lanes]
            if k < fwd:
                use = (r8 >= k + POOL_REACH - fwd) & (t0 + tt + k < seq)
                bot = bot + jnp.where(use, 1.0, 0.0) * h_next[k:k + 1, lanes]
        wsum = jnp.concatenate([wsum[:POOL_REACH] + top, wsum[POOL_REACH:tt - POOL_REACH],
                                wsum[tt - POOL_REACH:] + bot], axis=0)
        cnt = (jnp.minimum(pos + fwd, seq - 1) - jnp.maximum(pos - back, 0) + 1).astype(F32)
        diff = wsum / cnt - hm
        outs.append(_dot(diff.astype(BF16), pw_ref[gi]))
    mix = jnp.concatenate(outs, axis=1) * ps_ref[...]
    y = DEEPNORM_ALPHA * x + m[MOD_G_M:MOD_G_M + 1, :] * mix
    o_ref[...] = _layer_norm_rows(y, lg_ref[...], lb_ref[...])


def _pool_call(x, mod, pool_w, pool_scale, ln_g, ln_b):
    bsz, seq, _ = x.shape
    tt = 256
    hb = tt // POOL_REACH
    n_halo = seq // POOL_REACH
    vec = pl.BlockSpec((1, D_MODEL), lambda b, i: (0, 0))
    blocks = (2 * _nbytes((tt, D_MODEL), F32) + 2 * _nbytes((POOL_REACH, D_MODEL), F32)
              + _nbytes((MOD_ROWS, D_MODEL), F32) + _nbytes(pool_w.shape, BF16))
    temps = 5 * _nbytes((tt, D_MODEL), F32)
    return pl.pallas_call(
        functools.partial(_pool_kernel, tt=tt, seq=seq),
        grid=(bsz, seq // tt),
        in_specs=[
            pl.BlockSpec((None, tt, D_MODEL), lambda b, i: (b, i, 0)),
            pl.BlockSpec((None, POOL_REACH, D_MODEL), lambda b, i: (b, jnp.maximum(i * hb - 1, 0), 0)),
            pl.BlockSpec((None, POOL_REACH, D_MODEL), lambda b, i: (b, jnp.minimum((i + 1) * hb, n_halo - 1), 0)),
            pl.BlockSpec((None, MOD_ROWS, D_MODEL), lambda b, i: (b, 0, 0)),
            pl.BlockSpec(pool_w.shape, lambda b, i: (0, 0, 0)),
            vec, vec, vec,
        ],
        out_specs=pl.BlockSpec((None, tt, D_MODEL), lambda b, i: (b, i, 0)),
        out_shape=jax.ShapeDtypeStruct((bsz, seq, D_MODEL), F32),
        compiler_params=pltpu.CompilerParams(
            dimension_semantics=("parallel", "parallel"),
            vmem_limit_bytes=_vmem_limit(blocks, temp_bytes=temps)),
        name="pool",
    )(x, x, x, mod, pool_w, pool_scale.reshape(1, D_MODEL), ln_g.reshape(1, D_MODEL), ln_b.reshape(1, D_MODEL))


def _mlp_kernel(x_ref, mod_ref, w1_ref, w2_ref, lg_ref, lb_ref, o_ref, h_ref, acc_ref):
    j = pl.program_id(1)

    @pl.when(j == 0)
    def _():
        m = mod_ref[...]
        h = x_ref[...] * (1.0 + m[MOD_SC_F:MOD_SC_F + 1, :]) + m[MOD_SH_F:MOD_SH_F + 1, :]
        h_ref[...] = h.astype(BF16)
        acc_ref[...] = jnp.zeros_like(acc_ref)

    u = jnp.maximum(_dot(h_ref[...], w1_ref[...]), 0.0)
    acc_ref[...] += _dot((u * u).astype(BF16), w2_ref[...])

    @pl.when(j == pl.num_programs(1) - 1)
    def _():
        m = mod_ref[...]
        y = DEEPNORM_ALPHA * x_ref[...] + m[MOD_G_F:MOD_G_F + 1, :] * acc_ref[...]
        o_ref[...] = _layer_norm_rows(y, lg_ref[...], lb_ref[...])


def _mlp_call(x2, mod, w1, w2, layer, ln_g, ln_b, seq):
    m = x2.shape[0]
    tm, tf = 512, 1024
    vec = pl.BlockSpec((1, D_MODEL), lambda i, j: (0, 0))
    blocks = (2 * _nbytes((tm, D_MODEL), F32) + _nbytes((MOD_ROWS, D_MODEL), F32)
              + 2 * _nbytes((D_MODEL, tf), BF16))
    scratch = _nbytes((tm, D_MODEL), BF16) + _nbytes((tm, D_MODEL), F32)
    temps = 2 * _nbytes((tm, tf), F32) + _nbytes((tm, D_MODEL), F32)
    return pl.pallas_call(
        _mlp_kernel,
        grid=(m // tm, D_FF // tf),
        in_specs=[
            pl.BlockSpec((tm, D_MODEL), lambda i, j: (i, 0)),
            pl.BlockSpec((None, MOD_ROWS, D_MODEL), lambda i, j: ((i * tm) // seq, 0, 0)),
            pl.BlockSpec((None, D_MODEL, tf), lambda i, j: (layer, 0, j)),
            pl.BlockSpec((None, tf, D_MODEL), lambda i, j: (layer, j, 0)),
            vec, vec,
        ],
        out_specs=pl.BlockSpec((tm, D_MODEL), lambda i, j: (i, 0)),
        out_shape=jax.ShapeDtypeStruct((m, D_MODEL), F32),
        scratch_shapes=[pltpu.VMEM((tm, D_MODEL), BF16), pltpu.VMEM((tm, D_MODEL), F32)],
        compiler_params=pltpu.CompilerParams(
            dimension_semantics=("parallel", "arbitrary"),
            vmem_limit_bytes=_vmem_limit(blocks, scratch, temps)),
        name="mlp",
    )(x2, mod, w1, w2, ln_g.reshape(1, D_MODEL), ln_b.reshape(1, D_MODEL))


def _mod_table(cond_rows):
    bsz = cond_rows.shape[0]
    t = cond_rows.reshape(bsz, 6, D_MODEL)
    return jnp.pad(t, ((0, 0), (0, MOD_ROWS - 6), (0, 0)))


def _pack_weights(w_in, conv_w, a_log, dt_bias, w_out, pool_w, mlp_w1, mlp_w2):
    wi = w_in[0]
    w_main = jnp.concatenate(
        [wi[:, :OFF_BETA], wi[:, OFF_AQ:]], axis=1).astype(BF16)
    w_gate = jnp.pad(wi[:, OFF_BETA:OFF_AQ], ((0, 0), (0, LANES - 2 * N_GATE_COLS))).astype(BF16)
    conv_w8 = jnp.pad(conv_w[0], ((0, SUBLANES - CONV_W), (0, 0)))
    gate_params = jnp.zeros((SUBLANES, LANES), F32)
    gate_params = gate_params.at[0, N_GATE_COLS:2 * N_GATE_COLS].set(dt_bias[0].reshape(-1))
    gate_params = gate_params.at[1, N_GATE_COLS:2 * N_GATE_COLS].set(a_log[0].reshape(-1))
    return dict(w_main=w_main, w_gate=w_gate, conv_w8=conv_w8, gate_params=gate_params,
                w_out=w_out[0].astype(BF16), pool_w=pool_w[0].astype(BF16),
                w1=mlp_w1.astype(BF16), w2=mlp_w2.astype(BF16))


def _trunk(x, cond, pk, gdn_norm_w, q_norm_w, k_norm_w, pool_scale, ln_g, ln_b):
    bsz, seq, _ = x.shape
    m = bsz * seq
    mod0 = _mod_table(cond[0])
    mod1 = _mod_table(cond[1])
    x2 = x.reshape(m, D_MODEL)

    cos, sin_up, sin_dn = _rope_tables(seq)
    pm2, gates2, qt, kr2, vt = _inproj_call(x2, mod0, pk["w_main"], pk["w_gate"], cos, sin_up, sin_dn,
                                            q_norm_w[0], k_norm_w[0], bsz, seq)
    pm = pm2.reshape(bsz, seq, PM_WIDTH)
    gq, gk, gv, gcol, grow = _gdnprep_call(pm, gates2.reshape(bsz, seq, LANES), pk["conv_w8"], pk["gate_params"])
    o_f, o_b = _gdn_call(gq, gk, gv, gcol, grow)
    o_att = _flash_call(qt, kr2.reshape(bsz, seq, ATT_KV_WIDTH), vt)
    x2 = _outproj_call(o_f.reshape(m, GDN_WIDTH), o_b.reshape(m, GDN_WIDTH), pm2, o_att.reshape(m, ATT_WIDTH),
                       x2, mod0, pk["w_out"], gdn_norm_w[0], ln_g[0, 0], ln_b[0, 0], seq)
    x2 = _mlp_call(x2, mod0, pk["w1"], pk["w2"], 0, ln_g[0, 1], ln_b[0, 1], seq)

    x3 = _pool_call(x2.reshape(bsz, seq, D_MODEL), mod1, pk["pool_w"], pool_scale[0], ln_g[1, 0], ln_b[1, 0])
    x2 = _mlp_call(x3.reshape(m, D_MODEL), mod1, pk["w1"], pk["w2"], 1, ln_g[1, 1], ln_b[1, 1], seq)
    return x2.reshape(bsz, seq, D_MODEL)


def kernel(x_prompt, x_sample, c_prompt, c_sample, w_in, conv_w, a_log, dt_bias, gdn_norm_w, q_norm_w,
           k_norm_w, w_out, pool_w, pool_scale, mlp_w1, mlp_w2, ada_w, ada_b, ln_g, ln_b):
    bp, bs = c_prompt.shape[0], c_sample.shape[0]
    c_all = jnp.concatenate([c_prompt, c_sample], axis=0)
    c_all = jnp.pad(c_all, ((0, (-c_all.shape[0]) % SUBLANES), (0, 0)))
    cond = _ada_call(c_all, ada_w, ada_b)
    pk = _pack_weights(w_in, conv_w, a_log, dt_bias, w_out, pool_w, mlp_w1, mlp_w2)
    y_prompt = _trunk(x_prompt, cond[:, :bp], pk, gdn_norm_w, q_norm_w, k_norm_w, pool_scale, ln_g, ln_b)
    y_sample = _trunk(x_sample, cond[:, bp:bp + bs], pk, gdn_norm_w, q_norm_w, k_norm_w, pool_scale, ln_g, ln_b)
    return (y_prompt, y_sample)
```

```python
import functools

import jax
import jax.numpy as jnp
from jax import lax
from jax.experimental import pallas as pl
from jax.experimental.pallas import tpu as pltpu

F32 = jnp.float32
BF16 = jnp.bfloat16

D_MODEL = 2048
DEPTH = 2
GRID_W = 64
HEAD_DIM = 128
GDN_HEADS = 8
ATT_HEADS = 8
ATT_KV_HEADS = 2
ATT_GROUP = ATT_HEADS // ATT_KV_HEADS
GDN_WIDTH = GDN_HEADS * HEAD_DIM
ATT_WIDTH = ATT_HEADS * HEAD_DIM
ATT_KV_WIDTH = ATT_KV_HEADS * HEAD_DIM
CONV_W = 5
ROPE_THETA = 10000.0
POOL_WINDOWS = (2, 4, 8, 16)
N_POOL_GROUPS = 4
POOL_GROUP = D_MODEL // N_POOL_GROUPS
D_FF = 4 * D_MODEL
DEEPNORM_ALPHA = (2 * DEPTH) ** 0.25
NORM_EPS = 1e-6
LN_EPS = 1e-5

GDN_QKV = 3 * GDN_WIDTH
OFF_Z = GDN_QKV
OFF_BETA = OFF_Z + GDN_WIDTH
OFF_A = OFF_BETA + 2 * GDN_HEADS
OFF_AQ = OFF_A + 2 * GDN_HEADS
OFF_AK = OFF_AQ + ATT_WIDTH
OFF_AV = OFF_AK + ATT_KV_WIDTH
D_IN = OFF_AV + ATT_KV_WIDTH

PM_QKV = 0
PM_Z = GDN_QKV
PM_AQ = PM_Z + GDN_WIDTH
PM_AK = PM_AQ + ATT_WIDTH
PM_AV = PM_AK + ATT_KV_WIDTH
PM_WIDTH = PM_AV + ATT_KV_WIDTH

LANES = 128
SUBLANES = 8
BF16_ROWS = 16
V7X_VMEM_BYTES = 64 * 1024 * 1024

GDN_CHUNK = 128
N_GATE_COLS = 2 * GDN_HEADS

MOD_SH_M, MOD_SC_M, MOD_G_M, MOD_SH_F, MOD_SC_F, MOD_G_F = range(6)
MOD_ROWS = 8


def _vmem_limit(block_bytes, scratch_bytes=0, temp_bytes=0):
    need = 2 * block_bytes + scratch_bytes + temp_bytes
    return int(min(need + need // 4, V7X_VMEM_BYTES - 8 * 1024 * 1024))


def _nbytes(shape, dtype):
    n = 1
    for s in shape:
        n *= s
    return n * jnp.dtype(dtype).itemsize


def _sigmoid(x):
    return 1.0 / (1.0 + jnp.exp(-x))


def _silu(x):
    return x * _sigmoid(x)


def _layer_norm_rows(y, g, b):
    mu = jnp.mean(y, axis=-1, keepdims=True)
    yc = y - mu
    var = jnp.mean(yc * yc, axis=-1, keepdims=True)
    return yc * lax.rsqrt(var + LN_EPS) * g + b


def _dot(a, b):
    return jnp.dot(a, b, preferred_element_type=F32)


def _dot_nt(a, b):
    return lax.dot_general(a, b, (((1,), (1,)), ((), ())), preferred_element_type=F32)


def _ada_kernel(c_ref, w_ref, b_ref, o_ref):
    c = c_ref[...]
    s = _silu(c).astype(BF16)
    o_ref[...] = _dot(s, w_ref[...].astype(BF16)) + b_ref[...]


def _ada_call(c_all, ada_w, ada_b):
    rows = c_all.shape[0]
    n = ada_w.shape[-1]
    tn = 1024
    blocks = _nbytes((D_MODEL, tn), F32) + _nbytes((rows, D_MODEL), F32) + 2 * _nbytes((rows, tn), F32)
    return pl.pallas_call(
        _ada_kernel,
        grid=(DEPTH, n // tn),
        in_specs=[
            pl.BlockSpec((rows, D_MODEL), lambda l, j: (0, 0)),
            pl.BlockSpec((None, D_MODEL, tn), lambda l, j: (l, 0, j)),
            pl.BlockSpec((None, 1, tn), lambda l, j: (l, 0, j)),
        ],
        out_specs=pl.BlockSpec((None, rows, tn), lambda l, j: (l, 0, j)),
        out_shape=jax.ShapeDtypeStruct((DEPTH, rows, n), F32),
        compiler_params=pltpu.CompilerParams(
            dimension_semantics=("parallel", "parallel"),
            vmem_limit_bytes=_vmem_limit(blocks, temp_bytes=_nbytes((D_MODEL, tn), BF16))),
        name="ada",
    )(c_all, ada_w, ada_b.reshape(DEPTH, 1, n))


INPROJ_TN = PM_WIDTH // 2
ATT_COL0 = PM_AQ - INPROJ_TN
assert PM_AQ >= INPROJ_TN and ATT_COL0 % LANES == 0


def _inproj_kernel(x_ref, mod_ref, w_ref, wg_ref, cos_ref, su_ref, sd_ref, qw_ref, kw_ref,
                   o_ref, g_ref, qt_ref, k_ref, vt_ref, h_ref):
    j = pl.program_id(1)

    @pl.when(j == 0)
    def _():
        m = mod_ref[...]
        h = x_ref[...] * (1.0 + m[MOD_SC_M:MOD_SC_M + 1, :]) + m[MOD_SH_M:MOD_SH_M + 1, :]
        hb = h.astype(BF16)
        h_ref[...] = hb
        g_ref[...] = _dot(hb, wg_ref[...])
        o_ref[...] = _dot(hb, w_ref[...]).astype(o_ref.dtype)

    @pl.when(j == 1)
    def _():
        hb = h_ref[...]
        att = _dot(hb, w_ref[:, ATT_COL0:])
        o_ref[:, :ATT_COL0] = _dot(hb, w_ref[:, :ATT_COL0]).astype(o_ref.dtype)
        o_ref[:, ATT_COL0:] = att.astype(o_ref.dtype)
        cos, su, sd = cos_ref[...], su_ref[...], sd_ref[...]
        qw, kw = qw_ref[...], kw_ref[...]
        for hd in range(ATT_HEADS):
            lanes = slice(hd * HEAD_DIM, (hd + 1) * HEAD_DIM)
            q = _norm_rope(att[:, lanes], qw, cos, su, sd, HEAD_DIM ** -0.5 * LOG2_E)
            qt_ref[lanes, :] = q.T.astype(qt_ref.dtype)
        for hd in range(ATT_KV_HEADS):
            lanes = slice(hd * HEAD_DIM, (hd + 1) * HEAD_DIM)
            ak = att[:, ATT_WIDTH + hd * HEAD_DIM:ATT_WIDTH + (hd + 1) * HEAD_DIM]
            av = att[:, ATT_WIDTH + ATT_KV_WIDTH + hd * HEAD_DIM:ATT_WIDTH + ATT_KV_WIDTH + (hd + 1) * HEAD_DIM]
            k_ref[:, lanes] = _norm_rope(ak, kw, cos, su, sd, 1.0).astype(k_ref.dtype)
            vt_ref[hd, :HEAD_DIM, :] = av.T.astype(vt_ref.dtype)
            vt_ref[hd, HEAD_DIM:, :] = jnp.ones((BF16_ROWS, vt_ref.shape[-1]), vt_ref.dtype)


def _inproj_call(x2, mod, w_main, w_gate, cos, sin_up, sin_dn, q_norm_w, k_norm_w, bsz, seq):
    m = x2.shape[0]
    tm, tn = 512, INPROJ_TN
    nt = seq // tm
    per_key_tile = ATT_TK // tm
    assert PM_WIDTH == 2 * tn and ATT_TK % tm == 0
    tab = pl.BlockSpec((tm, HEAD_DIM), lambda i, j: (i % nt, 0))
    vec = pl.BlockSpec((1, HEAD_DIM), lambda i, j: (0, 0))
    blocks = (_nbytes((tm, D_MODEL), F32) + _nbytes((MOD_ROWS, D_MODEL), F32) + _nbytes((D_MODEL, tn), BF16)
              + _nbytes((D_MODEL, LANES), BF16) + _nbytes((tm, tn), BF16) + _nbytes((tm, LANES), F32)
              + 3 * _nbytes((tm, HEAD_DIM), F32) + _nbytes((ATT_WIDTH, tm), BF16)
              + _nbytes((tm, ATT_KV_WIDTH), BF16) + _nbytes((ATT_KV_HEADS, VT_ROWS, tm), BF16))
    scratch = _nbytes((tm, D_MODEL), BF16)
    temps = _nbytes((tm, D_MODEL), F32) + 2 * _nbytes((tm, tn), F32)
    return pl.pallas_call(
        _inproj_kernel,
        grid=(m // tm, PM_WIDTH // tn),
        in_specs=[
            pl.BlockSpec((tm, D_MODEL), lambda i, j: (i, 0)),
            pl.BlockSpec((None, MOD_ROWS, D_MODEL), lambda i, j: (i // nt, 0, 0)),
            pl.BlockSpec((D_MODEL, tn), lambda i, j: (0, j)),
            pl.BlockSpec((D_MODEL, LANES), lambda i, j: (0, 0)),
            tab, tab, tab, vec, vec,
        ],
        out_specs=[
            pl.BlockSpec((tm, tn), lambda i, j: (i, j)),
            pl.BlockSpec((tm, LANES), lambda i, j: (i, 0)),
            pl.BlockSpec((None, ATT_WIDTH, tm), lambda i, j: (i // nt, 0, i % nt)),
            pl.BlockSpec((tm, ATT_KV_WIDTH), lambda i, j: (i, 0)),
            pl.BlockSpec((None, ATT_KV_HEADS, None, VT_ROWS, tm),
                         lambda i, j: (i // nt, 0, (i % nt) // per_key_tile, 0, i % per_key_tile)),
        ],
        out_shape=[
            jax.ShapeDtypeStruct((m, PM_WIDTH), BF16),
            jax.ShapeDtypeStruct((m, LANES), F32),
            jax.ShapeDtypeStruct((bsz, ATT_WIDTH, seq), BF16),
            jax.ShapeDtypeStruct((m, ATT_KV_WIDTH), BF16),
            jax.ShapeDtypeStruct((bsz, ATT_KV_HEADS, seq // ATT_TK, VT_ROWS, ATT_TK), BF16),
        ],
        scratch_shapes=[pltpu.VMEM((tm, D_MODEL), BF16)],
        compiler_params=pltpu.CompilerParams(
            dimension_semantics=("parallel", "arbitrary"),
            vmem_limit_bytes=_vmem_limit(blocks, scratch, temps)),
        name="inproj",
    )(x2, mod, w_main, w_gate, cos, sin_up, sin_dn, q_norm_w.reshape(1, HEAD_DIM), k_norm_w.reshape(1, HEAD_DIM))


def _gdnprep_kernel(x_ref, xp_ref, xn_ref, gate_ref, cw_ref, gp_ref,
                    q_ref, k_ref, v_ref, gcol_ref, grow_ref, *, tt):
    i = pl.program_id(1)
    first = i == 0
    last = i == pl.num_programs(1) - 1
    cw = cw_ref[...]
    sr = lax.broadcasted_iota(jnp.int32, (tt, tt), 0)
    sc = lax.broadcasted_iota(jnp.int32, (tt, tt), 1)
    shifts = [tap - CONV_W // 2 for tap in range(CONV_W)]
    shift_mats = {s: jnp.where(sc == sr + s, 1.0, 0.0).astype(BF16) for s in shifts if s != 0}
    cw_chunk = 2 * HEAD_DIM
    r8 = lax.broadcasted_iota(jnp.int32, (SUBLANES, cw_chunk), 0)
    for part, out_ref in enumerate((q_ref, k_ref, v_ref)):
        for ch in range(GDN_WIDTH // cw_chunk):
            cols = slice(part * GDN_WIDTH + ch * cw_chunk, part * GDN_WIDTH + (ch + 1) * cw_chunk)
            xb = x_ref[:, cols]
            prev8 = xp_ref[:, cols].astype(F32)[BF16_ROWS - SUBLANES:]
            next8 = xn_ref[:, cols].astype(F32)[:SUBLANES]
            prev8 = jnp.where(first, 0.0, prev8)
            next8 = jnp.where(last, 0.0, next8)
            w = cw[:, cols]
            acc = xb.astype(F32) * w[CONV_W // 2:CONV_W // 2 + 1, :]
            top = jnp.zeros((SUBLANES, cw_chunk), F32)
            bot = jnp.zeros((SUBLANES, cw_chunk), F32)
            for tap, s in enumerate(shifts):
                if s == 0:
                    continue
                wt = w[tap:tap + 1, :]
                acc = acc + _dot(shift_mats[s], xb) * wt
                if s < 0:
                    top = top + jnp.where(r8 < -s, pltpu.roll(prev8, -s, 0), 0.0) * wt
                else:
                    bot = bot + jnp.where(r8 >= SUBLANES - s, pltpu.roll(next8, SUBLANES - s, 0), 0.0) * wt
            acc = jnp.concatenate([acc[:SUBLANES] + top, acc[SUBLANES:tt - SUBLANES],
                                   acc[tt - SUBLANES:] + bot], axis=0)
            y = _silu(acc)
            if part < 2:
                heads = []
                for h in range(cw_chunk // HEAD_DIM):
                    yh = y[:, h * HEAD_DIM:(h + 1) * HEAD_DIM]
                    inv = lax.rsqrt(jnp.sum(yh * yh, axis=-1, keepdims=True) + NORM_EPS)
                    if part == 0:
                        inv = inv * (HEAD_DIM ** -0.5)
                    heads.append(yh * inv)
                y = jnp.concatenate(heads, axis=1)
            out_ref[:, ch * cw_chunk:(ch + 1) * cw_chunk] = y.astype(out_ref.dtype)

    raw = gate_ref[...]
    gp = gp_ref[...]
    col = lax.broadcasted_iota(jnp.int32, raw.shape, 1)
    beta = _sigmoid(raw)
    z = raw + gp[0:1, :]
    softplus = jnp.maximum(z, 0.0) + jnp.log(1.0 + jnp.exp(-jnp.abs(z)))
    logdec = -jnp.exp(gp[1:2, :]) * softplus
    is_dec = (col >= N_GATE_COLS) & (col < 2 * N_GATE_COLS)
    gsrc = jnp.where(is_dec, logdec, 0.0)
    r = lax.broadcasted_iota(jnp.int32, (tt, tt), 0)
    c = lax.broadcasted_iota(jnp.int32, (tt, tt), 1)
    same = (r // GDN_CHUNK) == (c // GDN_CHUNK)
    p_lo = jnp.where(same & (c <= r), 1.0, 0.0).astype(F32)
    p_up = jnp.where(same & (c >= r), 1.0, 0.0).astype(F32)
    cum_lo = jnp.dot(p_lo, gsrc, precision=lax.Precision.HIGHEST, preferred_element_type=F32)
    cum_up = jnp.dot(p_up, gsrc, precision=lax.Precision.HIGHEST, preferred_element_type=F32)
    total = cum_lo + cum_up - gsrc
    fwd_col = col < N_GATE_COLS + GDN_HEADS
    gc = jnp.where(fwd_col, cum_lo, cum_up)
    tot_shift = pltpu.roll(total, N_GATE_COLS, 1)
    gcol = jnp.where(col < N_GATE_COLS, beta,
                     jnp.where(col < 2 * N_GATE_COLS, gc,
                               jnp.where(col < 3 * N_GATE_COLS, tot_shift, 0.0)))
    gcol_ref[...] = gcol
    grow_ref[...] = gcol.T


def _gdnprep_call(pm, gates, conv_w8, gate_params):
    bsz, seq, _ = pm.shape
    tt = 256
    hb = tt // BF16_ROWS
    n_halo = seq // BF16_ROWS
    blocks = (_nbytes((tt, GDN_QKV), BF16) + 2 * _nbytes((BF16_ROWS, GDN_QKV), BF16) + _nbytes((tt, LANES), F32)
              + _nbytes((SUBLANES, GDN_QKV), F32) + 3 * _nbytes((tt, GDN_WIDTH), BF16) + 2 * _nbytes((tt, LANES), F32))
    temps = 8 * _nbytes((tt, GDN_WIDTH), F32) + 4 * _nbytes((tt, tt), F32)
    return pl.pallas_call(
        functools.partial(_gdnprep_kernel, tt=tt),
        grid=(bsz, seq // tt),
        in_specs=[
            pl.BlockSpec((None, tt, GDN_QKV), lambda b, i: (b, i, 0)),
            pl.BlockSpec((None, BF16_ROWS, GDN_QKV), lambda b, i: (b, jnp.maximum(i * hb - 1, 0), 0)),
            pl.BlockSpec((None, BF16_ROWS, GDN_QKV), lambda b, i: (b, jnp.minimum((i + 1) * hb, n_halo - 1), 0)),
            pl.BlockSpec((None, tt, LANES), lambda b, i: (b, i, 0)),
            pl.BlockSpec((SUBLANES, GDN_QKV), lambda b, i: (0, 0)),
            pl.BlockSpec((SUBLANES, LANES), lambda b, i: (0, 0)),
        ],
        out_specs=[
            pl.BlockSpec((None, tt, GDN_WIDTH), lambda b, i: (b, i, 0)),
            pl.BlockSpec((None, tt, GDN_WIDTH), lambda b, i: (b, i, 0)),
            pl.BlockSpec((None, tt, GDN_WIDTH), lambda b, i: (b, i, 0)),
            pl.BlockSpec((None, tt, LANES), lambda b, i: (b, i, 0)),
            pl.BlockSpec((None, LANES, tt), lambda b, i: (b, 0, i)),
        ],
        out_shape=[
            jax.ShapeDtypeStruct((bsz, seq, GDN_WIDTH), BF16),
            jax.ShapeDtypeStruct((bsz, seq, GDN_WIDTH), BF16),
            jax.ShapeDtypeStruct((bsz, seq, GDN_WIDTH), BF16),
            jax.ShapeDtypeStruct((bsz, seq, LANES), F32),
            jax.ShapeDtypeStruct((bsz, LANES, seq), F32),
        ],
        compiler_params=pltpu.CompilerParams(
            dimension_semantics=("parallel", "parallel"),
            vmem_limit_bytes=_vmem_limit(blocks, temp_bytes=temps)),
        name="gdnprep",
    )(pm, pm, pm, gates, conv_w8, gate_params)


def _gdn_kernel(qf_ref, kf_ref, vf_ref, gcf_ref, grf_ref,
                qb_ref, kb_ref, vb_ref, gcb_ref, grb_ref,
                of_ref, ob_ref, state_ref):
    @pl.when(pl.program_id(1) == 0)
    def _():
        state_ref[...] = jnp.zeros_like(state_ref)

    n = GDN_CHUNK
    r = lax.broadcasted_iota(jnp.int32, (n, n), 0)
    c = lax.broadcasted_iota(jnp.int32, (n, n), 1)
    eye = jnp.where(r == c, 1.0, 0.0).astype(F32)

    units = []
    for d, (q_ref, k_ref, v_ref, gc_ref, gr_ref, o_ref) in enumerate((
            (qf_ref, kf_ref, vf_ref, gcf_ref, grf_ref, of_ref),
            (qb_ref, kb_ref, vb_ref, gcb_ref, grb_ref, ob_ref))):
        gcol = gc_ref[...]
        grow = gr_ref[...]
        reverse = d == 1
        incl = (r <= c) if reverse else (r >= c)
        strict = (r < c) if reverse else (r > c)
        for h in range(GDN_HEADS):
            lanes = slice(h * HEAD_DIM, (h + 1) * HEAD_DIM)
            j = d * GDN_HEADS + h
            beta = gcol[:, j:j + 1]
            gcc = gcol[:, N_GATE_COLS + j:N_GATE_COLS + j + 1]
            gtc = gcol[:, 2 * N_GATE_COLS + j:2 * N_GATE_COLS + j + 1]
            gcr = grow[N_GATE_COLS + j:N_GATE_COLS + j + 1, :]
            gtr = grow[2 * N_GATE_COLS + j:2 * N_GATE_COLS + j + 1, :]
            q = q_ref[:, lanes]
            k = k_ref[:, lanes]
            kf = k.astype(F32)
            kb = kf * beta
            egc = jnp.exp(gcc)
            units.append(dict(
                d=d, h=h, lanes=lanes, o_ref=o_ref, strict=strict, q=q, k=k,
                decay=jnp.where(incl, jnp.exp(jnp.where(incl, gcc - gcr, 0.0)), 0.0),
                kbb=kb.astype(BF16),
                vb=(v_ref[:, lanes].astype(F32) * beta).astype(BF16),
                kbg=(kb * egc).astype(BF16),
                qg=(q.astype(F32) * egc).astype(BF16),
                kdec_t=(kf * jnp.exp(gtc - gcc)).T.astype(BF16),
                sdec=jnp.exp(gtr)))

    for u in units:
        aq = _dot_nt(jnp.concatenate([u["kbb"], u["q"]], axis=0), u["k"])
        low = jnp.where(u["strict"], aq[:n] * u["decay"], 0.0)
        u["qk"] = (aq[n:] * u["decay"]).astype(BF16)
        u["tinv"] = eye - low
        u["lb"] = low.astype(BF16)

    for u in units:
        u["power"] = _dot(u["lb"], u["lb"])
    n_factors = (n - 1).bit_length() - 1
    for it in range(n_factors):
        for u in units:
            pb = u["power"].astype(BF16)
            if it + 1 < n_factors:
                both = _dot(jnp.concatenate([u["tinv"].astype(BF16), pb], axis=0), pb)
                u["tinv"] = u["tinv"] + both[:n]
                u["power"] = both[n:]
            else:
                u["tinv"] = u["tinv"] + _dot(u["tinv"].astype(BF16), pb)

    for u in units:
        u["uw"] = _dot(u["tinv"].astype(BF16), jnp.concatenate([u["vb"], u["kbg"]], axis=1))
    for u in units:
        u["state"] = state_ref[u["d"], u["h"]]
        w = u["uw"][:, HEAD_DIM:].astype(BF16)
        u["ws"] = _dot(jnp.concatenate([w, u["qg"]], axis=0), u["state"].astype(BF16))
    for u in units:
        v_new = (u["uw"][:, :HEAD_DIM] - u["ws"][:n]).astype(BF16)
        u["os"] = _dot(jnp.concatenate([u["qk"], u["kdec_t"]], axis=0), v_new)
    for u in units:
        u["o_ref"][:, u["lanes"]] = u["ws"][n:] + u["os"][:n]
        state_ref[u["d"], u["h"]] = u["state"] * u["sdec"] + u["os"][n:]


def _gdn_call(q, k, v, gcol, grow):
    bsz, seq, _ = q.shape
    tb = GDN_CHUNK
    nb = seq // tb
    qkv_spec_f = pl.BlockSpec((None, tb, GDN_WIDTH), lambda b, i: (b, i, 0))
    qkv_spec_b = pl.BlockSpec((None, tb, GDN_WIDTH), lambda b, i: (b, nb - 1 - i, 0))
    gc_spec_f = pl.BlockSpec((None, tb, LANES), lambda b, i: (b, i, 0))
    gc_spec_b = pl.BlockSpec((None, tb, LANES), lambda b, i: (b, nb - 1 - i, 0))
    gr_spec_f = pl.BlockSpec((None, LANES, tb), lambda b, i: (b, 0, i))
    gr_spec_b = pl.BlockSpec((None, LANES, tb), lambda b, i: (b, 0, nb - 1 - i))
    blocks = 2 * (3 * _nbytes((tb, GDN_WIDTH), BF16) + 2 * _nbytes((tb, LANES), F32) + _nbytes((tb, GDN_WIDTH), F32))
    scratch = _nbytes((2, GDN_HEADS, HEAD_DIM, HEAD_DIM), F32)
    temps = 2 * GDN_HEADS * 24 * _nbytes((GDN_CHUNK, LANES), F32)
    return pl.pallas_call(
        _gdn_kernel,
        grid=(bsz, nb),
        in_specs=[qkv_spec_f, qkv_spec_f, qkv_spec_f, gc_spec_f, gr_spec_f,
                  qkv_spec_b, qkv_spec_b, qkv_spec_b, gc_spec_b, gr_spec_b],
        out_specs=[qkv_spec_f, qkv_spec_b],
        out_shape=[jax.ShapeDtypeStruct((bsz, seq, GDN_WIDTH), F32),
                   jax.ShapeDtypeStruct((bsz, seq, GDN_WIDTH), F32)],
        scratch_shapes=[pltpu.VMEM((2, GDN_HEADS, HEAD_DIM, HEAD_DIM), F32)],
        compiler_params=pltpu.CompilerParams(
            dimension_semantics=("parallel", "arbitrary"),
            vmem_limit_bytes=_vmem_limit(blocks, scratch, temps)),
        name="gdn",
    )(q, k, v, gcol, grow, q, k, v, gcol, grow)


def _rope_tables(seq):
    rows = seq // GRID_W
    half = HEAD_DIM // 2
    inv_freq = ROPE_THETA ** (-jnp.arange(0, half, 2, dtype=F32) / half)
    ang_r = jnp.arange(rows, dtype=F32)[:, None] * inv_freq
    ang_c = jnp.arange(GRID_W, dtype=F32)[:, None] * inv_freq

    def table(fn):
        tr = jnp.broadcast_to(fn(ang_r)[:, None, :], (rows, GRID_W, half // 2))
        tc = jnp.broadcast_to(fn(ang_c)[None, :, :], (rows, GRID_W, half // 2))
        return jnp.concatenate([tr, tr, tc, tc], -1).reshape(seq, HEAD_DIM)

    cos, sin = table(jnp.cos), table(jnp.sin)
    lane = jnp.arange(HEAD_DIM)
    first = (lane // (HEAD_DIM // 4)) % 2 == 0
    sin_up = jnp.where(first, -sin, 0.0)
    sin_dn = jnp.where(first, 0.0, sin)
    return cos, sin_up, sin_dn


def _norm_rope(x, w, cos, sin_up, sin_dn, scale):
    xn = x * lax.rsqrt(jnp.mean(x * x, axis=-1, keepdims=True) + NORM_EPS) * w
    quarter = HEAD_DIM // 4
    up = pltpu.roll(xn, HEAD_DIM - quarter, 1)
    dn = pltpu.roll(xn, quarter, 1)
    y = xn * cos + up * sin_up + dn * sin_dn
    return y * scale if scale != 1.0 else y


ATT_TK = 1024
FLASH_SUB = 256
VT_ROWS = HEAD_DIM + BF16_ROWS
LOG2_E = 1.4426950408889634


def _flash_kernel(qt_ref, qtn_ref, k_ref, vt_ref, o_ref, m_ref, acc_ref, qc_ref, qn_ref,
                  sa_ref, sb_ref, mca_ref, mcb_ref, *, tq, tk, ts, n_k):
    m_ref[...] = jnp.full_like(m_ref, -jnp.inf)
    acc_ref[...] = jnp.zeros_like(acc_ref)
    for g in range(ATT_GROUP):
        qc_ref[:, g * tq:(g + 1) * tq] = qt_ref[g * HEAD_DIM:(g + 1) * HEAD_DIM, :]
        qn_ref[:, g * tq:(g + 1) * tq] = qtn_ref[g * HEAD_DIM:(g + 1) * HEAD_DIM, :]

    n_sub = tk // ts
    bufs = ((sa_ref, mca_ref), (sb_ref, mcb_ref))

    def scores(j, q_ref, s_ref, i):
        start = pl.multiple_of(j * tk + i * ts, ts)
        st = _dot(k_ref[pl.ds(start, ts), :], q_ref[...])
        s_ref[i * ts:(i + 1) * ts, :] = st
        return jnp.max(st, axis=0, keepdims=True)

    def weigh(j, s_ref, i, m_new):
        p = jnp.exp2(s_ref[i * ts:(i + 1) * ts, :] - m_new)
        return _dot(vt_ref[j, :, i * ts:(i + 1) * ts], p.astype(BF16))

    def step(j, parity, j_next, q_next):
        s_cur, mc_cur = bufs[parity]
        s_nxt, mc_nxt = bufs[1 - parity]
        m_prev = m_ref[...]
        m_new = jnp.maximum(m_prev, mc_cur[...])
        alpha = jnp.exp2(m_prev - m_new)
        mc_next, pv = None, None
        for i in range(n_sub):
            mx = scores(j_next, q_next, s_nxt, i)
            mc_next = mx if mc_next is None else jnp.maximum(mc_next, mx)
            pvi = weigh(j, s_cur, i, m_new)
            pv = pvi if pv is None else pv + pvi
        mc_nxt[...] = mc_next
        acc_ref[...] = alpha * acc_ref[...] + pv
        m_ref[...] = m_new

    @pl.when(pl.program_id(2) == 0)
    def _():
        mc0 = None
        for i in range(n_sub):
            mx = scores(0, qc_ref, sa_ref, i)
            mc0 = mx if mc0 is None else jnp.maximum(mc0, mx)
        mca_ref[...] = mc0

    def body(jj, carry):
        step(2 * jj, 0, 2 * jj + 1, qc_ref)
        step(2 * jj + 1, 1, 2 * jj + 2, qc_ref)
        return carry

    lax.fori_loop(0, n_k // 2 - 1, body, 0)
    step(n_k - 2, 0, n_k - 1, qc_ref)
    step(n_k - 1, 1, 0, qn_ref)
    for g in range(ATT_GROUP):
        lanes = slice(g * tq, (g + 1) * tq)
        out_t = acc_ref[:HEAD_DIM, lanes] / acc_ref[HEAD_DIM:HEAD_DIM + 1, lanes]
        o_ref[:, g * HEAD_DIM:(g + 1) * HEAD_DIM] = out_t.T.astype(o_ref.dtype)


def _flash_call(qt, kr, vt):
    bsz, _, seq = qt.shape
    tq, tk = 256, ATT_TK
    gw = ATT_GROUP * HEAD_DIM
    nq = ATT_GROUP * tq
    n_q, n_k = seq // tq, seq // tk
    assert n_k % 2 == 0, "the score buffers alternate by key-tile parity across query tiles"
    blocks = 3 * _nbytes((gw, tq), BF16) + _nbytes((seq, HEAD_DIM), BF16) + _nbytes((seq, VT_ROWS), BF16)
    scratch = (3 * _nbytes((SUBLANES, nq), F32) + _nbytes((VT_ROWS, nq), F32) + 2 * _nbytes((HEAD_DIM, nq), BF16)
               + 2 * _nbytes((tk, nq), F32))
    temps = 2 * _nbytes((tk, nq), F32)
    return pl.pallas_call(
        functools.partial(_flash_kernel, tq=tq, tk=tk, ts=FLASH_SUB, n_k=n_k),
        grid=(bsz, ATT_KV_HEADS, n_q),
        in_specs=[
            pl.BlockSpec((None, gw, tq), lambda b, h, i: (b, h, i)),
            pl.BlockSpec((None, gw, tq), lambda b, h, i: (b, h, jnp.minimum(i + 1, n_q - 1))),
            pl.BlockSpec((None, seq, HEAD_DIM), lambda b, h, i: (b, 0, h)),
            pl.BlockSpec((None, None, n_k, VT_ROWS, tk), lambda b, h, i: (b, h, 0, 0, 0)),
        ],
        out_specs=pl.BlockSpec((None, tq, gw), lambda b, h, i: (b, i, h)),
        out_shape=jax.ShapeDtypeStruct((bsz, seq, ATT_WIDTH), BF16),
        scratch_shapes=[pltpu.VMEM((1, nq), F32),
                        pltpu.VMEM((VT_ROWS, nq), F32), pltpu.VMEM((HEAD_DIM, nq), BF16),
                        pltpu.VMEM((HEAD_DIM, nq), BF16),
                        pltpu.VMEM((tk, nq), F32), pltpu.VMEM((tk, nq), F32),
                        pltpu.VMEM((1, nq), F32), pltpu.VMEM((1, nq), F32)],
        compiler_params=pltpu.CompilerParams(
            dimension_semantics=("parallel", "parallel", "arbitrary"),
            vmem_limit_bytes=_vmem_limit(blocks, scratch, temps)),
        name="flash",
    )(qt, qt, kr, vt)


def _outproj_kernel(of_ref, ob_ref, z_ref, oa_ref, x_ref, mod_ref, w_ref, nw_ref, lg_ref, lb_ref, o_ref):
    nw = nw_ref[...]
    heads = []
    for h in range(GDN_HEADS):
        lanes = slice(h * HEAD_DIM, (h + 1) * HEAD_DIM)
        o = of_ref[:, lanes] + ob_ref[:, lanes]
        on = o * lax.rsqrt(jnp.mean(o * o, axis=-1, keepdims=True) + NORM_EPS) * nw
        heads.append((on * _silu(z_ref[:, lanes].astype(F32))).astype(BF16))
    og = jnp.concatenate(heads, axis=1)
    mix = _dot(og, w_ref[:GDN_WIDTH, :]) + _dot(oa_ref[...], w_ref[GDN_WIDTH:, :])
    m = mod_ref[...]
    y = DEEPNORM_ALPHA * x_ref[...] + m[MOD_G_M:MOD_G_M + 1, :] * mix
    o_ref[...] = _layer_norm_rows(y, lg_ref[...], lb_ref[...])


def _outproj_call(o_f, o_b, pm2, o_att, x2, mod, w_out, gdn_norm_w, ln_g, ln_b, seq):
    m = x2.shape[0]
    tm = 512
    blocks = (2 * _nbytes((tm, GDN_WIDTH), F32) + 2 * _nbytes((tm, GDN_WIDTH), BF16)
              + 2 * _nbytes((tm, D_MODEL), F32) + _nbytes((MOD_ROWS, D_MODEL), F32)
              + _nbytes((D_MODEL, D_MODEL), BF16))
    vec = pl.BlockSpec((1, D_MODEL), lambda i: (0, 0))
    return pl.pallas_call(
        _outproj_kernel,
        grid=(m // tm,),
        in_specs=[
            pl.BlockSpec((tm, GDN_WIDTH), lambda i: (i, 0)),
            pl.BlockSpec((tm, GDN_WIDTH), lambda i: (i, 0)),
            pl.BlockSpec((tm, GDN_WIDTH), lambda i: (i, PM_Z // GDN_WIDTH)),
            pl.BlockSpec((tm, ATT_WIDTH), lambda i: (i, 0)),
            pl.BlockSpec((tm, D_MODEL), lambda i: (i, 0)),
            pl.BlockSpec((None, MOD_ROWS, D_MODEL), lambda i: ((i * tm) // seq, 0, 0)),
            pl.BlockSpec((D_MODEL, D_MODEL), lambda i: (0, 0)),
            pl.BlockSpec((1, HEAD_DIM), lambda i: (0, 0)),
            vec, vec,
        ],
        out_specs=pl.BlockSpec((tm, D_MODEL), lambda i: (i, 0)),
        out_shape=jax.ShapeDtypeStruct((m, D_MODEL), F32),
        compiler_params=pltpu.CompilerParams(
            dimension_semantics=("parallel",),
            vmem_limit_bytes=_vmem_limit(blocks, temp_bytes=4 * _nbytes((tm, D_MODEL), F32))),
        name="outproj",
    )(o_f, o_b, pm2, o_att, x2, mod, w_out, gdn_norm_w.reshape(1, HEAD_DIM),
      ln_g.reshape(1, D_MODEL), ln_b.reshape(1, D_MODEL))


POOL_REACH = SUBLANES
assert max(POOL_WINDOWS) // 2 <= POOL_REACH


def _pool_kernel(x_ref, xp_ref, xn_ref, mod_ref, pw_ref, ps_ref, lg_ref, lb_ref, o_ref, *, tt, seq):
    t0 = pl.program_id(1) * tt
    m = mod_ref[...]
    sc = 1.0 + m[MOD_SC_M:MOD_SC_M + 1, :]
    sh = m[MOD_SH_M:MOD_SH_M + 1, :]
    x = x_ref[...]
    h_main = x * sc + sh
    h_prev = xp_ref[...] * sc + sh
    h_next = xn_ref[...] * sc + sh
    r = lax.broadcasted_iota(jnp.int32, (tt, tt), 0)
    c = lax.broadcasted_iota(jnp.int32, (tt, tt), 1)
    pos = t0 + lax.broadcasted_iota(jnp.int32, (tt, 1), 0)
    r8 = lax.broadcasted_iota(jnp.int32, (POOL_REACH, 1), 0)
    outs = []
    for gi, win in enumerate(POOL_WINDOWS):
        back, fwd = win // 2, win - 1 - win // 2
        lanes = slice(gi * POOL_GROUP, (gi + 1) * POOL_GROUP)
        band = jnp.where((c >= r - back) & (c <= r + fwd), 1.0, 0.0).astype(BF16)
        hm = h_main[:, lanes]
        wsum = _dot(band, hm.astype(BF16))
        top = jnp.zeros((POOL_REACH, POOL_GROUP), F32)
        bot = jnp.zeros((POOL_REACH, POOL_GROUP), F32)
        for k in range(POOL_REACH):
            if POOL_REACH - k <= back:
                use = (r8 <= k - POOL_REACH + back) & (t0 - POOL_REACH + k >= 0)
                top = top + jnp.where(use, 1.0, 0.0) * h_prev[k:k + 1, lanes]
            if k < fwd:
                use = (r8 >= k + POOL_REACH - fwd) & (t0 + tt + k < seq)
                bot = bot + jnp.where(use, 1.0, 0.0) * h_next[k:k + 1, lanes]
        wsum = jnp.concatenate([wsum[:POOL_REACH] + top, wsum[POOL_REACH:tt - POOL_REACH],
                                wsum[tt - POOL_REACH:] + bot], axis=0)
        cnt = (jnp.minimum(pos + fwd, seq - 1) - jnp.maximum(pos - back, 0) + 1).astype(F32)
        diff = wsum / cnt - hm
        outs.append(_dot(diff.astype(BF16), pw_ref[gi]))
    mix = jnp.concatenate(outs, axis=1) * ps_ref[...]
    y = DEEPNORM_ALPHA * x + m[MOD_G_M:MOD_G_M + 1, :] * mix
    o_ref[...] = _layer_norm_rows(y, lg_ref[...], lb_ref[...])


def _pool_call(x, mod, pool_w, pool_scale, ln_g, ln_b):
    bsz, seq, _ = x.shape
    tt = 256
    hb = tt // POOL_REACH
    n_halo = seq // POOL_REACH
    vec = pl.BlockSpec((1, D_MODEL), lambda b, i: (0, 0))
    blocks = (2 * _nbytes((tt, D_MODEL), F32) + 2 * _nbytes((POOL_REACH, D_MODEL), F32)
              + _nbytes((MOD_ROWS, D_MODEL), F32) + _nbytes(pool_w.shape, BF16))
    temps = 5 * _nbytes((tt, D_MODEL), F32)
    return pl.pallas_call(
        functools.partial(_pool_kernel, tt=tt, seq=seq),
        grid=(bsz, seq // tt),
        in_specs=[
            pl.BlockSpec((None, tt, D_MODEL), lambda b, i: (b, i, 0)),
            pl.BlockSpec((None, POOL_REACH, D_MODEL), lambda b, i: (b, jnp.maximum(i * hb - 1, 0), 0)),
            pl.BlockSpec((None, POOL_REACH, D_MODEL), lambda b, i: (b, jnp.minimum((i + 1) * hb, n_halo - 1), 0)),
            pl.BlockSpec((None, MOD_ROWS, D_MODEL), lambda b, i: (b, 0, 0)),
            pl.BlockSpec(pool_w.shape, lambda b, i: (0, 0, 0)),
            vec, vec, vec,
        ],
        out_specs=pl.BlockSpec((None, tt, D_MODEL), lambda b, i: (b, i, 0)),
        out_shape=jax.ShapeDtypeStruct((bsz, seq, D_MODEL), F32),
        compiler_params=pltpu.CompilerParams(
            dimension_semantics=("parallel", "parallel"),
            vmem_limit_bytes=_vmem_limit(blocks, temp_bytes=temps)),
        name="pool",
    )(x, x, x, mod, pool_w, pool_scale.reshape(1, D_MODEL), ln_g.reshape(1, D_MODEL), ln_b.reshape(1, D_MODEL))


def _mlp_kernel(x_ref, mod_ref, w1_ref, w2_ref, lg_ref, lb_ref, o_ref, h_ref, acc_ref):
    j = pl.program_id(1)

    @pl.when(j == 0)
    def _():
        m = mod_ref[...]
        h = x_ref[...] * (1.0 + m[MOD_SC_F:MOD_SC_F + 1, :]) + m[MOD_SH_F:MOD_SH_F + 1, :]
        h_ref[...] = h.astype(BF16)
        acc_ref[...] = jnp.zeros_like(acc_ref)

    u = jnp.maximum(_dot(h_ref[...], w1_ref[...]), 0.0)
    acc_ref[...] += _dot((u * u).astype(BF16), w2_ref[...])

    @pl.when(j == pl.num_programs(1) - 1)
    def _():
        m = mod_ref[...]
        y = DEEPNORM_ALPHA * x_ref[...] + m[MOD_G_F:MOD_G_F + 1, :] * acc_ref[...]
        o_ref[...] = _layer_norm_rows(y, lg_ref[...], lb_ref[...])


def _mlp_call(x2, mod, w1, w2, layer, ln_g, ln_b, seq):
    m = x2.shape[0]
    tm, tf = 512, 1024
    vec = pl.BlockSpec((1, D_MODEL), lambda i, j: (0, 0))
    blocks = (2 * _nbytes((tm, D_MODEL), F32) + _nbytes((MOD_ROWS, D_MODEL), F32)
              + 2 * _nbytes((D_MODEL, tf), BF16))
    scratch = _nbytes((tm, D_MODEL), BF16) + _nbytes((tm, D_MODEL), F32)
    temps = 2 * _nbytes((tm, tf), F32) + _nbytes((tm, D_MODEL), F32)
    return pl.pallas_call(
        _mlp_kernel,
        grid=(m // tm, D_FF // tf),
        in_specs=[
            pl.BlockSpec((tm, D_MODEL), lambda i, j: (i, 0)),
            pl.BlockSpec((None, MOD_ROWS, D_MODEL), lambda i, j: ((i * tm) // seq, 0, 0)),
            pl.BlockSpec((None, D_MODEL, tf), lambda i, j: (layer, 0, j)),
            pl.BlockSpec((None, tf, D_MODEL), lambda i, j: (layer, j, 0)),
            vec, vec,
        ],
        out_specs=pl.BlockSpec((tm, D_MODEL), lambda i, j: (i, 0)),
        out_shape=jax.ShapeDtypeStruct((m, D_MODEL), F32),
        scratch_shapes=[pltpu.VMEM((tm, D_MODEL), BF16), pltpu.VMEM((tm, D_MODEL), F32)],
        compiler_params=pltpu.CompilerParams(
            dimension_semantics=("parallel", "arbitrary"),
            vmem_limit_bytes=_vmem_limit(blocks, scratch, temps)),
        name="mlp",
    )(x2, mod, w1, w2, ln_g.reshape(1, D_MODEL), ln_b.reshape(1, D_MODEL))


def _mod_table(cond_rows):
    bsz = cond_rows.shape[0]
    t = cond_rows.reshape(bsz, 6, D_MODEL)
    return jnp.pad(t, ((0, 0), (0, MOD_ROWS - 6), (0, 0)))


def _pack_weights(w_in, conv_w, a_log, dt_bias, w_out, pool_w, mlp_w1, mlp_w2):
    wi = w_in[0]
    w_main = jnp.concatenate(
        [wi[:, :OFF_BETA], wi[:, OFF_AQ:]], axis=1).astype(BF16)
    w_gate = jnp.pad(wi[:, OFF_BETA:OFF_AQ], ((0, 0), (0, LANES - 2 * N_GATE_COLS))).astype(BF16)
    conv_w8 = jnp.pad(conv_w[0], ((0, SUBLANES - CONV_W), (0, 0)))
    gate_params = jnp.zeros((SUBLANES, LANES), F32)
    gate_params = gate_params.at[0, N_GATE_COLS:2 * N_GATE_COLS].set(dt_bias[0].reshape(-1))
    gate_params = gate_params.at[1, N_GATE_COLS:2 * N_GATE_COLS].set(a_log[0].reshape(-1))
    return dict(w_main=w_main, w_gate=w_gate, conv_w8=conv_w8, gate_params=gate_params,
                w_out=w_out[0].astype(BF16), pool_w=pool_w[0].astype(BF16),
                w1=mlp_w1.astype(BF16), w2=mlp_w2.astype(BF16))


def _trunk(x, cond, pk, gdn_norm_w, q_norm_w, k_norm_w, pool_scale, ln_g, ln_b):
    bsz, seq, _ = x.shape
    m = bsz * seq
    mod0 = _mod_table(cond[0])
    mod1 = _mod_table(cond[1])
    x2 = x.reshape(m, D_MODEL)

    cos, sin_up, sin_dn = _rope_tables(seq)
    pm2, gates2, qt, kr2, vt = _inproj_call(x2, mod0, pk["w_main"], pk["w_gate"], cos, sin_up, sin_dn,
                                            q_norm_w[0], k_norm_w[0], bsz, seq)
    pm = pm2.reshape(bsz, seq, PM_WIDTH)
    gq, gk, gv, gcol, grow = _gdnprep_call(pm, gates2.reshape(bsz, seq, LANES), pk["conv_w8"], pk["gate_params"])
    o_f, o_b = _gdn_call(gq, gk, gv, gcol, grow)
    o_att = _flash_call(qt, kr2.reshape(bsz, seq, ATT_KV_WIDTH), vt)
    x2 = _outproj_call(o_f.reshape(m, GDN_WIDTH), o_b.reshape(m, GDN_WIDTH), pm2, o_att.reshape(m, ATT_WIDTH),
                       x2, mod0, pk["w_out"], gdn_norm_w[0], ln_g[0, 0], ln_b[0, 0], seq)
    x2 = _mlp_call(x2, mod0, pk["w1"], pk["w2"], 0, ln_g[0, 1], ln_b[0, 1], seq)

    x3 = _pool_call(x2.reshape(bsz, seq, D_MODEL), mod1, pk["pool_w"], pool_scale[0], ln_g[1, 0], ln_b[1, 0])
    x2 = _mlp_call(x3.reshape(m, D_MODEL), mod1, pk["w1"], pk["w2"], 1, ln_g[1, 1], ln_b[1, 1], seq)
    return x2.reshape(bsz, seq, D_MODEL)


def kernel(x_prompt, x_sample, c_prompt, c_sample, w_in, conv_w, a_log, dt_bias, gdn_norm_w, q_norm_w,
           k_norm_w, w_out, pool_w, pool_scale, mlp_w1, mlp_w2, ada_w, ada_b, ln_g, ln_b):
    bp, bs = c_prompt.shape[0], c_sample.shape[0]
    c_all = jnp.concatenate([c_prompt, c_sample], axis=0)
    c_all = jnp.pad(c_all, ((0, (-c_all.shape[0]) % SUBLANES), (0, 0)))
    cond = _ada_call(c_all, ada_w, ada_b)
    pk = _pack_weights(w_in, conv_w, a_log, dt_bias, w_out, pool_w, mlp_w1, mlp_w2)
    y_prompt = _trunk(x_prompt, cond[:, :bp], pk, gdn_norm_w, q_norm_w, k_norm_w, pool_scale, ln_g, ln_b)
    y_sample = _trunk(x_sample, cond[:, bp:bp + bs], pk, gdn_norm_w, q_norm_w, k_norm_w, pool_scale, ln_g, ln_b)
    return (y_prompt, y_sample)
```

```python
import functools

import jax
import jax.numpy as jnp
from jax import lax
from jax.experimental import pallas as pl
from jax.experimental.pallas import tpu as pltpu

F32 = jnp.float32
BF16 = jnp.bfloat16

D_MODEL = 2048
DEPTH = 2
GRID_W = 64
HEAD_DIM = 128
GDN_HEADS = 8
ATT_HEADS = 8
ATT_KV_HEADS = 2
ATT_GROUP = ATT_HEADS // ATT_KV_HEADS
GDN_WIDTH = GDN_HEADS * HEAD_DIM
ATT_WIDTH = ATT_HEADS * HEAD_DIM
ATT_KV_WIDTH = ATT_KV_HEADS * HEAD_DIM
CONV_W = 5
ROPE_THETA = 10000.0
POOL_WINDOWS = (2, 4, 8, 16)
N_POOL_GROUPS = 4
POOL_GROUP = D_MODEL // N_POOL_GROUPS
D_FF = 4 * D_MODEL
DEEPNORM_ALPHA = (2 * DEPTH) ** 0.25
NORM_EPS = 1e-6
LN_EPS = 1e-5

GDN_QKV = 3 * GDN_WIDTH
OFF_Z = GDN_QKV
OFF_BETA = OFF_Z + GDN_WIDTH
OFF_A = OFF_BETA + 2 * GDN_HEADS
OFF_AQ = OFF_A + 2 * GDN_HEADS
OFF_AK = OFF_AQ + ATT_WIDTH
OFF_AV = OFF_AK + ATT_KV_WIDTH
D_IN = OFF_AV + ATT_KV_WIDTH

PM_QKV = 0
PM_Z = GDN_QKV
PM_AQ = PM_Z + GDN_WIDTH
PM_AK = PM_AQ + ATT_WIDTH
PM_AV = PM_AK + ATT_KV_WIDTH
PM_WIDTH = PM_AV + ATT_KV_WIDTH

LANES = 128
SUBLANES = 8
BF16_ROWS = 16
V7X_VMEM_BYTES = 64 * 1024 * 1024

GDN_CHUNK = 128
GDN_BLOCK_CHUNKS = 4
N_GATE_COLS = 2 * GDN_HEADS

MOD_SH_M, MOD_SC_M, MOD_G_M, MOD_SH_F, MOD_SC_F, MOD_G_F = range(6)
MOD_ROWS = 8


def _vmem_limit(block_bytes, scratch_bytes=0, temp_bytes=0):
    need = 2 * block_bytes + scratch_bytes + temp_bytes
    return int(min(need + need // 4, V7X_VMEM_BYTES - 8 * 1024 * 1024))


def _nbytes(shape, dtype):
    n = 1
    for s in shape:
        n *= s
    return n * jnp.dtype(dtype).itemsize


def _sigmoid(x):
    return 1.0 / (1.0 + jnp.exp(-x))


def _silu(x):
    return x * _sigmoid(x)


def _layer_norm_rows(y, g, b):
    mu = jnp.mean(y, axis=-1, keepdims=True)
    yc = y - mu
    var = jnp.mean(yc * yc, axis=-1, keepdims=True)
    return yc * lax.rsqrt(var + LN_EPS) * g + b


def _dot(a, b):
    return jnp.dot(a, b, preferred_element_type=F32)


def _dot_nt(a, b):
    return lax.dot_general(a, b, (((1,), (1,)), ((), ())), preferred_element_type=F32)


def _ada_kernel(c_ref, w_ref, b_ref, o_ref):
    c = c_ref[...]
    s = _silu(c).astype(BF16)
    o_ref[...] = _dot(s, w_ref[...].astype(BF16)) + b_ref[...]


def _ada_call(c_all, ada_w, ada_b):
    rows = c_all.shape[0]
    n = ada_w.shape[-1]
    tn = 1024
    blocks = _nbytes((D_MODEL, tn), F32) + _nbytes((rows, D_MODEL), F32) + 2 * _nbytes((rows, tn), F32)
    return pl.pallas_call(
        _ada_kernel,
        grid=(DEPTH, n // tn),
        in_specs=[
            pl.BlockSpec((rows, D_MODEL), lambda l, j: (0, 0)),
            pl.BlockSpec((None, D_MODEL, tn), lambda l, j: (l, 0, j)),
            pl.BlockSpec((None, 1, tn), lambda l, j: (l, 0, j)),
        ],
        out_specs=pl.BlockSpec((None, rows, tn), lambda l, j: (l, 0, j)),
        out_shape=jax.ShapeDtypeStruct((DEPTH, rows, n), F32),
        compiler_params=pltpu.CompilerParams(
            dimension_semantics=("parallel", "parallel"),
            vmem_limit_bytes=_vmem_limit(blocks, temp_bytes=_nbytes((D_MODEL, tn), BF16))),
        name="ada",
    )(c_all, ada_w, ada_b.reshape(DEPTH, 1, n))


INPROJ_TN = PM_WIDTH // 2
ATT_COL0 = PM_AQ - INPROJ_TN
assert PM_AQ >= INPROJ_TN and ATT_COL0 % LANES == 0


def _inproj_kernel(x_ref, mod_ref, w_ref, wg_ref, cos_ref, su_ref, sd_ref, qw_ref, kw_ref,
                   o_ref, g_ref, qt_ref, k_ref, vt_ref, h_ref):
    j = pl.program_id(1)

    @pl.when(j == 0)
    def _():
        m = mod_ref[...]
        h = x_ref[...] * (1.0 + m[MOD_SC_M:MOD_SC_M + 1, :]) + m[MOD_SH_M:MOD_SH_M + 1, :]
        hb = h.astype(BF16)
        h_ref[...] = hb
        g_ref[...] = _dot(hb, wg_ref[...])
        o_ref[...] = _dot(hb, w_ref[...]).astype(o_ref.dtype)

    @pl.when(j == 1)
    def _():
        hb = h_ref[...]
        att = _dot(hb, w_ref[:, ATT_COL0:])
        o_ref[:, :ATT_COL0] = _dot(hb, w_ref[:, :ATT_COL0]).astype(o_ref.dtype)
        o_ref[:, ATT_COL0:] = att.astype(o_ref.dtype)
        cos, su, sd = cos_ref[...], su_ref[...], sd_ref[...]
        qw, kw = qw_ref[...], kw_ref[...]
        for hd in range(ATT_HEADS):
            lanes = slice(hd * HEAD_DIM, (hd + 1) * HEAD_DIM)
            q = _norm_rope(att[:, lanes], qw, cos, su, sd, HEAD_DIM ** -0.5 * LOG2_E)
            qt_ref[lanes, :] = q.T.astype(qt_ref.dtype)
        for hd in range(ATT_KV_HEADS):
            lanes = slice(hd * HEAD_DIM, (hd + 1) * HEAD_DIM)
            ak = att[:, ATT_WIDTH + hd * HEAD_DIM:ATT_WIDTH + (hd + 1) * HEAD_DIM]
            av = att[:, ATT_WIDTH + ATT_KV_WIDTH + hd * HEAD_DIM:ATT_WIDTH + ATT_KV_WIDTH + (hd + 1) * HEAD_DIM]
            k_ref[:, lanes] = _norm_rope(ak, kw, cos, su, sd, 1.0).astype(k_ref.dtype)
            vt_ref[hd, :HEAD_DIM, :] = av.T.astype(vt_ref.dtype)
            vt_ref[hd, HEAD_DIM:, :] = jnp.ones((BF16_ROWS, vt_ref.shape[-1]), vt_ref.dtype)


def _inproj_call(x2, mod, w_main, w_gate, cos, sin_up, sin_dn, q_norm_w, k_norm_w, bsz, seq):
    m = x2.shape[0]
    tm, tn = 512, INPROJ_TN
    nt = seq // tm
    per_key_tile = ATT_TK // tm
    assert PM_WIDTH == 2 * tn and ATT_TK % tm == 0
    tab = pl.BlockSpec((tm, HEAD_DIM), lambda i, j: (i % nt, 0))
    vec = pl.BlockSpec((1, HEAD_DIM), lambda i, j: (0, 0))
    blocks = (_nbytes((tm, D_MODEL), F32) + _nbytes((MOD_ROWS, D_MODEL), F32) + _nbytes((D_MODEL, tn), BF16)
              + _nbytes((D_MODEL, LANES), BF16) + _nbytes((tm, tn), BF16) + _nbytes((tm, LANES), F32)
              + 3 * _nbytes((tm, HEAD_DIM), F32) + _nbytes((ATT_WIDTH, tm), BF16)
              + _nbytes((tm, ATT_KV_WIDTH), BF16) + _nbytes((ATT_KV_HEADS, VT_ROWS, tm), BF16))
    scratch = _nbytes((tm, D_MODEL), BF16)
    temps = _nbytes((tm, D_MODEL), F32) + 2 * _nbytes((tm, tn), F32)
    return pl.pallas_call(
        _inproj_kernel,
        grid=(m // tm, PM_WIDTH // tn),
        in_specs=[
            pl.BlockSpec((tm, D_MODEL), lambda i, j: (i, 0)),
            pl.BlockSpec((None, MOD_ROWS, D_MODEL), lambda i, j: (i // nt, 0, 0)),
            pl.BlockSpec((D_MODEL, tn), lambda i, j: (0, j)),
            pl.BlockSpec((D_MODEL, LANES), lambda i, j: (0, 0)),
            tab, tab, tab, vec, vec,
        ],
        out_specs=[
            pl.BlockSpec((tm, tn), lambda i, j: (i, j)),
            pl.BlockSpec((tm, LANES), lambda i, j: (i, 0)),
            pl.BlockSpec((None, ATT_WIDTH, tm), lambda i, j: (i // nt, 0, i % nt)),
            pl.BlockSpec((tm, ATT_KV_WIDTH), lambda i, j: (i, 0)),
            pl.BlockSpec((None, ATT_KV_HEADS, None, VT_ROWS, tm),
                         lambda i, j: (i // nt, 0, (i % nt) // per_key_tile, 0, i % per_key_tile)),
        ],
        out_shape=[
            jax.ShapeDtypeStruct((m, PM_WIDTH), BF16),
            jax.ShapeDtypeStruct((m, LANES), F32),
            jax.ShapeDtypeStruct((bsz, ATT_WIDTH, seq), BF16),
            jax.ShapeDtypeStruct((m, ATT_KV_WIDTH), BF16),
            jax.ShapeDtypeStruct((bsz, ATT_KV_HEADS, seq // ATT_TK, VT_ROWS, ATT_TK), BF16),
        ],
        scratch_shapes=[pltpu.VMEM((tm, D_MODEL), BF16)],
        compiler_params=pltpu.CompilerParams(
            dimension_semantics=("parallel", "arbitrary"),
            vmem_limit_bytes=_vmem_limit(blocks, scratch, temps)),
        name="inproj",
    )(x2, mod, w_main, w_gate, cos, sin_up, sin_dn, q_norm_w.reshape(1, HEAD_DIM), k_norm_w.reshape(1, HEAD_DIM))


def _gdnprep_kernel(x_ref, xp_ref, xn_ref, gate_ref, cw_ref, gp_ref,
                    q_ref, k_ref, v_ref, gcol_ref, grow_ref, *, tt):
    i = pl.program_id(1)
    first = i == 0
    last = i == pl.num_programs(1) - 1
    cw = cw_ref[...]
    sr = lax.broadcasted_iota(jnp.int32, (tt, tt), 0)
    sc = lax.broadcasted_iota(jnp.int32, (tt, tt), 1)
    shifts = [tap - CONV_W // 2 for tap in range(CONV_W)]
    shift_mats = {s: jnp.where(sc == sr + s, 1.0, 0.0).astype(BF16) for s in shifts if s != 0}
    cw_chunk = 2 * HEAD_DIM
    r8 = lax.broadcasted_iota(jnp.int32, (SUBLANES, cw_chunk), 0)
    for part, out_ref in enumerate((q_ref, k_ref, v_ref)):
        for ch in range(GDN_WIDTH // cw_chunk):
            cols = slice(part * GDN_WIDTH + ch * cw_chunk, part * GDN_WIDTH + (ch + 1) * cw_chunk)
            xb = x_ref[:, cols]
            prev8 = xp_ref[:, cols].astype(F32)[BF16_ROWS - SUBLANES:]
            next8 = xn_ref[:, cols].astype(F32)[:SUBLANES]
            prev8 = jnp.where(first, 0.0, prev8)
            next8 = jnp.where(last, 0.0, next8)
            w = cw[:, cols]
            acc = xb.astype(F32) * w[CONV_W // 2:CONV_W // 2 + 1, :]
            top = jnp.zeros((SUBLANES, cw_chunk), F32)
            bot = jnp.zeros((SUBLANES, cw_chunk), F32)
            for tap, s in enumerate(shifts):
                if s == 0:
                    continue
                wt = w[tap:tap + 1, :]
                acc = acc + _dot(shift_mats[s], xb) * wt
                if s < 0:
                    top = top + jnp.where(r8 < -s, pltpu.roll(prev8, -s, 0), 0.0) * wt
                else:
                    bot = bot + jnp.where(r8 >= SUBLANES - s, pltpu.roll(next8, SUBLANES - s, 0), 0.0) * wt
            acc = jnp.concatenate([acc[:SUBLANES] + top, acc[SUBLANES:tt - SUBLANES],
                                   acc[tt - SUBLANES:] + bot], axis=0)
            y = _silu(acc)
            if part < 2:
                heads = []
                for h in range(cw_chunk // HEAD_DIM):
                    yh = y[:, h * HEAD_DIM:(h + 1) * HEAD_DIM]
                    inv = lax.rsqrt(jnp.sum(yh * yh, axis=-1, keepdims=True) + NORM_EPS)
                    if part == 0:
                        inv = inv * (HEAD_DIM ** -0.5)
                    heads.append(yh * inv)
                y = jnp.concatenate(heads, axis=1)
            out_ref[:, ch * cw_chunk:(ch + 1) * cw_chunk] = y.astype(out_ref.dtype)

    raw = gate_ref[...]
    gp = gp_ref[...]
    col = lax.broadcasted_iota(jnp.int32, raw.shape, 1)
    beta = _sigmoid(raw)
    z = raw + gp[0:1, :]
    softplus = jnp.maximum(z, 0.0) + jnp.log(1.0 + jnp.exp(-jnp.abs(z)))
    logdec = -jnp.exp(gp[1:2, :]) * softplus
    is_dec = (col >= N_GATE_COLS) & (col < 2 * N_GATE_COLS)
    gsrc = jnp.where(is_dec, logdec, 0.0)
    r = lax.broadcasted_iota(jnp.int32, (tt, tt), 0)
    c = lax.broadcasted_iota(jnp.int32, (tt, tt), 1)
    same = (r // GDN_CHUNK) == (c // GDN_CHUNK)
    p_lo = jnp.where(same & (c <= r), 1.0, 0.0).astype(F32)
    p_up = jnp.where(same & (c >= r), 1.0, 0.0).astype(F32)
    cum_lo = jnp.dot(p_lo, gsrc, precision=lax.Precision.HIGHEST, preferred_element_type=F32)
    cum_up = jnp.dot(p_up, gsrc, precision=lax.Precision.HIGHEST, preferred_element_type=F32)
    total = cum_lo + cum_up - gsrc
    fwd_col = col < N_GATE_COLS + GDN_HEADS
    gc = jnp.where(fwd_col, cum_lo, cum_up)
    tot_shift = pltpu.roll(total, N_GATE_COLS, 1)
    gcol = jnp.where(col < N_GATE_COLS, beta,
                     jnp.where(col < 2 * N_GATE_COLS, gc,
                               jnp.where(col < 3 * N_GATE_COLS, tot_shift, 0.0)))
    gcol_ref[...] = gcol
    grow_ref[...] = gcol.T


def _gdnprep_call(pm, gates, conv_w8, gate_params):
    bsz, seq, _ = pm.shape
    tt = 256
    hb = tt // BF16_ROWS
    n_halo = seq // BF16_ROWS
    blocks = (_nbytes((tt, GDN_QKV), BF16) + 2 * _nbytes((BF16_ROWS, GDN_QKV), BF16) + _nbytes((tt, LANES), F32)
              + _nbytes((SUBLANES, GDN_QKV), F32) + 3 * _nbytes((tt, GDN_WIDTH), BF16) + 2 * _nbytes((tt, LANES), F32))
    temps = 8 * _nbytes((tt, GDN_WIDTH), F32) + 4 * _nbytes((tt, tt), F32)
    return pl.pallas_call(
        functools.partial(_gdnprep_kernel, tt=tt),
        grid=(bsz, seq // tt),
        in_specs=[
            pl.BlockSpec((None, tt, GDN_QKV), lambda b, i: (b, i, 0)),
            pl.BlockSpec((None, BF16_ROWS, GDN_QKV), lambda b, i: (b, jnp.maximum(i * hb - 1, 0), 0)),
            pl.BlockSpec((None, BF16_ROWS, GDN_QKV), lambda b, i: (b, jnp.minimum((i + 1) * hb, n_halo - 1), 0)),
            pl.BlockSpec((None, tt, LANES), lambda b, i: (b, i, 0)),
            pl.BlockSpec((SUBLANES, GDN_QKV), lambda b, i: (0, 0)),
            pl.BlockSpec((SUBLANES, LANES), lambda b, i: (0, 0)),
        ],
        out_specs=[
            pl.BlockSpec((None, tt, GDN_WIDTH), lambda b, i: (b, i, 0)),
            pl.BlockSpec((None, tt, GDN_WIDTH), lambda b, i: (b, i, 0)),
            pl.BlockSpec((None, tt, GDN_WIDTH), lambda b, i: (b, i, 0)),
            pl.BlockSpec((None, tt, LANES), lambda b, i: (b, i, 0)),
            pl.BlockSpec((None, LANES, tt), lambda b, i: (b, 0, i)),
        ],
        out_shape=[
            jax.ShapeDtypeStruct((bsz, seq, GDN_WIDTH), BF16),
            jax.ShapeDtypeStruct((bsz, seq, GDN_WIDTH), BF16),
            jax.ShapeDtypeStruct((bsz, seq, GDN_WIDTH), BF16),
            jax.ShapeDtypeStruct((bsz, seq, LANES), F32),
            jax.ShapeDtypeStruct((bsz, LANES, seq), F32),
        ],
        compiler_params=pltpu.CompilerParams(
            dimension_semantics=("parallel", "parallel"),
            vmem_limit_bytes=_vmem_limit(blocks, temp_bytes=temps)),
        name="gdnprep",
    )(pm, pm, pm, gates, conv_w8, gate_params)


def _gdn_kernel(qf_ref, kf_ref, vf_ref, gcf_ref, grf_ref,
                qb_ref, kb_ref, vb_ref, gcb_ref, grb_ref,
                of_ref, ob_ref, state_ref):
    @pl.when(pl.program_id(1) == 0)
    def _():
        state_ref[...] = jnp.zeros_like(state_ref)

    n = GDN_CHUNK
    r = lax.broadcasted_iota(jnp.int32, (n, n), 0)
    c = lax.broadcasted_iota(jnp.int32, (n, n), 1)
    eye = jnp.where(r == c, 1.0, 0.0).astype(F32)

    units = []
    for d, (q_ref, k_ref, v_ref, gc_ref, gr_ref, o_ref) in enumerate((
            (qf_ref, kf_ref, vf_ref, gcf_ref, grf_ref, of_ref),
            (qb_ref, kb_ref, vb_ref, gcb_ref, grb_ref, ob_ref))):
        gcol = gc_ref[...]
        grow = gr_ref[...]
        reverse = d == 1
        incl = (r <= c) if reverse else (r >= c)
        strict = (r < c) if reverse else (r > c)
        for pos in range(GDN_BLOCK_CHUNKS):
            chunk = GDN_BLOCK_CHUNKS - 1 - pos if reverse else pos
            rows = slice(chunk * n, (chunk + 1) * n)
            for h in range(GDN_HEADS):
                lanes = slice(h * HEAD_DIM, (h + 1) * HEAD_DIM)
                j = d * GDN_HEADS + h
                beta = gcol[rows, j:j + 1]
                gcc = gcol[rows, N_GATE_COLS + j:N_GATE_COLS + j + 1]
                gtc = gcol[rows, 2 * N_GATE_COLS + j:2 * N_GATE_COLS + j + 1]
                gcr = grow[N_GATE_COLS + j:N_GATE_COLS + j + 1, rows]
                gtr = grow[2 * N_GATE_COLS + j:2 * N_GATE_COLS + j + 1, rows]
                q = q_ref[rows, lanes]
                k = k_ref[rows, lanes]
                kf = k.astype(F32)
                kb = kf * beta
                egc = jnp.exp(gcc)
                units.append(dict(
                    d=d, h=h, pos=pos, rows=rows, lanes=lanes, o_ref=o_ref, strict=strict, q=q, k=k,
                    decay=jnp.where(incl, jnp.exp(jnp.where(incl, gcc - gcr, 0.0)), 0.0),
                    kbb=kb.astype(BF16),
                    vb=(v_ref[rows, lanes].astype(F32) * beta).astype(BF16),
                    kbg=(kb * egc).astype(BF16),
                    qg=(q.astype(F32) * egc).astype(BF16),
                    kdec_t=(kf * jnp.exp(gtc - gcc)).T.astype(BF16),
                    sdec=jnp.exp(gtr)))

    for u in units:
        aq = _dot_nt(jnp.concatenate([u["kbb"], u["q"]], axis=0), u["k"])
        low = jnp.where(u["strict"], aq[:n] * u["decay"], 0.0)
        u["qk"] = (aq[n:] * u["decay"]).astype(BF16)
        u["tinv"] = eye - low
        u["lb"] = low.astype(BF16)

    for u in units:
        u["power"] = _dot(u["lb"], u["lb"])
    n_factors = (n - 1).bit_length() - 1
    for it in range(n_factors):
        for u in units:
            pb = u["power"].astype(BF16)
            if it + 1 < n_factors:
                both = _dot(jnp.concatenate([u["tinv"].astype(BF16), pb], axis=0), pb)
                u["tinv"] = u["tinv"] + both[:n]
                u["power"] = both[n:]
            else:
                u["tinv"] = u["tinv"] + _dot(u["tinv"].astype(BF16), pb)

    for u in units:
        u["uw"] = _dot(u["tinv"].astype(BF16), jnp.concatenate([u["vb"], u["kbg"]], axis=1))
    state = {(d, h): state_ref[d, h] for d in range(2) for h in range(GDN_HEADS)}
    for pos in range(GDN_BLOCK_CHUNKS):
        now = [u for u in units if u["pos"] == pos]
        for u in now:
            w = u["uw"][:, HEAD_DIM:].astype(BF16)
            u["ws"] = _dot(jnp.concatenate([w, u["qg"]], axis=0), state[u["d"], u["h"]].astype(BF16))
        for u in now:
            v_new = (u["uw"][:, :HEAD_DIM] - u["ws"][:n]).astype(BF16)
            u["os"] = _dot(jnp.concatenate([u["qk"], u["kdec_t"]], axis=0), v_new)
        for u in now:
            u["o_ref"][u["rows"], u["lanes"]] = u["ws"][n:] + u["os"][:n]
            state[u["d"], u["h"]] = state[u["d"], u["h"]] * u["sdec"] + u["os"][n:]
    for (d, h), value in state.items():
        state_ref[d, h] = value


def _gdn_call(q, k, v, gcol, grow):
    bsz, seq, _ = q.shape
    tb = GDN_CHUNK * GDN_BLOCK_CHUNKS
    nb = seq // tb
    qkv_spec_f = pl.BlockSpec((None, tb, GDN_WIDTH), lambda b, i: (b, i, 0))
    qkv_spec_b = pl.BlockSpec((None, tb, GDN_WIDTH), lambda b, i: (b, nb - 1 - i, 0))
    gc_spec_f = pl.BlockSpec((None, tb, LANES), lambda b, i: (b, i, 0))
    gc_spec_b = pl.BlockSpec((None, tb, LANES), lambda b, i: (b, nb - 1 - i, 0))
    gr_spec_f = pl.BlockSpec((None, LANES, tb), lambda b, i: (b, 0, i))
    gr_spec_b = pl.BlockSpec((None, LANES, tb), lambda b, i: (b, 0, nb - 1 - i))
    blocks = 2 * (3 * _nbytes((tb, GDN_WIDTH), BF16) + 2 * _nbytes((tb, LANES), F32) + _nbytes((tb, GDN_WIDTH), F32))
    scratch = _nbytes((2, GDN_HEADS, HEAD_DIM, HEAD_DIM), F32)
    temps = GDN_BLOCK_CHUNKS * 2 * GDN_HEADS * 24 * _nbytes((GDN_CHUNK, LANES), F32)
    return pl.pallas_call(
        _gdn_kernel,
        grid=(bsz, nb),
        in_specs=[qkv_spec_f, qkv_spec_f, qkv_spec_f, gc_spec_f, gr_spec_f,
                  qkv_spec_b, qkv_spec_b, qkv_spec_b, gc_spec_b, gr_spec_b],
        out_specs=[qkv_spec_f, qkv_spec_b],
        out_shape=[jax.ShapeDtypeStruct((bsz, seq, GDN_WIDTH), F32),
                   jax.ShapeDtypeStruct((bsz, seq, GDN_WIDTH), F32)],
        scratch_shapes=[pltpu.VMEM((2, GDN_HEADS, HEAD_DIM, HEAD_DIM), F32)],
        compiler_params=pltpu.CompilerParams(
            dimension_semantics=("parallel", "arbitrary"),
            vmem_limit_bytes=_vmem_limit(blocks, scratch, temps)),
        name="gdn",
    )(q, k, v, gcol, grow, q, k, v, gcol, grow)


def _rope_tables(seq):
    rows = seq // GRID_W
    half = HEAD_DIM // 2
    inv_freq = ROPE_THETA ** (-jnp.arange(0, half, 2, dtype=F32) / half)
    ang_r = jnp.arange(rows, dtype=F32)[:, None] * inv_freq
    ang_c = jnp.arange(GRID_W, dtype=F32)[:, None] * inv_freq

    def table(fn):
        tr = jnp.broadcast_to(fn(ang_r)[:, None, :], (rows, GRID_W, half // 2))
        tc = jnp.broadcast_to(fn(ang_c)[None, :, :], (rows, GRID_W, half // 2))
        return jnp.concatenate([tr, tr, tc, tc], -1).reshape(seq, HEAD_DIM)

    cos, sin = table(jnp.cos), table(jnp.sin)
    lane = jnp.arange(HEAD_DIM)
    first = (lane // (HEAD_DIM // 4)) % 2 == 0
    sin_up = jnp.where(first, -sin, 0.0)
    sin_dn = jnp.where(first, 0.0, sin)
    return cos, sin_up, sin_dn


def _norm_rope(x, w, cos, sin_up, sin_dn, scale):
    xn = x * lax.rsqrt(jnp.mean(x * x, axis=-1, keepdims=True) + NORM_EPS) * w
    quarter = HEAD_DIM // 4
    up = pltpu.roll(xn, HEAD_DIM - quarter, 1)
    dn = pltpu.roll(xn, quarter, 1)
    y = xn * cos + up * sin_up + dn * sin_dn
    return y * scale if scale != 1.0 else y


ATT_TK = 1024
FLASH_SUB = 256
VT_ROWS = HEAD_DIM + BF16_ROWS
LOG2_E = 1.4426950408889634


def _flash_kernel(qt_ref, qtn_ref, k_ref, vt_ref, o_ref, m_ref, acc_ref, qc_ref, qn_ref,
                  sa_ref, sb_ref, mca_ref, mcb_ref, *, tq, tk, ts, n_k):
    m_ref[...] = jnp.full_like(m_ref, -jnp.inf)
    acc_ref[...] = jnp.zeros_like(acc_ref)
    for g in range(ATT_GROUP):
        qc_ref[:, g * tq:(g + 1) * tq] = qt_ref[g * HEAD_DIM:(g + 1) * HEAD_DIM, :]
        qn_ref[:, g * tq:(g + 1) * tq] = qtn_ref[g * HEAD_DIM:(g + 1) * HEAD_DIM, :]

    n_sub = tk // ts
    bufs = ((sa_ref, mca_ref), (sb_ref, mcb_ref))

    def scores(j, q_ref, s_ref, i):
        start = pl.multiple_of(j * tk + i * ts, ts)
        st = _dot(k_ref[pl.ds(start, ts), :], q_ref[...])
        s_ref[i * ts:(i + 1) * ts, :] = st
        return jnp.max(st, axis=0, keepdims=True)

    def weigh(j, s_ref, i, m_new):
        p = jnp.exp2(s_ref[i * ts:(i + 1) * ts, :] - m_new)
        return _dot(vt_ref[j, :, i * ts:(i + 1) * ts], p.astype(BF16))

    def step(j, parity, j_next, q_next):
        s_cur, mc_cur = bufs[parity]
        s_nxt, mc_nxt = bufs[1 - parity]
        m_prev = m_ref[...]
        m_new = jnp.maximum(m_prev, mc_cur[...])
        alpha = jnp.exp2(m_prev - m_new)
        mc_next, pv = None, None
        for i in range(n_sub):
            mx = scores(j_next, q_next, s_nxt, i)
            mc_next = mx if mc_next is None else jnp.maximum(mc_next, mx)
            pvi = weigh(j, s_cur, i, m_new)
            pv = pvi if pv is None else pv + pvi
        mc_nxt[...] = mc_next
        acc_ref[...] = alpha * acc_ref[...] + pv
        m_ref[...] = m_new

    @pl.when(pl.program_id(2) == 0)
    def _():
        mc0 = None
        for i in range(n_sub):
            mx = scores(0, qc_ref, sa_ref, i)
            mc0 = mx if mc0 is None else jnp.maximum(mc0, mx)
        mca_ref[...] = mc0

    def body(jj, carry):
        step(2 * jj, 0, 2 * jj + 1, qc_ref)
        step(2 * jj + 1, 1, 2 * jj + 2, qc_ref)
        return carry

    lax.fori_loop(0, n_k // 2 - 1, body, 0)
    step(n_k - 2, 0, n_k - 1, qc_ref)
    step(n_k - 1, 1, 0, qn_ref)
    for g in range(ATT_GROUP):
        lanes = slice(g * tq, (g + 1) * tq)
        out_t = acc_ref[:HEAD_DIM, lanes] / acc_ref[HEAD_DIM:HEAD_DIM + 1, lanes]
        o_ref[:, g * HEAD_DIM:(g + 1) * HEAD_DIM] = out_t.T.astype(o_ref.dtype)


def _flash_call(qt, kr, vt):
    bsz, _, seq = qt.shape
    tq, tk = 256, ATT_TK
    gw = ATT_GROUP * HEAD_DIM
    nq = ATT_GROUP * tq
    n_q, n_k = seq // tq, seq // tk
    assert n_k % 2 == 0, "the score buffers alternate by key-tile parity across query tiles"
    blocks = 3 * _nbytes((gw, tq), BF16) + _nbytes((seq, HEAD_DIM), BF16) + _nbytes((seq, VT_ROWS), BF16)
    scratch = (3 * _nbytes((SUBLANES, nq), F32) + _nbytes((VT_ROWS, nq), F32) + 2 * _nbytes((HEAD_DIM, nq), BF16)
               + 2 * _nbytes((tk, nq), F32))
    temps = 2 * _nbytes((tk, nq), F32)
    return pl.pallas_call(
        functools.partial(_flash_kernel, tq=tq, tk=tk, ts=FLASH_SUB, n_k=n_k),
        grid=(bsz, ATT_KV_HEADS, n_q),
        in_specs=[
            pl.BlockSpec((None, gw, tq), lambda b, h, i: (b, h, i)),
            pl.BlockSpec((None, gw, tq), lambda b, h, i: (b, h, jnp.minimum(i + 1, n_q - 1))),
            pl.BlockSpec((None, seq, HEAD_DIM), lambda b, h, i: (b, 0, h)),
            pl.BlockSpec((None, None, n_k, VT_ROWS, tk), lambda b, h, i: (b, h, 0, 0, 0)),
        ],
        out_specs=pl.BlockSpec((None, tq, gw), lambda b, h, i: (b, i, h)),
        out_shape=jax.ShapeDtypeStruct((bsz, seq, ATT_WIDTH), BF16),
        scratch_shapes=[pltpu.VMEM((1, nq), F32),
                        pltpu.VMEM((VT_ROWS, nq), F32), pltpu.VMEM((HEAD_DIM, nq), BF16),
                        pltpu.VMEM((HEAD_DIM, nq), BF16),
                        pltpu.VMEM((tk, nq), F32), pltpu.VMEM((tk, nq), F32),
                        pltpu.VMEM((1, nq), F32), pltpu.VMEM((1, nq), F32)],
        compiler_params=pltpu.CompilerParams(
            dimension_semantics=("parallel", "parallel", "arbitrary"),
            vmem_limit_bytes=_vmem_limit(blocks, scratch, temps)),
        name="flash",
    )(qt, qt, kr, vt)


def _outproj_kernel(of_ref, ob_ref, z_ref, oa_ref, x_ref, mod_ref, w_ref, nw_ref, lg_ref, lb_ref, o_ref):
    nw = nw_ref[...]
    heads = []
    for h in range(GDN_HEADS):
        lanes = slice(h * HEAD_DIM, (h + 1) * HEAD_DIM)
        o = of_ref[:, lanes] + ob_ref[:, lanes]
        on = o * lax.rsqrt(jnp.mean(o * o, axis=-1, keepdims=True) + NORM_EPS) * nw
        heads.append((on * _silu(z_ref[:, lanes].astype(F32))).astype(BF16))
    og = jnp.concatenate(heads, axis=1)
    mix = _dot(og, w_ref[:GDN_WIDTH, :]) + _dot(oa_ref[...], w_ref[GDN_WIDTH:, :])
    m = mod_ref[...]
    y = DEEPNORM_ALPHA * x_ref[...] + m[MOD_G_M:MOD_G_M + 1, :] * mix
    o_ref[...] = _layer_norm_rows(y, lg_ref[...], lb_ref[...])


def _outproj_call(o_f, o_b, pm2, o_att, x2, mod, w_out, gdn_norm_w, ln_g, ln_b, seq):
    m = x2.shape[0]
    tm = 512
    blocks = (2 * _nbytes((tm, GDN_WIDTH), F32) + 2 * _nbytes((tm, GDN_WIDTH), BF16)
              + 2 * _nbytes((tm, D_MODEL), F32) + _nbytes((MOD_ROWS, D_MODEL), F32)
              + _nbytes((D_MODEL, D_MODEL), BF16))
    vec = pl.BlockSpec((1, D_MODEL), lambda i: (0, 0))
    return pl.pallas_call(
        _outproj_kernel,
        grid=(m // tm,),
        in_specs=[
            pl.BlockSpec((tm, GDN_WIDTH), lambda i: (i, 0)),
            pl.BlockSpec((tm, GDN_WIDTH), lambda i: (i, 0)),
            pl.BlockSpec((tm, GDN_WIDTH), lambda i: (i, PM_Z // GDN_WIDTH)),
            pl.BlockSpec((tm, ATT_WIDTH), lambda i: (i, 0)),
            pl.BlockSpec((tm, D_MODEL), lambda i: (i, 0)),
            pl.BlockSpec((None, MOD_ROWS, D_MODEL), lambda i: ((i * tm) // seq, 0, 0)),
            pl.BlockSpec((D_MODEL, D_MODEL), lambda i: (0, 0)),
            pl.BlockSpec((1, HEAD_DIM), lambda i: (0, 0)),
            vec, vec,
        ],
        out_specs=pl.BlockSpec((tm, D_MODEL), lambda i: (i, 0)),
        out_shape=jax.ShapeDtypeStruct((m, D_MODEL), F32),
        compiler_params=pltpu.CompilerParams(
            dimension_semantics=("parallel",),
            vmem_limit_bytes=_vmem_limit(blocks, temp_bytes=4 * _nbytes((tm, D_MODEL), F32))),
        name="outproj",
    )(o_f, o_b, pm2, o_att, x2, mod, w_out, gdn_norm_w.reshape(1, HEAD_DIM),
      ln_g.reshape(1, D_MODEL), ln_b.reshape(1, D_MODEL))


POOL_REACH = SUBLANES
assert max(POOL_WINDOWS) // 2 <= POOL_REACH


def _pool_kernel(x_ref, xp_ref, xn_ref, mod_ref, pw_ref, ps_ref, lg_ref, lb_ref, o_ref, *, tt, seq):
    t0 = pl.program_id(1) * tt
    m = mod_ref[...]
    sc = 1.0 + m[MOD_SC_M:MOD_SC_M + 1, :]
    sh = m[MOD_SH_M:MOD_SH_M + 1, :]
    x = x_ref[...]
    h_main = x * sc + sh
    h_prev = xp_ref[...] * sc + sh
    h_next = xn_ref[...] * sc + sh
    r = lax.broadcasted_iota(jnp.int32, (tt, tt), 0)
    c = lax.broadcasted_iota(jnp.int32, (tt, tt), 1)
    pos = t0 + lax.broadcasted_iota(jnp.int32, (tt, 1), 0)
    r8 = lax.broadcasted_iota(jnp.int32, (POOL_REACH, 1), 0)
    outs = []
    for gi, win in enumerate(POOL_WINDOWS):
        back, fwd = win // 2, win - 1 - win // 2
        lanes = slice(gi * POOL_GROUP, (gi + 1) * POOL_GROUP)
        band = jnp.where((c >= r - back) & (c <= r + fwd), 1.0, 0.0).astype(BF16)
        hm = h_main[:, lanes]
        wsum = _dot(band, hm.astype(BF16))
        top = jnp.zeros((POOL_REACH, POOL_GROUP), F32)
        bot = jnp.zeros((POOL_REACH, POOL_GROUP), F32)
        for k in range(POOL_REACH):
            if POOL_REACH - k <= back:
                use = (r8 <= k - POOL_REACH + back) & (t0 - POOL_REACH + k >= 0)
                top = top + jnp.where(use, 1.0, 0.0) * h_prev[k:k + 1, lanes]
            if k < fwd:
                use = (r8 >= k + POOL_REACH - fwd) & (t0 + tt + k < seq)
                bot = bot + jnp.where(use, 1.0, 0.0) * h_next[k:k + 1, lanes]
        wsum = jnp.concatenate([wsum[:POOL_REACH] + top, wsum[POOL_REACH:tt - POOL_REACH],
                                wsum[tt - POOL_REACH:] + bot], axis=0)
        cnt = (jnp.minimum(pos + fwd, seq - 1) - jnp.maximum(pos - back, 0) + 1).astype(F32)
        diff = wsum / cnt - hm
        outs.append(_dot(diff.astype(BF16), pw_ref[gi]))
    mix = jnp.concatenate(outs, axis=1) * ps_ref[...]
    y = DEEPNORM_ALPHA * x + m[MOD_G_M:MOD_G_M + 1, :] * mix
    o_ref[...] = _layer_norm_rows(y, lg_ref[...], lb_ref[...])


def _pool_call(x, mod, pool_w, pool_scale, ln_g, ln_b):
    bsz, seq, _ = x.shape
    tt = 256
    hb = tt // POOL_REACH
    n_halo = seq // POOL_REACH
    vec = pl.BlockSpec((1, D_MODEL), lambda b, i: (0, 0))
    blocks = (2 * _nbytes((tt, D_MODEL), F32) + 2 * _nbytes((POOL_REACH, D_MODEL), F32)
              + _nbytes((MOD_ROWS, D_MODEL), F32) + _nbytes(pool_w.shape, BF16))
    temps = 5 * _nbytes((tt, D_MODEL), F32)
    return pl.pallas_call(
        functools.partial(_pool_kernel, tt=tt, seq=seq),
        grid=(bsz, seq // tt),
        in_specs=[
            pl.BlockSpec((None, tt, D_MODEL), lambda b, i: (b, i, 0)),
            pl.BlockSpec((None, POOL_REACH, D_MODEL), lambda b, i: (b, jnp.maximum(i * hb - 1, 0), 0)),
            pl.BlockSpec((None, POOL_REACH, D_MODEL), lambda b, i: (b, jnp.minimum((i + 1) * hb, n_halo - 1), 0)),
            pl.BlockSpec((None, MOD_ROWS, D_MODEL), lambda b, i: (b, 0, 0)),
            pl.BlockSpec(pool_w.shape, lambda b, i: (0, 0, 0)),
            vec, vec, vec,
        ],
        out_specs=pl.BlockSpec((None, tt, D_MODEL), lambda b, i: (b, i, 0)),
        out_shape=jax.ShapeDtypeStruct((bsz, seq, D_MODEL), F32),
        compiler_params=pltpu.CompilerParams(
            dimension_semantics=("parallel", "parallel"),
            vmem_limit_bytes=_vmem_limit(blocks, temp_bytes=temps)),
        name="pool",
    )(x, x, x, mod, pool_w, pool_scale.reshape(1, D_MODEL), ln_g.reshape(1, D_MODEL), ln_b.reshape(1, D_MODEL))


def _mlp_kernel(x_ref, mod_ref, w1_ref, w2_ref, lg_ref, lb_ref, o_ref, h_ref, acc_ref):
    j = pl.program_id(1)

    @pl.when(j == 0)
    def _():
        m = mod_ref[...]
        h = x_ref[...] * (1.0 + m[MOD_SC_F:MOD_SC_F + 1, :]) + m[MOD_SH_F:MOD_SH_F + 1, :]
        h_ref[...] = h.astype(BF16)
        acc_ref[...] = jnp.zeros_like(acc_ref)

    u = jnp.maximum(_dot(h_ref[...], w1_ref[...]), 0.0)
    acc_ref[...] += _dot((u * u).astype(BF16), w2_ref[...])

    @pl.when(j == pl.num_programs(1) - 1)
    def _():
        m = mod_ref[...]
        y = DEEPNORM_ALPHA * x_ref[...] + m[MOD_G_F:MOD_G_F + 1, :] * acc_ref[...]
        o_ref[...] = _layer_norm_rows(y, lg_ref[...], lb_ref[...])


def _mlp_call(x2, mod, w1, w2, layer, ln_g, ln_b, seq):
    m = x2.shape[0]
    tm, tf = 512, 1024
    vec = pl.BlockSpec((1, D_MODEL), lambda i, j: (0, 0))
    blocks = (2 * _nbytes((tm, D_MODEL), F32) + _nbytes((MOD_ROWS, D_MODEL), F32)
              + 2 * _nbytes((D_MODEL, tf), BF16))
    scratch = _nbytes((tm, D_MODEL), BF16) + _nbytes((tm, D_MODEL), F32)
    temps = 2 * _nbytes((tm, tf), F32) + _nbytes((tm, D_MODEL), F32)
    return pl.pallas_call(
        _mlp_kernel,
        grid=(m // tm, D_FF // tf),
        in_specs=[
            pl.BlockSpec((tm, D_MODEL), lambda i, j: (i, 0)),
            pl.BlockSpec((None, MOD_ROWS, D_MODEL), lambda i, j: ((i * tm) // seq, 0, 0)),
            pl.BlockSpec((None, D_MODEL, tf), lambda i, j: (layer, 0, j)),
            pl.BlockSpec((None, tf, D_MODEL), lambda i, j: (layer, j, 0)),
            vec, vec,
        ],
        out_specs=pl.BlockSpec((tm, D_MODEL), lambda i, j: (i, 0)),
        out_shape=jax.ShapeDtypeStruct((m, D_MODEL), F32),
        scratch_shapes=[pltpu.VMEM((tm, D_MODEL), BF16), pltpu.VMEM((tm, D_MODEL), F32)],
        compiler_params=pltpu.CompilerParams(
            dimension_semantics=("parallel", "arbitrary"),
            vmem_limit_bytes=_vmem_limit(blocks, scratch, temps)),
        name="mlp",
    )(x2, mod, w1, w2, ln_g.reshape(1, D_MODEL), ln_b.reshape(1, D_MODEL))


def _mod_table(cond_rows):
    bsz = cond_rows.shape[0]
    t = cond_rows.reshape(bsz, 6, D_MODEL)
    return jnp.pad(t, ((0, 0), (0, MOD_ROWS - 6), (0, 0)))


def _pack_weights(w_in, conv_w, a_log, dt_bias, w_out, pool_w, mlp_w1, mlp_w2):
    wi = w_in[0]
    w_main = jnp.concatenate(
        [wi[:, :OFF_BETA], wi[:, OFF_AQ:]], axis=1).astype(BF16)
    w_gate = jnp.pad(wi[:, OFF_BETA:OFF_AQ], ((0, 0), (0, LANES - 2 * N_GATE_COLS))).astype(BF16)
    conv_w8 = jnp.pad(conv_w[0], ((0, SUBLANES - CONV_W), (0, 0)))
    gate_params = jnp.zeros((SUBLANES, LANES), F32)
    gate_params = gate_params.at[0, N_GATE_COLS:2 * N_GATE_COLS].set(dt_bias[0].reshape(-1))
    gate_params = gate_params.at[1, N_GATE_COLS:2 * N_GATE_COLS].set(a_log[0].reshape(-1))
    return dict(w_main=w_main, w_gate=w_gate, conv_w8=conv_w8, gate_params=gate_params,
                w_out=w_out[0].astype(BF16), pool_w=pool_w[0].astype(BF16),
                w1=mlp_w1.astype(BF16), w2=mlp_w2.astype(BF16))


def _trunk(x, cond, pk, gdn_norm_w, q_norm_w, k_norm_w, pool_scale, ln_g, ln_b):
    bsz, seq, _ = x.shape
    m = bsz * seq
    mod0 = _mod_table(cond[0])
    mod1 = _mod_table(cond[1])
    x2 = x.reshape(m, D_MODEL)

    cos, sin_up, sin_dn = _rope_tables(seq)
    pm2, gates2, qt, kr2, vt = _inproj_call(x2, mod0, pk["w_main"], pk["w_gate"], cos, sin_up, sin_dn,
                                            q_norm_w[0], k_norm_w[0], bsz, seq)
    pm = pm2.reshape(bsz, seq, PM_WIDTH)
    gq, gk, gv, gcol, grow = _gdnprep_call(pm, gates2.reshape(bsz, seq, LANES), pk["conv_w8"], pk["gate_params"])
    o_f, o_b = _gdn_call(gq, gk, gv, gcol, grow)
    o_att = _flash_call(qt, kr2.reshape(bsz, seq, ATT_KV_WIDTH), vt)
    x2 = _outproj_call(o_f.reshape(m, GDN_WIDTH), o_b.reshape(m, GDN_WIDTH), pm2, o_att.reshape(m, ATT_WIDTH),
                       x2, mod0, pk["w_out"], gdn_norm_w[0], ln_g[0, 0], ln_b[0, 0], seq)
    x2 = _mlp_call(x2, mod0, pk["w1"], pk["w2"], 0, ln_g[0, 1], ln_b[0, 1], seq)

    x3 = _pool_call(x2.reshape(bsz, seq, D_MODEL), mod1, pk["pool_w"], pool_scale[0], ln_g[1, 0], ln_b[1, 0])
    x2 = _mlp_call(x3.reshape(m, D_MODEL), mod1, pk["w1"], pk["w2"], 1, ln_g[1, 1], ln_b[1, 1], seq)
    return x2.reshape(bsz, seq, D_MODEL)


def kernel(x_prompt, x_sample, c_prompt, c_sample, w_in, conv_w, a_log, dt_bias, gdn_norm_w, q_norm_w,
           k_norm_w, w_out, pool_w, pool_scale, mlp_w1, mlp_w2, ada_w, ada_b, ln_g, ln_b):
    bp, bs = c_prompt.shape[0], c_sample.shape[0]
    c_all = jnp.concatenate([c_prompt, c_sample], axis=0)
    c_all = jnp.pad(c_all, ((0, (-c_all.shape[0]) % SUBLANES), (0, 0)))
    cond = _ada_call(c_all, ada_w, ada_b)
    pk = _pack_weights(w_in, conv_w, a_log, dt_bias, w_out, pool_w, mlp_w1, mlp_w2)
    y_prompt = _trunk(x_prompt, cond[:, :bp], pk, gdn_norm_w, q_norm_w, k_norm_w, pool_scale, ln_g, ln_b)
    y_sample = _trunk(x_sample, cond[:, bp:bp + bs], pk, gdn_norm_w, q_norm_w, k_norm_w, pool_scale, ln_g, ln_b)
    return (y_prompt, y_sample)
```

```python
import functools

import jax
import jax.numpy as jnp
from jax import lax
from jax.experimental import pallas as pl
from jax.experimental.pallas import tpu as pltpu

F32 = jnp.float32
BF16 = jnp.bfloat16

D_MODEL = 2048
DEPTH = 2
GRID_W = 64
HEAD_DIM = 128
GDN_HEADS = 8
ATT_HEADS = 8
ATT_KV_HEADS = 2
ATT_GROUP = ATT_HEADS // ATT_KV_HEADS
GDN_WIDTH = GDN_HEADS * HEAD_DIM
ATT_WIDTH = ATT_HEADS * HEAD_DIM
ATT_KV_WIDTH = ATT_KV_HEADS * HEAD_DIM
CONV_W = 5
ROPE_THETA = 10000.0
POOL_WINDOWS = (2, 4, 8, 16)
N_POOL_GROUPS = 4
POOL_GROUP = D_MODEL // N_POOL_GROUPS
D_FF = 4 * D_MODEL
DEEPNORM_ALPHA = (2 * DEPTH) ** 0.25
NORM_EPS = 1e-6
LN_EPS = 1e-5

GDN_QKV = 3 * GDN_WIDTH
OFF_Z = GDN_QKV
OFF_BETA = OFF_Z + GDN_WIDTH
OFF_A = OFF_BETA + 2 * GDN_HEADS
OFF_AQ = OFF_A + 2 * GDN_HEADS
OFF_AK = OFF_AQ + ATT_WIDTH
OFF_AV = OFF_AK + ATT_KV_WIDTH
D_IN = OFF_AV + ATT_KV_WIDTH

PM_QKV = 0
PM_Z = GDN_QKV
PM_AQ = PM_Z + GDN_WIDTH
PM_AK = PM_AQ + ATT_WIDTH
PM_AV = PM_AK + ATT_KV_WIDTH
PM_WIDTH = PM_AV + ATT_KV_WIDTH

LANES = 128
SUBLANES = 8
BF16_ROWS = 16
V7X_VMEM_BYTES = 64 * 1024 * 1024

GDN_CHUNK = 128
GDN_BLOCK_CHUNKS = 4
N_GATE_COLS = 2 * GDN_HEADS

MOD_SH_M, MOD_SC_M, MOD_G_M, MOD_SH_F, MOD_SC_F, MOD_G_F = range(6)
MOD_ROWS = 8


def _vmem_limit(block_bytes, scratch_bytes=0, temp_bytes=0):
    need = 2 * block_bytes + scratch_bytes + temp_bytes
    return int(min(need + need // 4, V7X_VMEM_BYTES - 8 * 1024 * 1024))


def _nbytes(shape, dtype):
    n = 1
    for s in shape:
        n *= s
    return n * jnp.dtype(dtype).itemsize


def _sigmoid(x):
    return 1.0 / (1.0 + jnp.exp(-x))


def _silu(x):
    return x * _sigmoid(x)


def _layer_norm_rows(y, g, b):
    mu = jnp.mean(y, axis=-1, keepdims=True)
    yc = y - mu
    var = jnp.mean(yc * yc, axis=-1, keepdims=True)
    return yc * lax.rsqrt(var + LN_EPS) * g + b


def _dot(a, b):
    return jnp.dot(a, b, preferred_element_type=F32)


def _dot_nt(a, b):
    return lax.dot_general(a, b, (((1,), (1,)), ((), ())), preferred_element_type=F32)


def _ada_kernel(c_ref, w_ref, b_ref, o_ref):
    c = c_ref[...]
    s = _silu(c).astype(BF16)
    o_ref[...] = _dot(s, w_ref[...].astype(BF16)) + b_ref[...]


def _ada_call(c_all, ada_w, ada_b):
    rows = c_all.shape[0]
    n = ada_w.shape[-1]
    tn = 1024
    blocks = _nbytes((D_MODEL, tn), F32) + _nbytes((rows, D_MODEL), F32) + 2 * _nbytes((rows, tn), F32)
    return pl.pallas_call(
        _ada_kernel,
        grid=(DEPTH, n // tn),
        in_specs=[
            pl.BlockSpec((rows, D_MODEL), lambda l, j: (0, 0)),
            pl.BlockSpec((None, D_MODEL, tn), lambda l, j: (l, 0, j)),
            pl.BlockSpec((None, 1, tn), lambda l, j: (l, 0, j)),
        ],
        out_specs=pl.BlockSpec((None, rows, tn), lambda l, j: (l, 0, j)),
        out_shape=jax.ShapeDtypeStruct((DEPTH, rows, n), F32),
        compiler_params=pltpu.CompilerParams(
            dimension_semantics=("parallel", "parallel"),
            vmem_limit_bytes=_vmem_limit(blocks, temp_bytes=_nbytes((D_MODEL, tn), BF16))),
        name="ada",
    )(c_all, ada_w, ada_b.reshape(DEPTH, 1, n))


INPROJ_TN = PM_WIDTH // 2
ATT_COL0 = PM_AQ - INPROJ_TN
assert PM_AQ >= INPROJ_TN and ATT_COL0 % LANES == 0


def _inproj_kernel(x_ref, mod_ref, w_ref, wg_ref, cos_ref, su_ref, sd_ref, qw_ref, kw_ref,
                   o_ref, g_ref, qt_ref, k_ref, vt_ref, h_ref):
    j = pl.program_id(1)

    @pl.when(j == 0)
    def _():
        m = mod_ref[...]
        h = x_ref[...] * (1.0 + m[MOD_SC_M:MOD_SC_M + 1, :]) + m[MOD_SH_M:MOD_SH_M + 1, :]
        hb = h.astype(BF16)
        h_ref[...] = hb
        g_ref[...] = _dot(hb, wg_ref[...])
        o_ref[...] = _dot(hb, w_ref[...]).astype(o_ref.dtype)

    @pl.when(j == 1)
    def _():
        hb = h_ref[...]
        att = _dot(hb, w_ref[:, ATT_COL0:])
        o_ref[:, :ATT_COL0] = _dot(hb, w_ref[:, :ATT_COL0]).astype(o_ref.dtype)
        o_ref[:, ATT_COL0:] = att.astype(o_ref.dtype)
        cos, su, sd = cos_ref[...], su_ref[...], sd_ref[...]
        qw, kw = qw_ref[...], kw_ref[...]
        for hd in range(ATT_HEADS):
            lanes = slice(hd * HEAD_DIM, (hd + 1) * HEAD_DIM)
            q = _norm_rope(att[:, lanes], qw, cos, su, sd, HEAD_DIM ** -0.5 * LOG2_E)
            qt_ref[lanes, :] = q.T.astype(qt_ref.dtype)
        for hd in range(ATT_KV_HEADS):
            lanes = slice(hd * HEAD_DIM, (hd + 1) * HEAD_DIM)
            ak = att[:, ATT_WIDTH + hd * HEAD_DIM:ATT_WIDTH + (hd + 1) * HEAD_DIM]
            av = att[:, ATT_WIDTH + ATT_KV_WIDTH + hd * HEAD_DIM:ATT_WIDTH + ATT_KV_WIDTH + (hd + 1) * HEAD_DIM]
            k_ref[:, lanes] = _norm_rope(ak, kw, cos, su, sd, 1.0).astype(k_ref.dtype)
            vt_ref[hd, :HEAD_DIM, :] = av.T.astype(vt_ref.dtype)
            vt_ref[hd, HEAD_DIM:, :] = jnp.ones((BF16_ROWS, vt_ref.shape[-1]), vt_ref.dtype)


def _inproj_call(x2, mod, w_main, w_gate, cos, sin_up, sin_dn, q_norm_w, k_norm_w, bsz, seq):
    m = x2.shape[0]
    tm, tn = 512, INPROJ_TN
    nt = seq // tm
    per_key_tile = ATT_TK // tm
    assert PM_WIDTH == 2 * tn and ATT_TK % tm == 0
    tab = pl.BlockSpec((tm, HEAD_DIM), lambda i, j: (i % nt, 0))
    vec = pl.BlockSpec((1, HEAD_DIM), lambda i, j: (0, 0))
    blocks = (_nbytes((tm, D_MODEL), F32) + _nbytes((MOD_ROWS, D_MODEL), F32) + _nbytes((D_MODEL, tn), BF16)
              + _nbytes((D_MODEL, LANES), BF16) + _nbytes((tm, tn), BF16) + _nbytes((tm, LANES), F32)
              + 3 * _nbytes((tm, HEAD_DIM), F32) + _nbytes((ATT_WIDTH, tm), BF16)
              + _nbytes((tm, ATT_KV_WIDTH), BF16) + _nbytes((ATT_KV_HEADS, VT_ROWS, tm), BF16))
    scratch = _nbytes((tm, D_MODEL), BF16)
    temps = _nbytes((tm, D_MODEL), F32) + 2 * _nbytes((tm, tn), F32)
    return pl.pallas_call(
        _inproj_kernel,
        grid=(m // tm, PM_WIDTH // tn),
        in_specs=[
            pl.BlockSpec((tm, D_MODEL), lambda i, j: (i, 0)),
            pl.BlockSpec((None, MOD_ROWS, D_MODEL), lambda i, j: (i // nt, 0, 0)),
            pl.BlockSpec((D_MODEL, tn), lambda i, j: (0, j)),
            pl.BlockSpec((D_MODEL, LANES), lambda i, j: (0, 0)),
            tab, tab, tab, vec, vec,
        ],
        out_specs=[
            pl.BlockSpec((tm, tn), lambda i, j: (i, j)),
            pl.BlockSpec((tm, LANES), lambda i, j: (i, 0)),
            pl.BlockSpec((None, ATT_WIDTH, tm), lambda i, j: (i // nt, 0, i % nt)),
            pl.BlockSpec((tm, ATT_KV_WIDTH), lambda i, j: (i, 0)),
            pl.BlockSpec((None, ATT_KV_HEADS, None, VT_ROWS, tm),
                         lambda i, j: (i // nt, 0, (i % nt) // per_key_tile, 0, i % per_key_tile)),
        ],
        out_shape=[
            jax.ShapeDtypeStruct((m, PM_WIDTH), BF16),
            jax.ShapeDtypeStruct((m, LANES), F32),
            jax.ShapeDtypeStruct((bsz, ATT_WIDTH, seq), BF16),
            jax.ShapeDtypeStruct((m, ATT_KV_WIDTH), BF16),
            jax.ShapeDtypeStruct((bsz, ATT_KV_HEADS, seq // ATT_TK, VT_ROWS, ATT_TK), BF16),
        ],
        scratch_shapes=[pltpu.VMEM((tm, D_MODEL), BF16)],
        compiler_params=pltpu.CompilerParams(
            dimension_semantics=("parallel", "arbitrary"),
            vmem_limit_bytes=_vmem_limit(blocks, scratch, temps)),
        name="inproj",
    )(x2, mod, w_main, w_gate, cos, sin_up, sin_dn, q_norm_w.reshape(1, HEAD_DIM), k_norm_w.reshape(1, HEAD_DIM))


def _gdnprep_kernel(x_ref, xp_ref, xn_ref, gate_ref, cw_ref, gp_ref,
                    q_ref, k_ref, v_ref, gcol_ref, grow_ref, *, tt):
    i = pl.program_id(1)
    first = i == 0
    last = i == pl.num_programs(1) - 1
    cw = cw_ref[...]
    sr = lax.broadcasted_iota(jnp.int32, (tt, tt), 0)
    sc = lax.broadcasted_iota(jnp.int32, (tt, tt), 1)
    shifts = [tap - CONV_W // 2 for tap in range(CONV_W)]
    shift_mats = {s: jnp.where(sc == sr + s, 1.0, 0.0).astype(BF16) for s in shifts if s != 0}
    cw_chunk = 2 * HEAD_DIM
    r8 = lax.broadcasted_iota(jnp.int32, (SUBLANES, cw_chunk), 0)
    for part, out_ref in enumerate((q_ref, k_ref, v_ref)):
        for ch in range(GDN_WIDTH // cw_chunk):
            cols = slice(part * GDN_WIDTH + ch * cw_chunk, part * GDN_WIDTH + (ch + 1) * cw_chunk)
            xb = x_ref[:, cols]
            prev8 = xp_ref[:, cols].astype(F32)[BF16_ROWS - SUBLANES:]
            next8 = xn_ref[:, cols].astype(F32)[:SUBLANES]
            prev8 = jnp.where(first, 0.0, prev8)
            next8 = jnp.where(last, 0.0, next8)
            w = cw[:, cols]
            acc = xb.astype(F32) * w[CONV_W // 2:CONV_W // 2 + 1, :]
            top = jnp.zeros((SUBLANES, cw_chunk), F32)
            bot = jnp.zeros((SUBLANES, cw_chunk), F32)
            for tap, s in enumerate(shifts):
                if s == 0:
                    continue
                wt = w[tap:tap + 1, :]
                acc = acc + _dot(shift_mats[s], xb) * wt
                if s < 0:
                    top = top + jnp.where(r8 < -s, pltpu.roll(prev8, -s, 0), 0.0) * wt
                else:
                    bot = bot + jnp.where(r8 >= SUBLANES - s, pltpu.roll(next8, SUBLANES - s, 0), 0.0) * wt
            acc = jnp.concatenate([acc[:SUBLANES] + top, acc[SUBLANES:tt - SUBLANES],
                                   acc[tt - SUBLANES:] + bot], axis=0)
            y = _silu(acc)
            if part < 2:
                heads = []
                for h in range(cw_chunk // HEAD_DIM):
                    yh = y[:, h * HEAD_DIM:(h + 1) * HEAD_DIM]
                    inv = lax.rsqrt(jnp.sum(yh * yh, axis=-1, keepdims=True) + NORM_EPS)
                    if part == 0:
                        inv = inv * (HEAD_DIM ** -0.5)
                    heads.append(yh * inv)
                y = jnp.concatenate(heads, axis=1)
            out_ref[:, ch * cw_chunk:(ch + 1) * cw_chunk] = y.astype(out_ref.dtype)

    raw = gate_ref[...]
    gp = gp_ref[...]
    col = lax.broadcasted_iota(jnp.int32, raw.shape, 1)
    beta = _sigmoid(raw)
    z = raw + gp[0:1, :]
    softplus = jnp.maximum(z, 0.0) + jnp.log(1.0 + jnp.exp(-jnp.abs(z)))
    logdec = -jnp.exp(gp[1:2, :]) * softplus
    is_dec = (col >= N_GATE_COLS) & (col < 2 * N_GATE_COLS)
    gsrc = jnp.where(is_dec, logdec, 0.0)
    r = lax.broadcasted_iota(jnp.int32, (tt, tt), 0)
    c = lax.broadcasted_iota(jnp.int32, (tt, tt), 1)
    same = (r // GDN_CHUNK) == (c // GDN_CHUNK)
    p_lo = jnp.where(same & (c <= r), 1.0, 0.0).astype(F32)
    p_up = jnp.where(same & (c >= r), 1.0, 0.0).astype(F32)
    cum_lo = jnp.dot(p_lo, gsrc, precision=lax.Precision.HIGHEST, preferred_element_type=F32)
    cum_up = jnp.dot(p_up, gsrc, precision=lax.Precision.HIGHEST, preferred_element_type=F32)
    total = cum_lo + cum_up - gsrc
    fwd_col = col < N_GATE_COLS + GDN_HEADS
    gc = jnp.where(fwd_col, cum_lo, cum_up)
    tot_shift = pltpu.roll(total, N_GATE_COLS, 1)
    gcol = jnp.where(col < N_GATE_COLS, beta,
                     jnp.where(col < 2 * N_GATE_COLS, gc,
                               jnp.where(col < 3 * N_GATE_COLS, tot_shift, 0.0)))
    gcol_ref[...] = gcol
    grow_ref[...] = gcol.T


def _gdnprep_call(pm, gates, conv_w8, gate_params):
    bsz, seq, _ = pm.shape
    tt = 256
    hb = tt // BF16_ROWS
    n_halo = seq // BF16_ROWS
    blocks = (_nbytes((tt, GDN_QKV), BF16) + 2 * _nbytes((BF16_ROWS, GDN_QKV), BF16) + _nbytes((tt, LANES), F32)
              + _nbytes((SUBLANES, GDN_QKV), F32) + 3 * _nbytes((tt, GDN_WIDTH), BF16) + 2 * _nbytes((tt, LANES), F32))
    temps = 8 * _nbytes((tt, GDN_WIDTH), F32) + 4 * _nbytes((tt, tt), F32)
    return pl.pallas_call(
        functools.partial(_gdnprep_kernel, tt=tt),
        grid=(bsz, seq // tt),
        in_specs=[
            pl.BlockSpec((None, tt, GDN_QKV), lambda b, i: (b, i, 0)),
            pl.BlockSpec((None, BF16_ROWS, GDN_QKV), lambda b, i: (b, jnp.maximum(i * hb - 1, 0), 0)),
            pl.BlockSpec((None, BF16_ROWS, GDN_QKV), lambda b, i: (b, jnp.minimum((i + 1) * hb, n_halo - 1), 0)),
            pl.BlockSpec((None, tt, LANES), lambda b, i: (b, i, 0)),
            pl.BlockSpec((SUBLANES, GDN_QKV), lambda b, i: (0, 0)),
            pl.BlockSpec((SUBLANES, LANES), lambda b, i: (0, 0)),
        ],
        out_specs=[
            pl.BlockSpec((None, tt, GDN_WIDTH), lambda b, i: (b, i, 0)),
            pl.BlockSpec((None, tt, GDN_WIDTH), lambda b, i: (b, i, 0)),
            pl.BlockSpec((None, tt, GDN_WIDTH), lambda b, i: (b, i, 0)),
            pl.BlockSpec((None, tt, LANES), lambda b, i: (b, i, 0)),
            pl.BlockSpec((None, LANES, tt), lambda b, i: (b, 0, i)),
        ],
        out_shape=[
            jax.ShapeDtypeStruct((bsz, seq, GDN_WIDTH), BF16),
            jax.ShapeDtypeStruct((bsz, seq, GDN_WIDTH), BF16),
            jax.ShapeDtypeStruct((bsz, seq, GDN_WIDTH), BF16),
            jax.ShapeDtypeStruct((bsz, seq, LANES), F32),
            jax.ShapeDtypeStruct((bsz, LANES, seq), F32),
        ],
        compiler_params=pltpu.CompilerParams(
            dimension_semantics=("parallel", "parallel"),
            vmem_limit_bytes=_vmem_limit(blocks, temp_bytes=temps)),
        name="gdnprep",
    )(pm, pm, pm, gates, conv_w8, gate_params)


def _gdn_kernel(qf_ref, kf_ref, vf_ref, gcf_ref, grf_ref,
                qb_ref, kb_ref, vb_ref, gcb_ref, grb_ref,
                of_ref, ob_ref, state_ref):
    @pl.when(pl.program_id(1) == 0)
    def _():
        state_ref[...] = jnp.zeros_like(state_ref)

    n = GDN_CHUNK
    r = lax.broadcasted_iota(jnp.int32, (n, n), 0)
    c = lax.broadcasted_iota(jnp.int32, (n, n), 1)
    eye = jnp.where(r == c, 1.0, 0.0).astype(F32)

    units = []
    for d, (q_ref, k_ref, v_ref, gc_ref, gr_ref, o_ref) in enumerate((
            (qf_ref, kf_ref, vf_ref, gcf_ref, grf_ref, of_ref),
            (qb_ref, kb_ref, vb_ref, gcb_ref, grb_ref, ob_ref))):
        gcol = gc_ref[...]
        grow = gr_ref[...]
        reverse = d == 1
        incl = (r <= c) if reverse else (r >= c)
        strict = (r < c) if reverse else (r > c)
        for pos in range(GDN_BLOCK_CHUNKS):
            chunk = GDN_BLOCK_CHUNKS - 1 - pos if reverse else pos
            rows = slice(chunk * n, (chunk + 1) * n)
            for h in range(GDN_HEADS):
                lanes = slice(h * HEAD_DIM, (h + 1) * HEAD_DIM)
                j = d * GDN_HEADS + h
                beta = gcol[rows, j:j + 1]
                gcc = gcol[rows, N_GATE_COLS + j:N_GATE_COLS + j + 1]
                gtc = gcol[rows, 2 * N_GATE_COLS + j:2 * N_GATE_COLS + j + 1]
                gcr = grow[N_GATE_COLS + j:N_GATE_COLS + j + 1, rows]
                gtr = grow[2 * N_GATE_COLS + j:2 * N_GATE_COLS + j + 1, rows]
                q = q_ref[rows, lanes]
                k = k_ref[rows, lanes]
                kf = k.astype(F32)
                kb = kf * beta
                egc = jnp.exp(gcc)
                units.append(dict(
                    d=d, h=h, pos=pos, rows=rows, lanes=lanes, o_ref=o_ref, strict=strict, q=q, k=k,
                    decay=jnp.where(incl, jnp.exp(jnp.where(incl, gcc - gcr, 0.0)), 0.0),
                    kbb=kb.astype(BF16),
                    vb=(v_ref[rows, lanes].astype(F32) * beta).astype(BF16),
                    kbg=(kb * egc).astype(BF16),
                    qg=(q.astype(F32) * egc).astype(BF16),
                    kdec_t=(kf * jnp.exp(gtc - gcc)).T.astype(BF16),
                    sdec=jnp.exp(gtr)))

    for u in units:
        aq = _dot_nt(jnp.concatenate([u["kbb"], u["q"]], axis=0), u["k"])
        low = jnp.where(u["strict"], aq[:n] * u["decay"], 0.0)
        u["qk"] = (aq[n:] * u["decay"]).astype(BF16)
        u["tinv"] = eye - low
        u["lb"] = low.astype(BF16)

    for u in units:
        u["power"] = _dot(u["lb"], u["lb"])
    n_factors = (n - 1).bit_length() - 1
    for it in range(n_factors):
        for u in units:
            pb = u["power"].astype(BF16)
            if it + 1 < n_factors:
                both = _dot(jnp.concatenate([u["tinv"].astype(BF16), pb], axis=0), pb)
                u["tinv"] = u["tinv"] + both[:n]
                u["power"] = both[n:]
            else:
                u["tinv"] = u["tinv"] + _dot(u["tinv"].astype(BF16), pb)

    for u in units:
        u["uw"] = _dot(u["tinv"].astype(BF16), jnp.concatenate([u["vb"], u["kbg"]], axis=1))
    state = {(d, h): state_ref[d, h] for d in range(2) for h in range(GDN_HEADS)}
    for pos in range(GDN_BLOCK_CHUNKS):
        now = [u for u in units if u["pos"] == pos]
        for u in now:
            w = u["uw"][:, HEAD_DIM:].astype(BF16)
            u["ws"] = _dot(jnp.concatenate([w, u["qg"]], axis=0), state[u["d"], u["h"]].astype(BF16))
        for u in now:
            v_new = (u["uw"][:, :HEAD_DIM] - u["ws"][:n]).astype(BF16)
            u["os"] = _dot(jnp.concatenate([u["qk"], u["kdec_t"]], axis=0), v_new)
        for u in now:
            u["o_ref"][u["rows"], u["lanes"]] = u["ws"][n:] + u["os"][:n]
            state[u["d"], u["h"]] = state[u["d"], u["h"]] * u["sdec"] + u["os"][n:]
    for (d, h), value in state.items():
        state_ref[d, h] = value


def _gdn_call(q, k, v, gcol, grow):
    bsz, seq, _ = q.shape
    tb = GDN_CHUNK * GDN_BLOCK_CHUNKS
    nb = seq // tb
    qkv_spec_f = pl.BlockSpec((None, tb, GDN_WIDTH), lambda b, i: (b, i, 0))
    qkv_spec_b = pl.BlockSpec((None, tb, GDN_WIDTH), lambda b, i: (b, nb - 1 - i, 0))
    gc_spec_f = pl.BlockSpec((None, tb, LANES), lambda b, i: (b, i, 0))
    gc_spec_b = pl.BlockSpec((None, tb, LANES), lambda b, i: (b, nb - 1 - i, 0))
    gr_spec_f = pl.BlockSpec((None, LANES, tb), lambda b, i: (b, 0, i))
    gr_spec_b = pl.BlockSpec((None, LANES, tb), lambda b, i: (b, 0, nb - 1 - i))
    blocks = 2 * (3 * _nbytes((tb, GDN_WIDTH), BF16) + 2 * _nbytes((tb, LANES), F32) + _nbytes((tb, GDN_WIDTH), F32))
    scratch = _nbytes((2, GDN_HEADS, HEAD_DIM, HEAD_DIM), F32)
    temps = GDN_BLOCK_CHUNKS * 2 * GDN_HEADS * 24 * _nbytes((GDN_CHUNK, LANES), F32)
    return pl.pallas_call(
        _gdn_kernel,
        grid=(bsz, nb),
        in_specs=[qkv_spec_f, qkv_spec_f, qkv_spec_f, gc_spec_f, gr_spec_f,
                  qkv_spec_b, qkv_spec_b, qkv_spec_b, gc_spec_b, gr_spec_b],
        out_specs=[qkv_spec_f, qkv_spec_b],
        out_shape=[jax.ShapeDtypeStruct((bsz, seq, GDN_WIDTH), F32),
                   jax.ShapeDtypeStruct((bsz, seq, GDN_WIDTH), F32)],
        scratch_shapes=[pltpu.VMEM((2, GDN_HEADS, HEAD_DIM, HEAD_DIM), F32)],
        compiler_params=pltpu.CompilerParams(
            dimension_semantics=("parallel", "arbitrary"),
            vmem_limit_bytes=_vmem_limit(blocks, scratch, temps)),
        name="gdn",
    )(q, k, v, gcol, grow, q, k, v, gcol, grow)


def _rope_tables(seq):
    rows = seq // GRID_W
    half = HEAD_DIM // 2
    inv_freq = ROPE_THETA ** (-jnp.arange(0, half, 2, dtype=F32) / half)
    ang_r = jnp.arange(rows, dtype=F32)[:, None] * inv_freq
    ang_c = jnp.arange(GRID_W, dtype=F32)[:, None] * inv_freq

    def table(fn):
        tr = jnp.broadcast_to(fn(ang_r)[:, None, :], (rows, GRID_W, half // 2))
        tc = jnp.broadcast_to(fn(ang_c)[None, :, :], (rows, GRID_W, half // 2))
        return jnp.concatenate([tr, tr, tc, tc], -1).reshape(seq, HEAD_DIM)

    cos, sin = table(jnp.cos), table(jnp.sin)
    lane = jnp.arange(HEAD_DIM)
    first = (lane // (HEAD_DIM // 4)) % 2 == 0
    sin_up = jnp.where(first, -sin, 0.0)
    sin_dn = jnp.where(first, 0.0, sin)
    return cos, sin_up, sin_dn


def _norm_rope(x, w, cos, sin_up, sin_dn, scale):
    xn = x * lax.rsqrt(jnp.mean(x * x, axis=-1, keepdims=True) + NORM_EPS) * w
    quarter = HEAD_DIM // 4
    up = pltpu.roll(xn, HEAD_DIM - quarter, 1)
    dn = pltpu.roll(xn, quarter, 1)
    y = xn * cos + up * sin_up + dn * sin_dn
    return y * scale if scale != 1.0 else y


ATT_TK = 2048
FLASH_SUB = 256
VT_ROWS = HEAD_DIM + BF16_ROWS
LOG2_E = 1.4426950408889634


def _flash_kernel(qt_ref, qtn_ref, k_ref, vt_ref, o_ref, m_ref, acc_ref, qc_ref, qn_ref,
                  sa_ref, sb_ref, mca_ref, mcb_ref, *, tq, tk, ts, n_k):
    m_ref[...] = jnp.full_like(m_ref, -jnp.inf)
    acc_ref[...] = jnp.zeros_like(acc_ref)
    for g in range(ATT_GROUP):
        qc_ref[:, g * tq:(g + 1) * tq] = qt_ref[g * HEAD_DIM:(g + 1) * HEAD_DIM, :]
        qn_ref[:, g * tq:(g + 1) * tq] = qtn_ref[g * HEAD_DIM:(g + 1) * HEAD_DIM, :]

    n_sub = tk // ts
    bufs = ((sa_ref, mca_ref), (sb_ref, mcb_ref))

    def scores(j, q_ref, s_ref, i):
        start = pl.multiple_of(j * tk + i * ts, ts)
        st = _dot(k_ref[pl.ds(start, ts), :], q_ref[...])
        s_ref[i * ts:(i + 1) * ts, :] = st
        return jnp.max(st, axis=0, keepdims=True)

    def weigh(j, s_ref, i, m_new):
        p = jnp.exp2(s_ref[i * ts:(i + 1) * ts, :] - m_new)
        return _dot(vt_ref[j, :, i * ts:(i + 1) * ts], p.astype(BF16))

    def step(j, parity, j_next, q_next):
        s_cur, mc_cur = bufs[parity]
        s_nxt, mc_nxt = bufs[1 - parity]
        m_prev = m_ref[...]
        m_new = jnp.maximum(m_prev, mc_cur[...])
        alpha = jnp.exp2(m_prev - m_new)
        mc_next, pv = None, None
        for i in range(n_sub):
            mx = scores(j_next, q_next, s_nxt, i)
            mc_next = mx if mc_next is None else jnp.maximum(mc_next, mx)
            pvi = weigh(j, s_cur, i, m_new)
            pv = pvi if pv is None else pv + pvi
        mc_nxt[...] = mc_next
        acc_ref[...] = alpha * acc_ref[...] + pv
        m_ref[...] = m_new

    @pl.when(pl.program_id(2) == 0)
    def _():
        mc0 = None
        for i in range(n_sub):
            mx = scores(0, qc_ref, sa_ref, i)
            mc0 = mx if mc0 is None else jnp.maximum(mc0, mx)
        mca_ref[...] = mc0

    def body(jj, carry):
        step(2 * jj, 0, 2 * jj + 1, qc_ref)
        step(2 * jj + 1, 1, 2 * jj + 2, qc_ref)
        return carry

    lax.fori_loop(0, n_k // 2 - 1, body, 0)
    step(n_k - 2, 0, n_k - 1, qc_ref)
    step(n_k - 1, 1, 0, qn_ref)
    for g in range(ATT_GROUP):
        lanes = slice(g * tq, (g + 1) * tq)
        out_t = acc_ref[:HEAD_DIM, lanes] / acc_ref[HEAD_DIM:HEAD_DIM + 1, lanes]
        o_ref[:, g * HEAD_DIM:(g + 1) * HEAD_DIM] = out_t.T.astype(o_ref.dtype)


def _flash_call(qt, kr, vt):
    bsz, _, seq = qt.shape
    tq, tk = 256, ATT_TK
    gw = ATT_GROUP * HEAD_DIM
    nq = ATT_GROUP * tq
    n_q, n_k = seq // tq, seq // tk
    assert n_k % 2 == 0, "the score buffers alternate by key-tile parity across query tiles"
    blocks = 3 * _nbytes((gw, tq), BF16) + _nbytes((seq, HEAD_DIM), BF16) + _nbytes((seq, VT_ROWS), BF16)
    scratch = (3 * _nbytes((SUBLANES, nq), F32) + _nbytes((VT_ROWS, nq), F32) + 2 * _nbytes((HEAD_DIM, nq), BF16)
               + 2 * _nbytes((tk, nq), F32))
    temps = 2 * _nbytes((tk, nq), F32)
    return pl.pallas_call(
        functools.partial(_flash_kernel, tq=tq, tk=tk, ts=FLASH_SUB, n_k=n_k),
        grid=(bsz, ATT_KV_HEADS, n_q),
        in_specs=[
            pl.BlockSpec((None, gw, tq), lambda b, h, i: (b, h, i)),
            pl.BlockSpec((None, gw, tq), lambda b, h, i: (b, h, jnp.minimum(i + 1, n_q - 1))),
            pl.BlockSpec((None, seq, HEAD_DIM), lambda b, h, i: (b, 0, h)),
            pl.BlockSpec((None, None, n_k, VT_ROWS, tk), lambda b, h, i: (b, h, 0, 0, 0)),
        ],
        out_specs=pl.BlockSpec((None, tq, gw), lambda b, h, i: (b, i, h)),
        out_shape=jax.ShapeDtypeStruct((bsz, seq, ATT_WIDTH), BF16),
        scratch_shapes=[pltpu.VMEM((1, nq), F32),
                        pltpu.VMEM((VT_ROWS, nq), F32), pltpu.VMEM((HEAD_DIM, nq), BF16),
                        pltpu.VMEM((HEAD_DIM, nq), BF16),
                        pltpu.VMEM((tk, nq), F32), pltpu.VMEM((tk, nq), F32),
                        pltpu.VMEM((1, nq), F32), pltpu.VMEM((1, nq), F32)],
        compiler_params=pltpu.CompilerParams(
            dimension_semantics=("parallel", "parallel", "arbitrary"),
            vmem_limit_bytes=_vmem_limit(blocks, scratch, temps)),
        name="flash",
    )(qt, qt, kr, vt)


def _outproj_kernel(of_ref, ob_ref, z_ref, oa_ref, x_ref, mod_ref, w_ref, nw_ref, lg_ref, lb_ref, o_ref):
    nw = nw_ref[...]
    heads = []
    for h in range(GDN_HEADS):
        lanes = slice(h * HEAD_DIM, (h + 1) * HEAD_DIM)
        o = of_ref[:, lanes] + ob_ref[:, lanes]
        on = o * lax.rsqrt(jnp.mean(o * o, axis=-1, keepdims=True) + NORM_EPS) * nw
        heads.append((on * _silu(z_ref[:, lanes].astype(F32))).astype(BF16))
    og = jnp.concatenate(heads, axis=1)
    mix = _dot(og, w_ref[:GDN_WIDTH, :]) + _dot(oa_ref[...], w_ref[GDN_WIDTH:, :])
    m = mod_ref[...]
    y = DEEPNORM_ALPHA * x_ref[...] + m[MOD_G_M:MOD_G_M + 1, :] * mix
    o_ref[...] = _layer_norm_rows(y, lg_ref[...], lb_ref[...])


def _outproj_call(o_f, o_b, pm2, o_att, x2, mod, w_out, gdn_norm_w, ln_g, ln_b, seq):
    m = x2.shape[0]
    tm = 512
    blocks = (2 * _nbytes((tm, GDN_WIDTH), F32) + 2 * _nbytes((tm, GDN_WIDTH), BF16)
              + 2 * _nbytes((tm, D_MODEL), F32) + _nbytes((MOD_ROWS, D_MODEL), F32)
              + _nbytes((D_MODEL, D_MODEL), BF16))
    vec = pl.BlockSpec((1, D_MODEL), lambda i: (0, 0))
    return pl.pallas_call(
        _outproj_kernel,
        grid=(m // tm,),
        in_specs=[
            pl.BlockSpec((tm, GDN_WIDTH), lambda i: (i, 0)),
            pl.BlockSpec((tm, GDN_WIDTH), lambda i: (i, 0)),
            pl.BlockSpec((tm, GDN_WIDTH), lambda i: (i, PM_Z // GDN_WIDTH)),
            pl.BlockSpec((tm, ATT_WIDTH), lambda i: (i, 0)),
            pl.BlockSpec((tm, D_MODEL), lambda i: (i, 0)),
            pl.BlockSpec((None, MOD_ROWS, D_MODEL), lambda i: ((i * tm) // seq, 0, 0)),
            pl.BlockSpec((D_MODEL, D_MODEL), lambda i: (0, 0)),
            pl.BlockSpec((1, HEAD_DIM), lambda i: (0, 0)),
            vec, vec,
        ],
        out_specs=pl.BlockSpec((tm, D_MODEL), lambda i: (i, 0)),
        out_shape=jax.ShapeDtypeStruct((m, D_MODEL), F32),
        compiler_params=pltpu.CompilerParams(
            dimension_semantics=("parallel",),
            vmem_limit_bytes=_vmem_limit(blocks, temp_bytes=4 * _nbytes((tm, D_MODEL), F32))),
        name="outproj",
    )(o_f, o_b, pm2, o_att, x2, mod, w_out, gdn_norm_w.reshape(1, HEAD_DIM),
      ln_g.reshape(1, D_MODEL), ln_b.reshape(1, D_MODEL))


POOL_REACH = SUBLANES
assert max(POOL_WINDOWS) // 2 <= POOL_REACH


def _pool_kernel(x_ref, xp_ref, xn_ref, mod_ref, pw_ref, ps_ref, lg_ref, lb_ref, o_ref, *, tt, seq):
    t0 = pl.program_id(1) * tt
    m = mod_ref[...]
    sc = 1.0 + m[MOD_SC_M:MOD_SC_M + 1, :]
    sh = m[MOD_SH_M:MOD_SH_M + 1, :]
    x = x_ref[...]
    h_main = x * sc + sh
    h_prev = xp_ref[...] * sc + sh
    h_next = xn_ref[...] * sc + sh
    r = lax.broadcasted_iota(jnp.int32, (tt, tt), 0)
    c = lax.broadcasted_iota(jnp.int32, (tt, tt), 1)
    pos = t0 + lax.broadcasted_iota(jnp.int32, (tt, 1), 0)
    r8 = lax.broadcasted_iota(jnp.int32, (POOL_REACH, 1), 0)
    outs = []
    for gi, win in enumerate(POOL_WINDOWS):
        back, fwd = win // 2, win - 1 - win // 2
        lanes = slice(gi * POOL_GROUP, (gi + 1) * POOL_GROUP)
        band = jnp.where((c >= r - back) & (c <= r + fwd), 1.0, 0.0).astype(BF16)
        hm = h_main[:, lanes]
        wsum = _dot(band, hm.astype(BF16))
        top = jnp.zeros((POOL_REACH, POOL_GROUP), F32)
        bot = jnp.zeros((POOL_REACH, POOL_GROUP), F32)
        for k in range(POOL_REACH):
            if POOL_REACH - k <= back:
                use = (r8 <= k - POOL_REACH + back) & (t0 - POOL_REACH + k >= 0)
                top = top + jnp.where(use, 1.0, 0.0) * h_prev[k:k + 1, lanes]
            if k < fwd:
                use = (r8 >= k + POOL_REACH - fwd) & (t0 + tt + k < seq)
                bot = bot + jnp.where(use, 1.0, 0.0) * h_next[k:k + 1, lanes]
        wsum = jnp.concatenate([wsum[:POOL_REACH] + top, wsum[POOL_REACH:tt - POOL_REACH],
                                wsum[tt - POOL_REACH:] + bot], axis=0)
        cnt = (jnp.minimum(pos + fwd, seq - 1) - jnp.maximum(pos - back, 0) + 1).astype(F32)
        diff = wsum / cnt - hm
        outs.append(_dot(diff.astype(BF16), pw_ref[gi]))
    mix = jnp.concatenate(outs, axis=1) * ps_ref[...]
    y = DEEPNORM_ALPHA * x + m[MOD_G_M:MOD_G_M + 1, :] * mix
    o_ref[...] = _layer_norm_rows(y, lg_ref[...], lb_ref[...])


def _pool_call(x, mod, pool_w, pool_scale, ln_g, ln_b):
    bsz, seq, _ = x.shape
    tt = 256
    hb = tt // POOL_REACH
    n_halo = seq // POOL_REACH
    vec = pl.BlockSpec((1, D_MODEL), lambda b, i: (0, 0))
    blocks = (2 * _nbytes((tt, D_MODEL), F32) + 2 * _nbytes((POOL_REACH, D_MODEL), F32)
              + _nbytes((MOD_ROWS, D_MODEL), F32) + _nbytes(pool_w.shape, BF16))
    temps = 5 * _nbytes((tt, D_MODEL), F32)
    return pl.pallas_call(
        functools.partial(_pool_kernel, tt=tt, seq=seq),
        grid=(bsz, seq // tt),
        in_specs=[
            pl.BlockSpec((None, tt, D_MODEL), lambda b, i: (b, i, 0)),
            pl.BlockSpec((None, POOL_REACH, D_MODEL), lambda b, i: (b, jnp.maximum(i * hb - 1, 0), 0)),
            pl.BlockSpec((None, POOL_REACH, D_MODEL), lambda b, i: (b, jnp.minimum((i + 1) * hb, n_halo - 1), 0)),
            pl.BlockSpec((None, MOD_ROWS, D_MODEL), lambda b, i: (b, 0, 0)),
            pl.BlockSpec(pool_w.shape, lambda b, i: (0, 0, 0)),
            vec, vec, vec,
        ],
        out_specs=pl.BlockSpec((None, tt, D_MODEL), lambda b, i: (b, i, 0)),
        out_shape=jax.ShapeDtypeStruct((bsz, seq, D_MODEL), F32),
        compiler_params=pltpu.CompilerParams(
            dimension_semantics=("parallel", "parallel"),
            vmem_limit_bytes=_vmem_limit(blocks, temp_bytes=temps)),
        name="pool",
    )(x, x, x, mod, pool_w, pool_scale.reshape(1, D_MODEL), ln_g.reshape(1, D_MODEL), ln_b.reshape(1, D_MODEL))


def _mlp_kernel(x_ref, mod_ref, w1_ref, w2_ref, lg_ref, lb_ref, o_ref, h_ref, acc_ref):
    j = pl.program_id(1)

    @pl.when(j == 0)
    def _():
        m = mod_ref[...]
        h = x_ref[...] * (1.0 + m[MOD_SC_F:MOD_SC_F + 1, :]) + m[MOD_SH_F:MOD_SH_F + 1, :]
        h_ref[...] = h.astype(BF16)
        acc_ref[...] = jnp.zeros_like(acc_ref)

    u = jnp.maximum(_dot(h_ref[...], w1_ref[...]), 0.0)
    acc_ref[...] += _dot((u * u).astype(BF16), w2_ref[...])

    @pl.when(j == pl.num_programs(1) - 1)
    def _():
        m = mod_ref[...]
        y = DEEPNORM_ALPHA * x_ref[...] + m[MOD_G_F:MOD_G_F + 1, :] * acc_ref[...]
        o_ref[...] = _layer_norm_rows(y, lg_ref[...], lb_ref[...])


def _mlp_call(x2, mod, w1, w2, layer, ln_g, ln_b, seq):
    m = x2.shape[0]
    tm, tf = 512, 1024
    vec = pl.BlockSpec((1, D_MODEL), lambda i, j: (0, 0))
    blocks = (2 * _nbytes((tm, D_MODEL), F32) + _nbytes((MOD_ROWS, D_MODEL), F32)
              + 2 * _nbytes((D_MODEL, tf), BF16))
    scratch = _nbytes((tm, D_MODEL), BF16) + _nbytes((tm, D_MODEL), F32)
    temps = 2 * _nbytes((tm, tf), F32) + _nbytes((tm, D_MODEL), F32)
    return pl.pallas_call(
        _mlp_kernel,
        grid=(m // tm, D_FF // tf),
        in_specs=[
            pl.BlockSpec((tm, D_MODEL), lambda i, j: (i, 0)),
            pl.BlockSpec((None, MOD_ROWS, D_MODEL), lambda i, j: ((i * tm) // seq, 0, 0)),
            pl.BlockSpec((None, D_MODEL, tf), lambda i, j: (layer, 0, j)),
            pl.BlockSpec((None, tf, D_MODEL), lambda i, j: (layer, j, 0)),
            vec, vec,
        ],
        out_specs=pl.BlockSpec((tm, D_MODEL), lambda i, j: (i, 0)),
        out_shape=jax.ShapeDtypeStruct((m, D_MODEL), F32),
        scratch_shapes=[pltpu.VMEM((tm, D_MODEL), BF16), pltpu.VMEM((tm, D_MODEL), F32)],
        compiler_params=pltpu.CompilerParams(
            dimension_semantics=("parallel", "arbitrary"),
            vmem_limit_bytes=_vmem_limit(blocks, scratch, temps)),
        name="mlp",
    )(x2, mod, w1, w2, ln_g.reshape(1, D_MODEL), ln_b.reshape(1, D_MODEL))


def _mod_table(cond_rows):
    bsz = cond_rows.shape[0]
    t = cond_rows.reshape(bsz, 6, D_MODEL)
    return jnp.pad(t, ((0, 0), (0, MOD_ROWS - 6), (0, 0)))


def _pack_weights(w_in, conv_w, a_log, dt_bias, w_out, pool_w, mlp_w1, mlp_w2):
    wi = w_in[0]
    w_main = jnp.concatenate(
        [wi[:, :OFF_BETA], wi[:, OFF_AQ:]], axis=1).astype(BF16)
    w_gate = jnp.pad(wi[:, OFF_BETA:OFF_AQ], ((0, 0), (0, LANES - 2 * N_GATE_COLS))).astype(BF16)
    conv_w8 = jnp.pad(conv_w[0], ((0, SUBLANES - CONV_W), (0, 0)))
    gate_params = jnp.zeros((SUBLANES, LANES), F32)
    gate_params = gate_params.at[0, N_GATE_COLS:2 * N_GATE_COLS].set(dt_bias[0].reshape(-1))
    gate_params = gate_params.at[1, N_GATE_COLS:2 * N_GATE_COLS].set(a_log[0].reshape(-1))
    return dict(w_main=w_main, w_gate=w_gate, conv_w8=conv_w8, gate_params=gate_params,
                w_out=w_out[0].astype(BF16), pool_w=pool_w[0].astype(BF16),
                w1=mlp_w1.astype(BF16), w2=mlp_w2.astype(BF16))


def _trunk(x, cond, pk, gdn_norm_w, q_norm_w, k_norm_w, pool_scale, ln_g, ln_b):
    bsz, seq, _ = x.shape
    m = bsz * seq
    mod0 = _mod_table(cond[0])
    mod1 = _mod_table(cond[1])
    x2 = x.reshape(m, D_MODEL)

    cos, sin_up, sin_dn = _rope_tables(seq)
    pm2, gates2, qt, kr2, vt = _inproj_call(x2, mod0, pk["w_main"], pk["w_gate"], cos, sin_up, sin_dn,
                                            q_norm_w[0], k_norm_w[0], bsz, seq)
    pm = pm2.reshape(bsz, seq, PM_WIDTH)
    gq, gk, gv, gcol, grow = _gdnprep_call(pm, gates2.reshape(bsz, seq, LANES), pk["conv_w8"], pk["gate_params"])
    o_f, o_b = _gdn_call(gq, gk, gv, gcol, grow)
    o_att = _flash_call(qt, kr2.reshape(bsz, seq, ATT_KV_WIDTH), vt)
    x2 = _outproj_call(o_f.reshape(m, GDN_WIDTH), o_b.reshape(m, GDN_WIDTH), pm2, o_att.reshape(m, ATT_WIDTH),
                       x2, mod0, pk["w_out"], gdn_norm_w[0], ln_g[0, 0], ln_b[0, 0], seq)
    x2 = _mlp_call(x2, mod0, pk["w1"], pk["w2"], 0, ln_g[0, 1], ln_b[0, 1], seq)

    x3 = _pool_call(x2.reshape(bsz, seq, D_MODEL), mod1, pk["pool_w"], pool_scale[0], ln_g[1, 0], ln_b[1, 0])
    x2 = _mlp_call(x3.reshape(m, D_MODEL), mod1, pk["w1"], pk["w2"], 1, ln_g[1, 1], ln_b[1, 1], seq)
    return x2.reshape(bsz, seq, D_MODEL)


def kernel(x_prompt, x_sample, c_prompt, c_sample, w_in, conv_w, a_log, dt_bias, gdn_norm_w, q_norm_w,
           k_norm_w, w_out, pool_w, pool_scale, mlp_w1, mlp_w2, ada_w, ada_b, ln_g, ln_b):
    bp, bs = c_prompt.shape[0], c_sample.shape[0]
    c_all = jnp.concatenate([c_prompt, c_sample], axis=0)
    c_all = jnp.pad(c_all, ((0, (-c_all.shape[0]) % SUBLANES), (0, 0)))
    cond = _ada_call(c_all, ada_w, ada_b)
    pk = _pack_weights(w_in, conv_w, a_log, dt_bias, w_out, pool_w, mlp_w1, mlp_w2)
    y_prompt = _trunk(x_prompt, cond[:, :bp], pk, gdn_norm_w, q_norm_w, k_norm_w, pool_scale, ln_g, ln_b)
    y_sample = _trunk(x_sample, cond[:, bp:bp + bs], pk, gdn_norm_w, q_norm_w, k_norm_w, pool_scale, ln_g, ln_b)
    return (y_prompt, y_sample)
```

```python
import functools

import jax
import jax.numpy as jnp
from jax import lax
from jax.experimental import pallas as pl
from jax.experimental.pallas import tpu as pltpu

F32 = jnp.float32
BF16 = jnp.bfloat16

D_MODEL = 2048
DEPTH = 2
GRID_W = 64
HEAD_DIM = 128
GDN_HEADS = 8
ATT_HEADS = 8
ATT_KV_HEADS = 2
ATT_GROUP = ATT_HEADS // ATT_KV_HEADS
GDN_WIDTH = GDN_HEADS * HEAD_DIM
ATT_WIDTH = ATT_HEADS * HEAD_DIM
ATT_KV_WIDTH = ATT_KV_HEADS * HEAD_DIM
CONV_W = 5
ROPE_THETA = 10000.0
POOL_WINDOWS = (2, 4, 8, 16)
N_POOL_GROUPS = 4
POOL_GROUP = D_MODEL // N_POOL_GROUPS
D_FF = 4 * D_MODEL
DEEPNORM_ALPHA = (2 * DEPTH) ** 0.25
NORM_EPS = 1e-6
LN_EPS = 1e-5

GDN_QKV = 3 * GDN_WIDTH
OFF_Z = GDN_QKV
OFF_BETA = OFF_Z + GDN_WIDTH
OFF_A = OFF_BETA + 2 * GDN_HEADS
OFF_AQ = OFF_A + 2 * GDN_HEADS
OFF_AK = OFF_AQ + ATT_WIDTH
OFF_AV = OFF_AK + ATT_KV_WIDTH
D_IN = OFF_AV + ATT_KV_WIDTH

PM_QKV = 0
PM_Z = GDN_QKV
PM_AQ = PM_Z + GDN_WIDTH
PM_AK = PM_AQ + ATT_WIDTH
PM_AV = PM_AK + ATT_KV_WIDTH
PM_WIDTH = PM_AV + ATT_KV_WIDTH

LANES = 128
SUBLANES = 8
BF16_ROWS = 16
V7X_VMEM_BYTES = 64 * 1024 * 1024

GDN_CHUNK = 128
GDN_BLOCK_CHUNKS = 4
N_GATE_COLS = 2 * GDN_HEADS

MOD_SH_M, MOD_SC_M, MOD_G_M, MOD_SH_F, MOD_SC_F, MOD_G_F = range(6)
MOD_ROWS = 8


def _vmem_limit(block_bytes, scratch_bytes=0, temp_bytes=0):
    need = 2 * block_bytes + scratch_bytes + temp_bytes
    return int(min(need + need // 4, V7X_VMEM_BYTES - 8 * 1024 * 1024))


def _nbytes(shape, dtype):
    n = 1
    for s in shape:
        n *= s
    return n * jnp.dtype(dtype).itemsize


def _sigmoid(x):
    return 1.0 / (1.0 + jnp.exp(-x))


def _silu(x):
    return x * _sigmoid(x)


def _layer_norm_rows(y, g, b):
    mu = jnp.mean(y, axis=-1, keepdims=True)
    yc = y - mu
    var = jnp.mean(yc * yc, axis=-1, keepdims=True)
    return yc * lax.rsqrt(var + LN_EPS) * g + b


def _dot(a, b):
    return jnp.dot(a, b, preferred_element_type=F32)


def _dot_nt(a, b):
    return lax.dot_general(a, b, (((1,), (1,)), ((), ())), preferred_element_type=F32)


def _ada_kernel(c_ref, w_ref, b_ref, o_ref):
    c = c_ref[...]
    s = _silu(c).astype(BF16)
    o_ref[...] = _dot(s, w_ref[...].astype(BF16)) + b_ref[...]


def _ada_call(c_all, ada_w, ada_b):
    rows = c_all.shape[0]
    n = ada_w.shape[-1]
    tn = 1024
    blocks = _nbytes((D_MODEL, tn), F32) + _nbytes((rows, D_MODEL), F32) + 2 * _nbytes((rows, tn), F32)
    return pl.pallas_call(
        _ada_kernel,
        grid=(DEPTH, n // tn),
        in_specs=[
            pl.BlockSpec((rows, D_MODEL), lambda l, j: (0, 0)),
            pl.BlockSpec((None, D_MODEL, tn), lambda l, j: (l, 0, j)),
            pl.BlockSpec((None, 1, tn), lambda l, j: (l, 0, j)),
        ],
        out_specs=pl.BlockSpec((None, rows, tn), lambda l, j: (l, 0, j)),
        out_shape=jax.ShapeDtypeStruct((DEPTH, rows, n), F32),
        compiler_params=pltpu.CompilerParams(
            dimension_semantics=("parallel", "parallel"),
            vmem_limit_bytes=_vmem_limit(blocks, temp_bytes=_nbytes((D_MODEL, tn), BF16))),
        name="ada",
    )(c_all, ada_w, ada_b.reshape(DEPTH, 1, n))


INPROJ_TN = PM_WIDTH // 2
ATT_COL0 = PM_AQ - INPROJ_TN
assert PM_AQ >= INPROJ_TN and ATT_COL0 % LANES == 0


def _inproj_kernel(x_ref, mod_ref, w_ref, wg_ref, cos_ref, su_ref, sd_ref, qw_ref, kw_ref,
                   o_ref, g_ref, qt_ref, k_ref, vt_ref, h_ref):
    j = pl.program_id(1)

    @pl.when(j == 0)
    def _():
        m = mod_ref[...]
        h = x_ref[...] * (1.0 + m[MOD_SC_M:MOD_SC_M + 1, :]) + m[MOD_SH_M:MOD_SH_M + 1, :]
        hb = h.astype(BF16)
        h_ref[...] = hb
        g_ref[...] = _dot(hb, wg_ref[...])
        o_ref[...] = _dot(hb, w_ref[...]).astype(o_ref.dtype)

    @pl.when(j == 1)
    def _():
        hb = h_ref[...]
        att = _dot(hb, w_ref[:, ATT_COL0:])
        o_ref[:, :ATT_COL0] = _dot(hb, w_ref[:, :ATT_COL0]).astype(o_ref.dtype)
        o_ref[:, ATT_COL0:] = att.astype(o_ref.dtype)
        cos, su, sd = cos_ref[...], su_ref[...], sd_ref[...]
        qw, kw = qw_ref[...], kw_ref[...]
        for hd in range(ATT_HEADS):
            lanes = slice(hd * HEAD_DIM, (hd + 1) * HEAD_DIM)
            q = _norm_rope(att[:, lanes], qw, cos, su, sd, HEAD_DIM ** -0.5 * LOG2_E)
            qt_ref[lanes, :] = q.T.astype(qt_ref.dtype)
        for hd in range(ATT_KV_HEADS):
            lanes = slice(hd * HEAD_DIM, (hd + 1) * HEAD_DIM)
            ak = att[:, ATT_WIDTH + hd * HEAD_DIM:ATT_WIDTH + (hd + 1) * HEAD_DIM]
            av = att[:, ATT_WIDTH + ATT_KV_WIDTH + hd * HEAD_DIM:ATT_WIDTH + ATT_KV_WIDTH + (hd + 1) * HEAD_DIM]
            k_ref[:, lanes] = _norm_rope(ak, kw, cos, su, sd, 1.0).astype(k_ref.dtype)
            vt_ref[hd, :HEAD_DIM, :] = av.T.astype(vt_ref.dtype)
            vt_ref[hd, HEAD_DIM:, :] = jnp.ones((BF16_ROWS, vt_ref.shape[-1]), vt_ref.dtype)


def _inproj_call(x2, mod, w_main, w_gate, cos, sin_up, sin_dn, q_norm_w, k_norm_w, bsz, seq):
    m = x2.shape[0]
    tm, tn = 512, INPROJ_TN
    nt = seq // tm
    per_key_tile = ATT_TK // tm
    assert PM_WIDTH == 2 * tn and ATT_TK % tm == 0
    tab = pl.BlockSpec((tm, HEAD_DIM), lambda i, j: (i % nt, 0))
    vec = pl.BlockSpec((1, HEAD_DIM), lambda i, j: (0, 0))
    blocks = (_nbytes((tm, D_MODEL), F32) + _nbytes((MOD_ROWS, D_MODEL), F32) + _nbytes((D_MODEL, tn), BF16)
              + _nbytes((D_MODEL, LANES), BF16) + _nbytes((tm, tn), BF16) + _nbytes((tm, LANES), F32)
              + 3 * _nbytes((tm, HEAD_DIM), F32) + _nbytes((ATT_WIDTH, tm), BF16)
              + _nbytes((tm, ATT_KV_WIDTH), BF16) + _nbytes((ATT_KV_HEADS, VT_ROWS, tm), BF16))
    scratch = _nbytes((tm, D_MODEL), BF16)
    temps = _nbytes((tm, D_MODEL), F32) + 2 * _nbytes((tm, tn), F32)
    return pl.pallas_call(
        _inproj_kernel,
        grid=(m // tm, PM_WIDTH // tn),
        in_specs=[
            pl.BlockSpec((tm, D_MODEL), lambda i, j: (i, 0)),
            pl.BlockSpec((None, MOD_ROWS, D_MODEL), lambda i, j: (i // nt, 0, 0)),
            pl.BlockSpec((D_MODEL, tn), lambda i, j: (0, j)),
            pl.BlockSpec((D_MODEL, LANES), lambda i, j: (0, 0)),
            tab, tab, tab, vec, vec,
        ],
        out_specs=[
            pl.BlockSpec((tm, tn), lambda i, j: (i, j)),
            pl.BlockSpec((tm, LANES), lambda i, j: (i, 0)),
            pl.BlockSpec((None, ATT_WIDTH, tm), lambda i, j: (i // nt, 0, i % nt)),
            pl.BlockSpec((tm, ATT_KV_WIDTH), lambda i, j: (i, 0)),
            pl.BlockSpec((None, ATT_KV_HEADS, None, VT_ROWS, tm),
                         lambda i, j: (i // nt, 0, (i % nt) // per_key_tile, 0, i % per_key_tile)),
        ],
        out_shape=[
            jax.ShapeDtypeStruct((m, PM_WIDTH), BF16),
            jax.ShapeDtypeStruct((m, LANES), F32),
            jax.ShapeDtypeStruct((bsz, ATT_WIDTH, seq), BF16),
            jax.ShapeDtypeStruct((m, ATT_KV_WIDTH), BF16),
            jax.ShapeDtypeStruct((bsz, ATT_KV_HEADS, seq // ATT_TK, VT_ROWS, ATT_TK), BF16),
        ],
        scratch_shapes=[pltpu.VMEM((tm, D_MODEL), BF16)],
        compiler_params=pltpu.CompilerParams(
            dimension_semantics=("parallel", "arbitrary"),
            vmem_limit_bytes=_vmem_limit(blocks, scratch, temps)),
        name="inproj",
    )(x2, mod, w_main, w_gate, cos, sin_up, sin_dn, q_norm_w.reshape(1, HEAD_DIM), k_norm_w.reshape(1, HEAD_DIM))


def _gdnprep_kernel(x_ref, xp_ref, xn_ref, gate_ref, cw_ref, gp_ref,
                    q_ref, k_ref, v_ref, gcol_ref, grow_ref, *, tt):
    i = pl.program_id(1)
    first = i == 0
    last = i == pl.num_programs(1) - 1
    cw = cw_ref[...]
    sr = lax.broadcasted_iota(jnp.int32, (tt, tt), 0)
    sc = lax.broadcasted_iota(jnp.int32, (tt, tt), 1)
    shifts = [tap - CONV_W // 2 for tap in range(CONV_W)]
    shift_mats = {s: jnp.where(sc == sr + s, 1.0, 0.0).astype(BF16) for s in shifts if s != 0}
    cw_chunk = 2 * HEAD_DIM
    r8 = lax.broadcasted_iota(jnp.int32, (SUBLANES, cw_chunk), 0)
    for part, out_ref in enumerate((q_ref, k_ref, v_ref)):
        for ch in range(GDN_WIDTH // cw_chunk):
            cols = slice(part * GDN_WIDTH + ch * cw_chunk, part * GDN_WIDTH + (ch + 1) * cw_chunk)
            xb = x_ref[:, cols]
            prev8 = xp_ref[:, cols].astype(F32)[BF16_ROWS - SUBLANES:]
            next8 = xn_ref[:, cols].astype(F32)[:SUBLANES]
            prev8 = jnp.where(first, 0.0, prev8)
            next8 = jnp.where(last, 0.0, next8)
            w = cw[:, cols]
            acc = xb.astype(F32) * w[CONV_W // 2:CONV_W // 2 + 1, :]
            top = jnp.zeros((SUBLANES, cw_chunk), F32)
            bot = jnp.zeros((SUBLANES, cw_chunk), F32)
            for tap, s in enumerate(shifts):
                if s == 0:
                    continue
                wt = w[tap:tap + 1, :]
                acc = acc + _dot(shift_mats[s], xb) * wt
                if s < 0:
                    top = top + jnp.where(r8 < -s, pltpu.roll(prev8, -s, 0), 0.0) * wt
                else:
                    bot = bot + jnp.where(r8 >= SUBLANES - s, pltpu.roll(next8, SUBLANES - s, 0), 0.0) * wt
            acc = jnp.concatenate([acc[:SUBLANES] + top, acc[SUBLANES:tt - SUBLANES],
                                   acc[tt - SUBLANES:] + bot], axis=0)
            y = _silu(acc)
            if part < 2:
                heads = []
                for h in range(cw_chunk // HEAD_DIM):
                    yh = y[:, h * HEAD_DIM:(h + 1) * HEAD_DIM]
                    inv = lax.rsqrt(jnp.sum(yh * yh, axis=-1, keepdims=True) + NORM_EPS)
                    if part == 0:
                        inv = inv * (HEAD_DIM ** -0.5)
                    heads.append(yh * inv)
                y = jnp.concatenate(heads, axis=1)
            out_ref[:, ch * cw_chunk:(ch + 1) * cw_chunk] = y.astype(out_ref.dtype)

    raw = gate_ref[...]
    gp = gp_ref[...]
    col = lax.broadcasted_iota(jnp.int32, raw.shape, 1)
    beta = _sigmoid(raw)
    z = raw + gp[0:1, :]
    softplus = jnp.maximum(z, 0.0) + jnp.log(1.0 + jnp.exp(-jnp.abs(z)))
    logdec = -jnp.exp(gp[1:2, :]) * softplus
    is_dec = (col >= N_GATE_COLS) & (col < 2 * N_GATE_COLS)
    gsrc = jnp.where(is_dec, logdec, 0.0)
    r = lax.broadcasted_iota(jnp.int32, (tt, tt), 0)
    c = lax.broadcasted_iota(jnp.int32, (tt, tt), 1)
    same = (r // GDN_CHUNK) == (c // GDN_CHUNK)
    p_lo = jnp.where(same & (c <= r), 1.0, 0.0).astype(F32)
    p_up = jnp.where(same & (c >= r), 1.0, 0.0).astype(F32)
    cum_lo = jnp.dot(p_lo, gsrc, precision=lax.Precision.HIGHEST, preferred_element_type=F32)
    cum_up = jnp.dot(p_up, gsrc, precision=lax.Precision.HIGHEST, preferred_element_type=F32)
    total = cum_lo + cum_up - gsrc
    fwd_col = col < N_GATE_COLS + GDN_HEADS
    gc = jnp.where(fwd_col, cum_lo, cum_up)
    tot_shift = pltpu.roll(total, N_GATE_COLS, 1)
    gcol = jnp.where(col < N_GATE_COLS, beta,
                     jnp.where(col < 2 * N_GATE_COLS, gc,
                               jnp.where(col < 3 * N_GATE_COLS, tot_shift, 0.0)))
    gcol_ref[...] = gcol
    grow_ref[...] = gcol.T


def _gdnprep_call(pm, gates, conv_w8, gate_params):
    bsz, seq, _ = pm.shape
    tt = 256
    hb = tt // BF16_ROWS
    n_halo = seq // BF16_ROWS
    blocks = (_nbytes((tt, GDN_QKV), BF16) + 2 * _nbytes((BF16_ROWS, GDN_QKV), BF16) + _nbytes((tt, LANES), F32)
              + _nbytes((SUBLANES, GDN_QKV), F32) + 3 * _nbytes((tt, GDN_WIDTH), BF16) + 2 * _nbytes((tt, LANES), F32))
    temps = 8 * _nbytes((tt, GDN_WIDTH), F32) + 4 * _nbytes((tt, tt), F32)
    return pl.pallas_call(
        functools.partial(_gdnprep_kernel, tt=tt),
        grid=(bsz, seq // tt),
        in_specs=[
            pl.BlockSpec((None, tt, GDN_QKV), lambda b, i: (b, i, 0)),
            pl.BlockSpec((None, BF16_ROWS, GDN_QKV), lambda b, i: (b, jnp.maximum(i * hb - 1, 0), 0)),
            pl.BlockSpec((None, BF16_ROWS, GDN_QKV), lambda b, i: (b, jnp.minimum((i + 1) * hb, n_halo - 1), 0)),
            pl.BlockSpec((None, tt, LANES), lambda b, i: (b, i, 0)),
            pl.BlockSpec((SUBLANES, GDN_QKV), lambda b, i: (0, 0)),
            pl.BlockSpec((SUBLANES, LANES), lambda b, i: (0, 0)),
        ],
        out_specs=[
            pl.BlockSpec((None, tt, GDN_WIDTH), lambda b, i: (b, i, 0)),
            pl.BlockSpec((None, tt, GDN_WIDTH), lambda b, i: (b, i, 0)),
            pl.BlockSpec((None, tt, GDN_WIDTH), lambda b, i: (b, i, 0)),
            pl.BlockSpec((None, tt, LANES), lambda b, i: (b, i, 0)),
            pl.BlockSpec((None, LANES, tt), lambda b, i: (b, 0, i)),
        ],
        out_shape=[
            jax.ShapeDtypeStruct((bsz, seq, GDN_WIDTH), BF16),
            jax.ShapeDtypeStruct((bsz, seq, GDN_WIDTH), BF16),
            jax.ShapeDtypeStruct((bsz, seq, GDN_WIDTH), BF16),
            jax.ShapeDtypeStruct((bsz, seq, LANES), F32),
            jax.ShapeDtypeStruct((bsz, LANES, seq), F32),
        ],
        compiler_params=pltpu.CompilerParams(
            dimension_semantics=("parallel", "parallel"),
            vmem_limit_bytes=_vmem_limit(blocks, temp_bytes=temps)),
        name="gdnprep",
    )(pm, pm, pm, gates, conv_w8, gate_params)


def _gdn_kernel(qf_ref, kf_ref, vf_ref, gcf_ref, grf_ref,
                qb_ref, kb_ref, vb_ref, gcb_ref, grb_ref,
                of_ref, ob_ref, state_ref):
    @pl.when(pl.program_id(1) == 0)
    def _():
        state_ref[...] = jnp.zeros_like(state_ref)

    n = GDN_CHUNK
    r = lax.broadcasted_iota(jnp.int32, (n, n), 0)
    c = lax.broadcasted_iota(jnp.int32, (n, n), 1)
    eye = jnp.where(r == c, 1.0, 0.0).astype(F32)

    units = []
    for d, (q_ref, k_ref, v_ref, gc_ref, gr_ref, o_ref) in enumerate((
            (qf_ref, kf_ref, vf_ref, gcf_ref, grf_ref, of_ref),
            (qb_ref, kb_ref, vb_ref, gcb_ref, grb_ref, ob_ref))):
        gcol = gc_ref[...]
        grow = gr_ref[...]
        reverse = d == 1
        incl = (r <= c) if reverse else (r >= c)
        strict = (r < c) if reverse else (r > c)
        for pos in range(GDN_BLOCK_CHUNKS):
            chunk = GDN_BLOCK_CHUNKS - 1 - pos if reverse else pos
            rows = slice(chunk * n, (chunk + 1) * n)
            for h in range(GDN_HEADS):
                lanes = slice(h * HEAD_DIM, (h + 1) * HEAD_DIM)
                j = d * GDN_HEADS + h
                beta = gcol[rows, j:j + 1]
                gcc = gcol[rows, N_GATE_COLS + j:N_GATE_COLS + j + 1]
                gtc = gcol[rows, 2 * N_GATE_COLS + j:2 * N_GATE_COLS + j + 1]
                gcr = grow[N_GATE_COLS + j:N_GATE_COLS + j + 1, rows]
                gtr = grow[2 * N_GATE_COLS + j:2 * N_GATE_COLS + j + 1, rows]
                q = q_ref[rows, lanes]
                k = k_ref[rows, lanes]
                kf = k.astype(F32)
                kb = kf * beta
                egc = jnp.exp(gcc)
                units.append(dict(
                    d=d, h=h, pos=pos, rows=rows, lanes=lanes, o_ref=o_ref, strict=strict, q=q, k=k,
                    decay=jnp.where(incl, jnp.exp(jnp.where(incl, gcc - gcr, 0.0)), 0.0),
                    kbb=kb.astype(BF16),
                    vb=(v_ref[rows, lanes].astype(F32) * beta).astype(BF16),
                    kbg=(kb * egc).astype(BF16),
                    qg=(q.astype(F32) * egc).astype(BF16),
                    kdec_t=(kf * jnp.exp(gtc - gcc)).T.astype(BF16),
                    sdec=jnp.exp(gtr)))

    for u in units:
        aq = _dot_nt(jnp.concatenate([u["kbb"], u["q"]], axis=0), u["k"])
        low = jnp.where(u["strict"], aq[:n] * u["decay"], 0.0)
        u["qk"] = (aq[n:] * u["decay"]).astype(BF16)
        u["tinv"] = eye - low
        u["lb"] = low.astype(BF16)

    for u in units:
        u["power"] = _dot(u["lb"], u["lb"])
    n_factors = (n - 1).bit_length() - 1
    for it in range(n_factors):
        for u in units:
            pb = u["power"].astype(BF16)
            if it + 1 < n_factors:
                both = _dot(jnp.concatenate([u["tinv"].astype(BF16), pb], axis=0), pb)
                u["tinv"] = u["tinv"] + both[:n]
                u["power"] = both[n:]
            else:
                u["tinv"] = u["tinv"] + _dot(u["tinv"].astype(BF16), pb)

    for u in units:
        u["uw"] = _dot(u["tinv"].astype(BF16), jnp.concatenate([u["vb"], u["kbg"]], axis=1))
    state = {(d, h): state_ref[d, h] for d in range(2) for h in range(GDN_HEADS)}
    for pos in range(GDN_BLOCK_CHUNKS):
        now = [u for u in units if u["pos"] == pos]
        for u in now:
            w = u["uw"][:, HEAD_DIM:].astype(BF16)
            u["ws"] = _dot(jnp.concatenate([w, u["qg"]], axis=0), state[u["d"], u["h"]].astype(BF16))
        for u in now:
            v_new = (u["uw"][:, :HEAD_DIM] - u["ws"][:n]).astype(BF16)
            u["os"] = _dot(jnp.concatenate([u["qk"], u["kdec_t"]], axis=0), v_new)
        for u in now:
            u["o_ref"][u["rows"], u["lanes"]] = u["ws"][n:] + u["os"][:n]
            state[u["d"], u["h"]] = state[u["d"], u["h"]] * u["sdec"] + u["os"][n:]
    for (d, h), value in state.items():
        state_ref[d, h] = value


def _gdn_call(q, k, v, gcol, grow):
    bsz, seq, _ = q.shape
    tb = GDN_CHUNK * GDN_BLOCK_CHUNKS
    nb = seq // tb
    qkv_spec_f = pl.BlockSpec((None, tb, GDN_WIDTH), lambda b, i: (b, i, 0))
    qkv_spec_b = pl.BlockSpec((None, tb, GDN_WIDTH), lambda b, i: (b, nb - 1 - i, 0))
    gc_spec_f = pl.BlockSpec((None, tb, LANES), lambda b, i: (b, i, 0))
    gc_spec_b = pl.BlockSpec((None, tb, LANES), lambda b, i: (b, nb - 1 - i, 0))
    gr_spec_f = pl.BlockSpec((None, LANES, tb), lambda b, i: (b, 0, i))
    gr_spec_b = pl.BlockSpec((None, LANES, tb), lambda b, i: (b, 0, nb - 1 - i))
    blocks = 2 * (3 * _nbytes((tb, GDN_WIDTH), BF16) + 2 * _nbytes((tb, LANES), F32) + _nbytes((tb, GDN_WIDTH), F32))
    scratch = _nbytes((2, GDN_HEADS, HEAD_DIM, HEAD_DIM), F32)
    temps = GDN_BLOCK_CHUNKS * 2 * GDN_HEADS * 24 * _nbytes((GDN_CHUNK, LANES), F32)
    return pl.pallas_call(
        _gdn_kernel,
        grid=(bsz, nb),
        in_specs=[qkv_spec_f, qkv_spec_f, qkv_spec_f, gc_spec_f, gr_spec_f,
                  qkv_spec_b, qkv_spec_b, qkv_spec_b, gc_spec_b, gr_spec_b],
        out_specs=[qkv_spec_f, qkv_spec_b],
        out_shape=[jax.ShapeDtypeStruct((bsz, seq, GDN_WIDTH), F32),
                   jax.ShapeDtypeStruct((bsz, seq, GDN_WIDTH), F32)],
        scratch_shapes=[pltpu.VMEM((2, GDN_HEADS, HEAD_DIM, HEAD_DIM), F32)],
        compiler_params=pltpu.CompilerParams(
            dimension_semantics=("parallel", "arbitrary"),
            vmem_limit_bytes=_vmem_limit(blocks, scratch, temps)),
        name="gdn",
    )(q, k, v, gcol, grow, q, k, v, gcol, grow)


def _rope_tables(seq):
    rows = seq // GRID_W
    half = HEAD_DIM // 2
    inv_freq = ROPE_THETA ** (-jnp.arange(0, half, 2, dtype=F32) / half)
    ang_r = jnp.arange(rows, dtype=F32)[:, None] * inv_freq
    ang_c = jnp.arange(GRID_W, dtype=F32)[:, None] * inv_freq

    def table(fn):
        tr = jnp.broadcast_to(fn(ang_r)[:, None, :], (rows, GRID_W, half // 2))
        tc = jnp.broadcast_to(fn(ang_c)[None, :, :], (rows, GRID_W, half // 2))
        return jnp.concatenate([tr, tr, tc, tc], -1).reshape(seq, HEAD_DIM)

    cos, sin = table(jnp.cos), table(jnp.sin)
    lane = jnp.arange(HEAD_DIM)
    first = (lane // (HEAD_DIM // 4)) % 2 == 0
    sin_up = jnp.where(first, -sin, 0.0)
    sin_dn = jnp.where(first, 0.0, sin)
    return cos, sin_up, sin_dn


def _norm_rope(x, w, cos, sin_up, sin_dn, scale):
    xn = x * lax.rsqrt(jnp.mean(x * x, axis=-1, keepdims=True) + NORM_EPS) * w
    quarter = HEAD_DIM // 4
    up = pltpu.roll(xn, HEAD_DIM - quarter, 1)
    dn = pltpu.roll(xn, quarter, 1)
    y = xn * cos + up * sin_up + dn * sin_dn
    return y * scale if scale != 1.0 else y


ATT_TK = 2048
FLASH_SUB = 256
VT_ROWS = HEAD_DIM + BF16_ROWS
LOG2_E = 1.4426950408889634


def _flash_kernel(qt_ref, qtn_ref, k_ref, vt_ref, o_ref, m_ref, acc_ref, qc_ref, qn_ref,
                  sa_ref, sb_ref, mca_ref, mcb_ref, *, tq, tk, ts, n_k):
    m_ref[...] = jnp.full_like(m_ref, -jnp.inf)
    acc_ref[...] = jnp.zeros_like(acc_ref)
    for g in range(ATT_GROUP):
        qc_ref[:, g * tq:(g + 1) * tq] = qt_ref[g * HEAD_DIM:(g + 1) * HEAD_DIM, :]
        qn_ref[:, g * tq:(g + 1) * tq] = qtn_ref[g * HEAD_DIM:(g + 1) * HEAD_DIM, :]

    n_sub = tk // ts
    bufs = ((sa_ref, mca_ref), (sb_ref, mcb_ref))

    def scores(j, q_ref, s_ref, i):
        start = pl.multiple_of(j * tk + i * ts, ts)
        st = _dot(k_ref[pl.ds(start, ts), :], q_ref[...])
        s_ref[i * ts:(i + 1) * ts, :] = st
        return jnp.max(st, axis=0, keepdims=True)

    def weigh(j, s_ref, i, m_new):
        p = jnp.exp2(s_ref[i * ts:(i + 1) * ts, :] - m_new)
        return _dot(vt_ref[j, :, i * ts:(i + 1) * ts], p.astype(BF16))

    def step(j, parity, j_next, q_next):
        s_cur, mc_cur = bufs[parity]
        s_nxt, mc_nxt = bufs[1 - parity]
        m_prev = m_ref[...]
        m_new = jnp.maximum(m_prev, mc_cur[...])
        alpha = jnp.exp2(m_prev - m_new)
        mc_next, pv = None, None
        for i in range(n_sub):
            pvi = weigh(j, s_cur, i, m_new)
            pv = pvi if pv is None else pv + pvi
            mx = scores(j_next, q_next, s_nxt, i)
            mc_next = mx if mc_next is None else jnp.maximum(mc_next, mx)
        mc_nxt[...] = mc_next
        acc_ref[...] = alpha * acc_ref[...] + pv
        m_ref[...] = m_new

    @pl.when(pl.program_id(2) == 0)
    def _():
        mc0 = None
        for i in range(n_sub):
            mx = scores(0, qc_ref, sa_ref, i)
            mc0 = mx if mc0 is None else jnp.maximum(mc0, mx)
        mca_ref[...] = mc0

    def body(jj, carry):
        step(2 * jj, 0, 2 * jj + 1, qc_ref)
        step(2 * jj + 1, 1, 2 * jj + 2, qc_ref)
        return carry

    lax.fori_loop(0, n_k // 2 - 1, body, 0)
    step(n_k - 2, 0, n_k - 1, qc_ref)
    step(n_k - 1, 1, 0, qn_ref)
    for g in range(ATT_GROUP):
        lanes = slice(g * tq, (g + 1) * tq)
        out_t = acc_ref[:HEAD_DIM, lanes] / acc_ref[HEAD_DIM:HEAD_DIM + 1, lanes]
        o_ref[:, g * HEAD_DIM:(g + 1) * HEAD_DIM] = out_t.T.astype(o_ref.dtype)


def _flash_call(qt, kr, vt):
    bsz, _, seq = qt.shape
    tq, tk = 256, ATT_TK
    gw = ATT_GROUP * HEAD_DIM
    nq = ATT_GROUP * tq
    n_q, n_k = seq // tq, seq // tk
    assert n_k % 2 == 0, "the score buffers alternate by key-tile parity across query tiles"
    blocks = 3 * _nbytes((gw, tq), BF16) + _nbytes((seq, HEAD_DIM), BF16) + _nbytes((seq, VT_ROWS), BF16)
    scratch = (3 * _nbytes((SUBLANES, nq), F32) + _nbytes((VT_ROWS, nq), F32) + 2 * _nbytes((HEAD_DIM, nq), BF16)
               + 2 * _nbytes((tk, nq), F32))
    temps = 2 * _nbytes((tk, nq), F32)
    return pl.pallas_call(
        functools.partial(_flash_kernel, tq=tq, tk=tk, ts=FLASH_SUB, n_k=n_k),
        grid=(bsz, ATT_KV_HEADS, n_q),
        in_specs=[
            pl.BlockSpec((None, gw, tq), lambda b, h, i: (b, h, i)),
            pl.BlockSpec((None, gw, tq), lambda b, h, i: (b, h, jnp.minimum(i + 1, n_q - 1))),
            pl.BlockSpec((None, seq, HEAD_DIM), lambda b, h, i: (b, 0, h)),
            pl.BlockSpec((None, None, n_k, VT_ROWS, tk), lambda b, h, i: (b, h, 0, 0, 0)),
        ],
        out_specs=pl.BlockSpec((None, tq, gw), lambda b, h, i: (b, i, h)),
        out_shape=jax.ShapeDtypeStruct((bsz, seq, ATT_WIDTH), BF16),
        scratch_shapes=[pltpu.VMEM((1, nq), F32),
                        pltpu.VMEM((VT_ROWS, nq), F32), pltpu.VMEM((HEAD_DIM, nq), BF16),
                        pltpu.VMEM((HEAD_DIM, nq), BF16),
                        pltpu.VMEM((tk, nq), F32), pltpu.VMEM((tk, nq), F32),
                        pltpu.VMEM((1, nq), F32), pltpu.VMEM((1, nq), F32)],
        compiler_params=pltpu.CompilerParams(
            dimension_semantics=("parallel", "parallel", "arbitrary"),
            vmem_limit_bytes=_vmem_limit(blocks, scratch, temps)),
        name="flash",
    )(qt, qt, kr, vt)


def _outproj_kernel(of_ref, ob_ref, z_ref, oa_ref, x_ref, mod_ref, w_ref, nw_ref, lg_ref, lb_ref, o_ref):
    nw = nw_ref[...]
    heads = []
    for h in range(GDN_HEADS):
        lanes = slice(h * HEAD_DIM, (h + 1) * HEAD_DIM)
        o = of_ref[:, lanes] + ob_ref[:, lanes]
        on = o * lax.rsqrt(jnp.mean(o * o, axis=-1, keepdims=True) + NORM_EPS) * nw
        heads.append((on * _silu(z_ref[:, lanes].astype(F32))).astype(BF16))
    og = jnp.concatenate(heads, axis=1)
    mix = _dot(og, w_ref[:GDN_WIDTH, :]) + _dot(oa_ref[...], w_ref[GDN_WIDTH:, :])
    m = mod_ref[...]
    y = DEEPNORM_ALPHA * x_ref[...] + m[MOD_G_M:MOD_G_M + 1, :] * mix
    o_ref[...] = _layer_norm_rows(y, lg_ref[...], lb_ref[...])


def _outproj_call(o_f, o_b, pm2, o_att, x2, mod, w_out, gdn_norm_w, ln_g, ln_b, seq):
    m = x2.shape[0]
    tm = 512
    blocks = (2 * _nbytes((tm, GDN_WIDTH), F32) + 2 * _nbytes((tm, GDN_WIDTH), BF16)
              + 2 * _nbytes((tm, D_MODEL), F32) + _nbytes((MOD_ROWS, D_MODEL), F32)
              + _nbytes((D_MODEL, D_MODEL), BF16))
    vec = pl.BlockSpec((1, D_MODEL), lambda i: (0, 0))
    return pl.pallas_call(
        _outproj_kernel,
        grid=(m // tm,),
        in_specs=[
            pl.BlockSpec((tm, GDN_WIDTH), lambda i: (i, 0)),
            pl.BlockSpec((tm, GDN_WIDTH), lambda i: (i, 0)),
            pl.BlockSpec((tm, GDN_WIDTH), lambda i: (i, PM_Z // GDN_WIDTH)),
            pl.BlockSpec((tm, ATT_WIDTH), lambda i: (i, 0)),
            pl.BlockSpec((tm, D_MODEL), lambda i: (i, 0)),
            pl.BlockSpec((None, MOD_ROWS, D_MODEL), lambda i: ((i * tm) // seq, 0, 0)),
            pl.BlockSpec((D_MODEL, D_MODEL), lambda i: (0, 0)),
            pl.BlockSpec((1, HEAD_DIM), lambda i: (0, 0)),
            vec, vec,
        ],
        out_specs=pl.BlockSpec((tm, D_MODEL), lambda i: (i, 0)),
        out_shape=jax.ShapeDtypeStruct((m, D_MODEL), F32),
        compiler_params=pltpu.CompilerParams(
            dimension_semantics=("parallel",),
            vmem_limit_bytes=_vmem_limit(blocks, temp_bytes=4 * _nbytes((tm, D_MODEL), F32))),
        name="outproj",
    )(o_f, o_b, pm2, o_att, x2, mod, w_out, gdn_norm_w.reshape(1, HEAD_DIM),
      ln_g.reshape(1, D_MODEL), ln_b.reshape(1, D_MODEL))


POOL_REACH = SUBLANES
assert max(POOL_WINDOWS) // 2 <= POOL_REACH


def _pool_kernel(x_ref, xp_ref, xn_ref, mod_ref, pw_ref, ps_ref, lg_ref, lb_ref, o_ref, *, tt, seq):
    t0 = pl.program_id(1) * tt
    m = mod_ref[...]
    sc = 1.0 + m[MOD_SC_M:MOD_SC_M + 1, :]
    sh = m[MOD_SH_M:MOD_SH_M + 1, :]
    x = x_ref[...]
    h_main = x * sc + sh
    h_prev = xp_ref[...] * sc + sh
    h_next = xn_ref[...] * sc + sh
    r = lax.broadcasted_iota(jnp.int32, (tt, tt), 0)
    c = lax.broadcasted_iota(jnp.int32, (tt, tt), 1)
    pos = t0 + lax.broadcasted_iota(jnp.int32, (tt, 1), 0)
    r8 = lax.broadcasted_iota(jnp.int32, (POOL_REACH, 1), 0)
    outs = []
    for gi, win in enumerate(POOL_WINDOWS):
        back, fwd = win // 2, win - 1 - win // 2
        lanes = slice(gi * POOL_GROUP, (gi + 1) * POOL_GROUP)
        band = jnp.where((c >= r - back) & (c <= r + fwd), 1.0, 0.0).astype(BF16)
        hm = h_main[:, lanes]
        wsum = _dot(band, hm.astype(BF16))
        top = jnp.zeros((POOL_REACH, POOL_GROUP), F32)
        bot = jnp.zeros((POOL_REACH, POOL_GROUP), F32)
        for k in range(POOL_REACH):
            if POOL_REACH - k <= back:
                use = (r8 <= k - POOL_REACH + back) & (t0 - POOL_REACH + k >= 0)
                top = top + jnp.where(use, 1.0, 0.0) * h_prev[k:k + 1, lanes]
            if k < fwd:
                use = (r8 >= k + POOL_REACH - fwd) & (t0 + tt + k < seq)
                bot = bot + jnp.where(use, 1.0, 0.0) * h_next[k:k + 1, lanes]
        wsum = jnp.concatenate([wsum[:POOL_REACH] + top, wsum[POOL_REACH:tt - POOL_REACH],
                                wsum[tt - POOL_REACH:] + bot], axis=0)
        cnt = (jnp.minimum(pos + fwd, seq - 1) - jnp.maximum(pos - back, 0) + 1).astype(F32)
        diff = wsum / cnt - hm
        outs.append(_dot(diff.astype(BF16), pw_ref[gi]))
    mix = jnp.concatenate(outs, axis=1) * ps_ref[...]
    y = DEEPNORM_ALPHA * x + m[MOD_G_M:MOD_G_M + 1, :] * mix
    o_ref[...] = _layer_norm_rows(y, lg_ref[...], lb_ref[...])


def _pool_call(x, mod, pool_w, pool_scale, ln_g, ln_b):
    bsz, seq, _ = x.shape
    tt = 256
    hb = tt // POOL_REACH
    n_halo = seq // POOL_REACH
    vec = pl.BlockSpec((1, D_MODEL), lambda b, i: (0, 0))
    blocks = (2 * _nbytes((tt, D_MODEL), F32) + 2 * _nbytes((POOL_REACH, D_MODEL), F32)
              + _nbytes((MOD_ROWS, D_MODEL), F32) + _nbytes(pool_w.shape, BF16))
    temps = 5 * _nbytes((tt, D_MODEL), F32)
    return pl.pallas_call(
        functools.partial(_pool_kernel, tt=tt, seq=seq),
        grid=(bsz, seq // tt),
        in_specs=[
            pl.BlockSpec((None, tt, D_MODEL), lambda b, i: (b, i, 0)),
            pl.BlockSpec((None, POOL_REACH, D_MODEL), lambda b, i: (b, jnp.maximum(i * hb - 1, 0), 0)),
            pl.BlockSpec((None, POOL_REACH, D_MODEL), lambda b, i: (b, jnp.minimum((i + 1) * hb, n_halo - 1), 0)),
            pl.BlockSpec((None, MOD_ROWS, D_MODEL), lambda b, i: (b, 0, 0)),
            pl.BlockSpec(pool_w.shape, lambda b, i: (0, 0, 0)),
            vec, vec, vec,
        ],
        out_specs=pl.BlockSpec((None, tt, D_MODEL), lambda b, i: (b, i, 0)),
        out_shape=jax.ShapeDtypeStruct((bsz, seq, D_MODEL), F32),
        compiler_params=pltpu.CompilerParams(
            dimension_semantics=("parallel", "parallel"),
            vmem_limit_bytes=_vmem_limit(blocks, temp_bytes=temps)),
        name="pool",
    )(x, x, x, mod, pool_w, pool_scale.reshape(1, D_MODEL), ln_g.reshape(1, D_MODEL), ln_b.reshape(1, D_MODEL))


def _mlp_kernel(x_ref, mod_ref, w1_ref, w2_ref, lg_ref, lb_ref, o_ref, h_ref, acc_ref):
    j = pl.program_id(1)

    @pl.when(j == 0)
    def _():
        m = mod_ref[...]
        h = x_ref[...] * (1.0 + m[MOD_SC_F:MOD_SC_F + 1, :]) + m[MOD_SH_F:MOD_SH_F + 1, :]
        h_ref[...] = h.astype(BF16)
        acc_ref[...] = jnp.zeros_like(acc_ref)

    u = jnp.maximum(_dot(h_ref[...], w1_ref[...]), 0.0)
    acc_ref[...] += _dot((u * u).astype(BF16), w2_ref[...])

    @pl.when(j == pl.num_programs(1) - 1)
    def _():
        m = mod_ref[...]
        y = DEEPNORM_ALPHA * x_ref[...] + m[MOD_G_F:MOD_G_F + 1, :] * acc_ref[...]
        o_ref[...] = _layer_norm_rows(y, lg_ref[...], lb_ref[...])


def _mlp_call(x2, mod, w1, w2, layer, ln_g, ln_b, seq):
    m = x2.shape[0]
    tm, tf = 512, 1024
    vec = pl.BlockSpec((1, D_MODEL), lambda i, j: (0, 0))
    blocks = (2 * _nbytes((tm, D_MODEL), F32) + _nbytes((MOD_ROWS, D_MODEL), F32)
              + 2 * _nbytes((D_MODEL, tf), BF16))
    scratch = _nbytes((tm, D_MODEL), BF16) + _nbytes((tm, D_MODEL), F32)
    temps = 2 * _nbytes((tm, tf), F32) + _nbytes((tm, D_MODEL), F32)
    return pl.pallas_call(
        _mlp_kernel,
        grid=(m // tm, D_FF // tf),
        in_specs=[
            pl.BlockSpec((tm, D_MODEL), lambda i, j: (i, 0)),
            pl.BlockSpec((None, MOD_ROWS, D_MODEL), lambda i, j: ((i * tm) // seq, 0, 0)),
            pl.BlockSpec((None, D_MODEL, tf), lambda i, j: (layer, 0, j)),
            pl.BlockSpec((None, tf, D_MODEL), lambda i, j: (layer, j, 0)),
            vec, vec,
        ],
        out_specs=pl.BlockSpec((tm, D_MODEL), lambda i, j: (i, 0)),
        out_shape=jax.ShapeDtypeStruct((m, D_MODEL), F32),
        scratch_shapes=[pltpu.VMEM((tm, D_MODEL), BF16), pltpu.VMEM((tm, D_MODEL), F32)],
        compiler_params=pltpu.CompilerParams(
            dimension_semantics=("parallel", "arbitrary"),
            vmem_limit_bytes=_vmem_limit(blocks, scratch, temps)),
        name="mlp",
    )(x2, mod, w1, w2, ln_g.reshape(1, D_MODEL), ln_b.reshape(1, D_MODEL))


def _mod_table(cond_rows):
    bsz = cond_rows.shape[0]
    t = cond_rows.reshape(bsz, 6, D_MODEL)
    return jnp.pad(t, ((0, 0), (0, MOD_ROWS - 6), (0, 0)))


def _pack_weights(w_in, conv_w, a_log, dt_bias, w_out, pool_w, mlp_w1, mlp_w2):
    wi = w_in[0]
    w_main = jnp.concatenate(
        [wi[:, :OFF_BETA], wi[:, OFF_AQ:]], axis=1).astype(BF16)
    w_gate = jnp.pad(wi[:, OFF_BETA:OFF_AQ], ((0, 0), (0, LANES - 2 * N_GATE_COLS))).astype(BF16)
    conv_w8 = jnp.pad(conv_w[0], ((0, SUBLANES - CONV_W), (0, 0)))
    gate_params = jnp.zeros((SUBLANES, LANES), F32)
    gate_params = gate_params.at[0, N_GATE_COLS:2 * N_GATE_COLS].set(dt_bias[0].reshape(-1))
    gate_params = gate_params.at[1, N_GATE_COLS:2 * N_GATE_COLS].set(a_log[0].reshape(-1))
    return dict(w_main=w_main, w_gate=w_gate, conv_w8=conv_w8, gate_params=gate_params,
                w_out=w_out[0].astype(BF16), pool_w=pool_w[0].astype(BF16),
                w1=mlp_w1.astype(BF16), w2=mlp_w2.astype(BF16))


def _trunk(x, cond, pk, gdn_norm_w, q_norm_w, k_norm_w, pool_scale, ln_g, ln_b):
    bsz, seq, _ = x.shape
    m = bsz * seq
    mod0 = _mod_table(cond[0])
    mod1 = _mod_table(cond[1])
    x2 = x.reshape(m, D_MODEL)

    cos, sin_up, sin_dn = _rope_tables(seq)
    pm2, gates2, qt, kr2, vt = _inproj_call(x2, mod0, pk["w_main"], pk["w_gate"], cos, sin_up, sin_dn,
                                            q_norm_w[0], k_norm_w[0], bsz, seq)
    pm = pm2.reshape(bsz, seq, PM_WIDTH)
    gq, gk, gv, gcol, grow = _gdnprep_call(pm, gates2.reshape(bsz, seq, LANES), pk["conv_w8"], pk["gate_params"])
    o_f, o_b = _gdn_call(gq, gk, gv, gcol, grow)
    o_att = _flash_call(qt, kr2.reshape(bsz, seq, ATT_KV_WIDTH), vt)
    x2 = _outproj_call(o_f.reshape(m, GDN_WIDTH), o_b.reshape(m, GDN_WIDTH), pm2, o_att.reshape(m, ATT_WIDTH),
                       x2, mod0, pk["w_out"], gdn_norm_w[0], ln_g[0, 0], ln_b[0, 0], seq)
    x2 = _mlp_call(x2, mod0, pk["w1"], pk["w2"], 0, ln_g[0, 1], ln_b[0, 1], seq)

    x3 = _pool_call(x2.reshape(bsz, seq, D_MODEL), mod1, pk["pool_w"], pool_scale[0], ln_g[1, 0], ln_b[1, 0])
    x2 = _mlp_call(x3.reshape(m, D_MODEL), mod1, pk["w1"], pk["w2"], 1, ln_g[1, 1], ln_b[1, 1], seq)
    return x2.reshape(bsz, seq, D_MODEL)


def kernel(x_prompt, x_sample, c_prompt, c_sample, w_in, conv_w, a_log, dt_bias, gdn_norm_w, q_norm_w,
           k_norm_w, w_out, pool_w, pool_scale, mlp_w1, mlp_w2, ada_w, ada_b, ln_g, ln_b):
    bp, bs = c_prompt.shape[0], c_sample.shape[0]
    c_all = jnp.concatenate([c_prompt, c_sample], axis=0)
    c_all = jnp.pad(c_all, ((0, (-c_all.shape[0]) % SUBLANES), (0, 0)))
    cond = _ada_call(c_all, ada_w, ada_b)
    pk = _pack_weights(w_in, conv_w, a_log, dt_bias, w_out, pool_w, mlp_w1, mlp_w2)
    y_prompt = _trunk(x_prompt, cond[:, :bp], pk, gdn_norm_w, q_norm_w, k_norm_w, pool_scale, ln_g, ln_b)
    y_sample = _trunk(x_sample, cond[:, bp:bp + bs], pk, gdn_norm_w, q_norm_w, k_norm_w, pool_scale, ln_g, ln_b)
    return (y_prompt, y_sample)
```
